```python
import math
import jax, jax.numpy as jnp
from jax import lax
import numpy as np

D_MODEL = 4096
BATCH = 2
SEQ = 4096
DEPTH = 1

HEAD_DIM = 128
MIX_WIDTH = D_MODEL
A_HEADS = MIX_WIDTH // (2 * HEAD_DIM)
A_KV_HEADS = A_HEADS // 4
A_GROUP = A_HEADS // A_KV_HEADS
DILATED_BRANCHES = ((128, 1), (512, 4), (2048, 16))
B_HEADS = MIX_WIDTH // (2 * HEAD_DIM)
IDX_HEADS = 16
IDX_DIM = 64
DSA_TOPK_MAX = 256
Q_BLK = 128
PEER_HEADS = 8
PEER_NKEYS = 128
PEER_EXPERTS = PEER_NKEYS * PEER_NKEYS
PEER_QDIM = 256
PEER_HALF = PEER_QDIM // 2
PEER_TOPK = 16
PEER_TOK_BLK = 32
ROPE_THETA = 10000.0
NORM_EPS = 1e-6
NEG = -1e30
ADA_CHUNKS = 6

QA_COLS = A_HEADS * HEAD_DIM
KA_COLS = A_KV_HEADS * HEAD_DIM
VA_COLS = A_KV_HEADS * HEAD_DIM
QB_COLS = B_HEADS * HEAD_DIM
KB_COLS = HEAD_DIM
VB_COLS = HEAD_DIM
QI_COLS = IDX_HEADS * IDX_DIM
KI_COLS = IDX_DIM
WI_COLS = IDX_HEADS
IN_COLS = QA_COLS + KA_COLS + VA_COLS + QB_COLS + KB_COLS + VB_COLS + QI_COLS + KI_COLS + WI_COLS
IN_SPLITS = (QA_COLS,
             QA_COLS + KA_COLS,
             QA_COLS + KA_COLS + VA_COLS,
             QA_COLS + KA_COLS + VA_COLS + QB_COLS,
             QA_COLS + KA_COLS + VA_COLS + QB_COLS + KB_COLS,
             QA_COLS + KA_COLS + VA_COLS + QB_COLS + KB_COLS + VB_COLS,
             QA_COLS + KA_COLS + VA_COLS + QB_COLS + KB_COLS + VB_COLS + QI_COLS,
             QA_COLS + KA_COLS + VA_COLS + QB_COLS + KB_COLS + VB_COLS + QI_COLS + KI_COLS)

kernel_name = "hybrid_dilated_dsa_peer_block"


def rms_norm(x, g):
    xf = x.astype(jnp.float32)
    y = xf * lax.rsqrt(jnp.mean(xf * xf, axis=-1, keepdims=True) + NORM_EPS)
    return (y * g.astype(jnp.float32)).astype(x.dtype)


def rope_tables(positions, dim):
    inv = jnp.power(ROPE_THETA, -jnp.arange(0, dim, 2, dtype=jnp.float32) / dim)
    ang = positions.astype(jnp.float32)[..., None] * inv
    return jnp.cos(ang)[:, :, None, :], jnp.sin(ang)[:, :, None, :]


def apply_rope(t, cos, sin):
    tf = t.astype(jnp.float32)
    half = t.shape[-1] // 2
    t1, t2 = tf[..., :half], tf[..., half:]
    return jnp.concatenate([t1 * cos - t2 * sin, t2 * cos + t1 * sin], axis=-1).astype(t.dtype)


def dilated_branch(q, k, v, window, dil):
    b_, s_, kvh, grp, dh = q.shape
    blk = window // dil
    span = dil * blk
    s_pad = -(-s_ // span) * span
    pad = s_pad - s_
    m = s_pad // dil
    nb = m // blk

    def split(t):
        t = jnp.pad(t, [(0, 0), (0, pad)] + [(0, 0)] * (t.ndim - 2))
        t = t.reshape((b_, m, dil) + t.shape[2:])
        t = jnp.moveaxis(t, 2, 1)
        return t.reshape((b_, dil, nb, blk) + t.shape[3:])

    def with_prev(t):
        prev = jnp.pad(t, [(0, 0), (0, 0), (1, 0)] + [(0, 0)] * (t.ndim - 3))[:, :, :-1]
        return jnp.concatenate([prev, t], axis=3)

    qs = split(q)
    kc = with_prev(split(k))
    vc = with_prev(split(v))
    s = jnp.einsum('brnqkgd,brnjkd->brnkgqj', qs, kc).astype(jnp.float32) * (dh ** -0.5)
    qi = jnp.arange(blk)[:, None]
    kj = jnp.arange(2 * blk)[None, :]
    dist = qi + blk - kj
    band = (dist >= 0) & (dist <= blk)
    first = (jnp.arange(nb) == 0)[:, None, None]
    mask = band[None] & ~(first & (kj < blk)[None])
    s = jnp.where(mask[None, None, :, None, None], s, NEG)
    mx = jnp.max(s, axis=-1, keepdims=True)
    p = jnp.exp(s - mx)
    l = jnp.sum(p, axis=-1, keepdims=True)
    o = jnp.einsum('brnkgqj,brnjkd->brnqkgd', (p / l).astype(v.dtype), vc)
    lse = jnp.moveaxis((mx + jnp.log(l))[..., 0], -1, 3)

    def merge(t):
        t = t.reshape((b_, dil, m) + t.shape[4:])
        t = jnp.moveaxis(t, 1, 2)
        return t.reshape((b_, s_pad) + t.shape[3:])[:, :s_]

    return merge(o), merge(lse)


def dilated_mixture(q, k, v):
    b_, s_ = q.shape[:2]
    qg = q.reshape(b_, s_, A_KV_HEADS, A_GROUP, HEAD_DIM)
    outs, lses = [], []
    for window, dil in DILATED_BRANCHES:
        o, lse = dilated_branch(qg, k, v, window, dil)
        outs.append(o)
        lses.append(lse)
    wts = jax.nn.softmax(jnp.stack(lses, axis=0), axis=0)
    o = jnp.einsum('ibskg,ibskgd->bskgd', wts.astype(q.dtype), jnp.stack(outs, axis=0))
    return o.reshape(b_, s_, A_HEADS * HEAD_DIM)


def dsa_attention(q, k, v, qi, ki, wi):
    b_, s_ = q.shape[:2]
    topk = min(DSA_TOPK_MAX, s_ // 4)
    nb = s_ // Q_BLK

    def blocks(t):
        return jnp.moveaxis(t.reshape((b_, nb, Q_BLK) + t.shape[2:]), 1, 0)

    tpos = jnp.arange(s_, dtype=jnp.int32).reshape(nb, Q_BLK)
    key_pos = jnp.arange(s_, dtype=jnp.int32)
    ki_f = ki.astype(jnp.float32)

    def one_block(args):
        qb, qib, wib, tb = args
        rel = jax.nn.relu(jnp.einsum('bqhd,bsd->bqhs', qib.astype(jnp.float32), ki_f) * (IDX_DIM ** -0.5))
        score = jnp.einsum('bqh,bqhs->bqs', wib.astype(jnp.float32) * (IDX_HEADS ** -0.5), rel)
        causal = key_pos[None, :] <= tb[:, None]
        score = jnp.where(causal[None], score, -jnp.inf)
        _, sel = lax.top_k(score, topk)
        flat = sel.reshape(b_, -1, 1)
        kg = jnp.take_along_axis(k, flat, axis=1).reshape(b_, Q_BLK, topk, HEAD_DIM)
        vg = jnp.take_along_axis(v, flat, axis=1).reshape(b_, Q_BLK, topk, HEAD_DIM)
        s = jnp.einsum('bqhd,bqkd->bqhk', qb, kg).astype(jnp.float32) * (HEAD_DIM ** -0.5)
        valid = sel <= tb[None, :, None]
        s = jnp.where(valid[:, :, None, :], s, NEG)
        p = jax.nn.softmax(s, axis=-1).astype(vg.dtype)
        return jnp.einsum('bqhk,bqkd->bqhd', p, vg)

    out = lax.map(one_block, (blocks(q), blocks(qi), blocks(wi), tpos))
    return jnp.moveaxis(out, 0, 1).reshape(b_, s_, B_HEADS * HEAD_DIM)


def mixing(h, positions, w_in, w_out):
    b_, s_, _ = h.shape
    proj = h @ w_in
    qa, ka, va, qb, kb, vb, qi, ki, wi = jnp.split(proj, IN_SPLITS, axis=-1)
    cos_h, sin_h = rope_tables(positions, HEAD_DIM)
    cos_i, sin_i = rope_tables(positions, IDX_DIM)
    qa = apply_rope(qa.reshape(b_, s_, A_HEADS, HEAD_DIM), cos_h, sin_h)
    ka = apply_rope(ka.reshape(b_, s_, A_KV_HEADS, HEAD_DIM), cos_h, sin_h)
    va = va.reshape(b_, s_, A_KV_HEADS, HEAD_DIM)
    o_a = dilated_mixture(qa, ka, va)
    qb = apply_rope(qb.reshape(b_, s_, B_HEADS, HEAD_DIM), cos_h, sin_h)
    kb = apply_rope(kb[:, :, None, :], cos_h, sin_h)[:, :, 0]
    qi = apply_rope(qi.reshape(b_, s_, IDX_HEADS, IDX_DIM), cos_i, sin_i)
    ki = apply_rope(ki[:, :, None, :], cos_i, sin_i)[:, :, 0]
    o_b = dsa_attention(qb, kb, vb, qi, ki, wi)
    return jnp.concatenate([o_a, o_b], axis=-1) @ w_out


def peer(h, w_q, sub_keys, u_tab, v_tab):
    b_, s_, d = h.shape
    t_ = b_ * s_
    x = h.reshape(t_, d)
    q = (x @ w_q).reshape(t_, PEER_HEADS, 2, PEER_HALF).astype(jnp.float32)
    s = jnp.einsum('thpc,hpnc->thpn', q, sub_keys.astype(jnp.float32))
    v_half, i_half = lax.top_k(s, PEER_TOPK)
    cand = v_half[:, :, 0, :, None] + v_half[:, :, 1, None, :]
    cand_idx = i_half[:, :, 0, :, None] * PEER_NKEYS + i_half[:, :, 1, None, :]
    top_s, pos = lax.top_k(cand.reshape(t_, PEER_HEADS, -1), PEER_TOPK)
    experts = jnp.take_along_axis(cand_idx.reshape(t_, PEER_HEADS, -1), pos, axis=-1)
    gates = jax.nn.softmax(top_s, axis=-1)
    nbk = t_ // PEER_TOK_BLK

    def one_block(args):
        xb, eb, gb = args
        u = jnp.take(u_tab, eb, axis=0)
        a = jnp.einsum('td,thkd->thk', xb, u).astype(jnp.float32)
        act = (jax.nn.gelu(a, approximate=False) * gb).astype(xb.dtype)
        vv = jnp.take(v_tab, eb, axis=0)
        return jnp.einsum('thk,thkd->td', act, vv)

    out = lax.map(one_block, (x.reshape(nbk, PEER_TOK_BLK, d),
                              experts.reshape(nbk, PEER_TOK_BLK, PEER_HEADS, PEER_TOPK),
                              gates.reshape(nbk, PEER_TOK_BLK, PEER_HEADS, PEER_TOPK)))
    return out.reshape(b_, s_, d)


def setup_inputs(seed: int = 0) -> dict:
    key = jax.random.key(seed)
    ks = jax.random.split(key, 16)
    f32 = jnp.float32
    x = jax.random.normal(ks[0], (BATCH, SEQ, D_MODEL), f32)
    c = jax.random.normal(ks[1], (BATCH, D_MODEL), f32)
    offsets = jax.random.randint(ks[2], (BATCH, 1), 0, 1024, dtype=jnp.int32)
    positions = offsets + jnp.arange(SEQ, dtype=jnp.int32)[None, :]
    ln1_g = 1.0 + 0.01 * jax.random.normal(ks[3], (DEPTH, D_MODEL), f32)
    ln2_g = 1.0 + 0.01 * jax.random.normal(ks[4], (DEPTH, D_MODEL), f32)
    w_ada = 0.5 * (D_MODEL ** -0.5) * jax.random.normal(ks[5], (DEPTH, D_MODEL, ADA_CHUNKS * D_MODEL), f32)
    b_ada = 0.02 * jax.random.normal(ks[6], (DEPTH, ADA_CHUNKS * D_MODEL), f32)
    w_in = (D_MODEL ** -0.5) * jax.random.normal(ks[7], (DEPTH, D_MODEL, IN_COLS), f32)
    w_out = (MIX_WIDTH ** -0.5) * jax.random.normal(ks[8], (DEPTH, MIX_WIDTH, D_MODEL), f32)
    peer_wq = (D_MODEL ** -0.5) * jax.random.normal(ks[9], (DEPTH, D_MODEL, PEER_HEADS * PEER_QDIM), f32)
    peer_sub_keys = (PEER_HALF ** -0.5) * jax.random.normal(ks[10], (DEPTH, PEER_HEADS, 2, PEER_NKEYS, PEER_HALF), f32)
    peer_u = (D_MODEL ** -0.5) * jax.random.normal(ks[11], (DEPTH, PEER_EXPERTS, D_MODEL), f32)
    peer_v = (PEER_HEADS ** -0.5) * jax.random.normal(ks[12], (DEPTH, PEER_EXPERTS, D_MODEL), f32)
    lnf_g = 1.0 + 0.01 * jax.random.normal(ks[13], (D_MODEL,), f32)
    return {"x": x, "c": c, "positions": positions, "ln1_g": ln1_g, "ln2_g": ln2_g,
            "w_ada": w_ada, "b_ada": b_ada, "w_in": w_in, "w_out": w_out,
            "peer_wq": peer_wq, "peer_sub_keys": peer_sub_keys, "peer_u": peer_u,
            "peer_v": peer_v, "lnf_g": lnf_g}


def reference(x, c, positions, ln1_g, ln2_g, w_ada, b_ada, w_in, w_out,
              peer_wq, peer_sub_keys, peer_u, peer_v, lnf_g):
    for layer in range(DEPTH):
        mod = jax.nn.silu(c) @ w_ada[layer] + b_ada[layer]
        shift1, scale1, gate1, shift2, scale2, gate2 = [m[:, None, :] for m in jnp.split(mod, ADA_CHUNKS, axis=-1)]
        h = rms_norm(x, ln1_g[layer]) * (1.0 + scale1) + shift1
        x = x + gate1 * mixing(h, positions, w_in[layer], w_out[layer])
        h = rms_norm(x, ln2_g[layer]) * (1.0 + scale2) + shift2
        x = x + gate2 * peer(h, peer_wq[layer], peer_sub_keys[layer], peer_u[layer], peer_v[layer])
    return rms_norm(x, lnf_g)
```

```python
import functools
import math

import jax
import jax.numpy as jnp
from jax import lax
from jax.experimental import pallas as pl
from jax.experimental.pallas import tpu as pltpu

F32 = jnp.float32
BF16 = jnp.bfloat16

HEAD_DIM = 128
A_HEADS = 16
A_KV_HEADS = 4
A_GROUP = A_HEADS // A_KV_HEADS
DILATED_BRANCHES = ((128, 1), (512, 4), (2048, 16))
WIN_BLK = 128
B_HEADS = 16
IDX_HEADS = 16
IDX_DIM = 64
DSA_TOPK_MAX = 256
Q_BLK = 128
PEER_HEADS = 8
PEER_NKEYS = 128
PEER_HALF = 128
PEER_TOPK = 16
ROPE_THETA = 10000.0
NORM_EPS = 1e-6
NEG = -1e30
ADA_CHUNKS = 6
INT_MIN = -(2 ** 31)

LANES = 128
VMEM_LIMIT = 56 * 1024 * 1024

PROJ_TILE = 512
QA_OFF = 0
QB_OFF = QA_OFF + A_HEADS * HEAD_DIM
KA_OFF = QB_OFF + B_HEADS * HEAD_DIM
MIX_OFF = KA_OFF + A_KV_HEADS * HEAD_DIM
KB_OFF = MIX_OFF
KIKI_OFF = MIX_OFF + LANES
VB_OFF = MIX_OFF + 2 * LANES
WI_OFF = MIX_OFF + 3 * LANES
QI_OFF = MIX_OFF + PROJ_TILE
VA_OFF = QI_OFF + IDX_HEADS * IDX_DIM
PROJ_COLS = VA_OFF + A_KV_HEADS * HEAD_DIM


def _params(semantics):
    return pltpu.CompilerParams(dimension_semantics=semantics, vmem_limit_bytes=VMEM_LIMIT)


def _tile(n, pref):
    return pref if n % pref == 0 else n


def _norm_mod(x, g, scale, shift):
    xf = x.astype(F32)
    y = xf * lax.rsqrt(jnp.mean(xf * xf, axis=-1, keepdims=True) + NORM_EPS)
    return (y * g) * (1.0 + scale) + shift


def _ada_kernel(ct_ref, w_ref, b_ref, o_ref, *, nbatch):
    @pl.when(pl.program_id(1) == 0)
    def _():
        o_ref[...] = jnp.broadcast_to(b_ref[...], o_ref.shape)

    ct = ct_ref[...]
    s = ct * jax.nn.sigmoid(ct)
    w = w_ref[...]
    rows = [jnp.sum(w * s[:, b:b + 1], axis=0, keepdims=True) for b in range(nbatch)]
    o_ref[...] += jnp.concatenate(rows, axis=0)


def _ada(c, w_ada, b_ada):
    nbatch, d = c.shape
    n = w_ada.shape[1]
    tk, tn = _tile(d, 512), _tile(n, 2048)
    return pl.pallas_call(
        functools.partial(_ada_kernel, nbatch=nbatch),
        grid=(n // tn, d // tk),
        in_specs=[pl.BlockSpec((tk, nbatch), lambda j, k: (k, 0)),
                  pl.BlockSpec((tk, tn), lambda j, k: (k, j)),
                  pl.BlockSpec((1, tn), lambda j, k: (0, j))],
        out_specs=pl.BlockSpec((nbatch, tn), lambda j, k: (0, j)),
        out_shape=jax.ShapeDtypeStruct((nbatch, n), F32),
        compiler_params=_params(("parallel", "arbitrary")),
        name="ada",
    )(c.T, w_ada, b_ada.reshape(1, n))


def _rope_kernel(pos_ref, ch_ref, sh_ref, ci_ref, sa_ref, sb_ref):
    pos = pos_ref[...].astype(F32)
    lane = lax.broadcasted_iota(jnp.int32, (1, LANES), 1)
    expo_h = -((2 * (lane & 63)).astype(F32)) / HEAD_DIM
    expo_i = -((2 * (lane & 31)).astype(F32)) / IDX_DIM
    inv = jnp.power(ROPE_THETA, jnp.where(lane < 64, expo_h, expo_i))
    ang = pos * inv
    c = jnp.cos(ang)
    s = jnp.sin(ang)
    ch_ref[...] = jnp.where(lane < 64, c, pltpu.roll(c, 64, 1))
    sh_ref[...] = jnp.where(lane < 64, -s, pltpu.roll(s, 64, 1))
    grp = lane >> 5
    c64, c96, c32 = pltpu.roll(c, 64, 1), pltpu.roll(c, 96, 1), pltpu.roll(c, 32, 1)
    s64, s96, s32 = pltpu.roll(s, 64, 1), pltpu.roll(s, 96, 1), pltpu.roll(s, 32, 1)
    ci_ref[...] = jnp.where(grp == 0, c64, jnp.where(grp == 1, c96, jnp.where(grp == 2, c, c32)))
    sa_ref[...] = jnp.where(grp == 0, -s64, jnp.where(grp == 2, -s, 0.0))
    sb_ref[...] = jnp.where(grp == 1, s96, jnp.where(grp == 3, s32, 0.0))


def _rope_tables(positions):
    nbatch, seq = positions.shape
    ts = _tile(seq, 1024)
    spec = pl.BlockSpec((None, ts, LANES), lambda b, i: (b, i, 0))
    shape = jax.ShapeDtypeStruct((nbatch, seq, LANES), F32)
    return pl.pallas_call(
        _rope_kernel,
        grid=(nbatch, seq // ts),
        in_specs=[pl.BlockSpec((None, ts, 1), lambda b, i: (b, i, 0))],
        out_specs=[spec] * 5,
        out_shape=[shape] * 5,
        compiler_params=_params(("parallel", "parallel")),
        name="rope",
    )(positions.reshape(nbatch, seq, 1))


def _rope_head(a, ch, sh):
    return a * ch + pltpu.roll(a, 64, 1) * sh


def _rope_idx(a, ci, sa, sb):
    return a * ci + pltpu.roll(a, 96, 1) * sa + pltpu.roll(a, 32, 1) * sb


def _in_proj_kernel(x_ref, mod_ref, g_ref, w_ref, ch_ref, sh_ref, ci_ref, sa_ref, sb_ref,
                    p_ref, aux_ref, h_scr, acc_scr):
    j = pl.program_id(2)

    @pl.when(j == 0)
    def _():
        h_scr[...] = _norm_mod(x_ref[...], g_ref[...], mod_ref[1:2, :], mod_ref[0:1, :]).astype(BF16)

    acc_scr[...] = jnp.dot(h_scr[...], w_ref[...], preferred_element_type=F32)
    nblk = PROJ_TILE // LANES
    qscale = HEAD_DIM ** -0.5

    def blk(q):
        return acc_scr[:, q * LANES:(q + 1) * LANES]

    def put(q, val):
        p_ref[:, q * LANES:(q + 1) * LANES] = val.astype(BF16)

    @pl.when(j < KA_OFF // PROJ_TILE)
    def _():
        for q in range(nblk):
            put(q, _rope_head(blk(q), ch_ref[...], sh_ref[...]) * qscale)

    @pl.when(j == KA_OFF // PROJ_TILE)
    def _():
        for q in range(nblk):
            put(q, _rope_head(blk(q), ch_ref[...], sh_ref[...]))

    @pl.when(j == MIX_OFF // PROJ_TILE)
    def _():
        put(0, _rope_head(blk(0), ch_ref[...], sh_ref[...]))
        put(1, _rope_idx(blk(1), ci_ref[...], sa_ref[...], sb_ref[...]))
        put(2, blk(2))
        put(3, blk(3))
        aux_ref[...] = blk(3)

    @pl.when((j > MIX_OFF // PROJ_TILE) & (j < VA_OFF // PROJ_TILE))
    def _():
        for q in range(nblk):
            put(q, _rope_idx(blk(q), ci_ref[...], sa_ref[...], sb_ref[...]))

    @pl.when(j == VA_OFF // PROJ_TILE)
    def _():
        for q in range(nblk):
            put(q, blk(q))


def _permute_w_in(w_in):
    d = w_in.shape[0]
    w = w_in.astype(BF16)
    sizes = (A_HEADS * HEAD_DIM, A_KV_HEADS * HEAD_DIM, A_KV_HEADS * HEAD_DIM, B_HEADS * HEAD_DIM,
             HEAD_DIM, HEAD_DIM, IDX_HEADS * IDX_DIM, IDX_DIM, IDX_HEADS)
    offs = [0]
    for sz in sizes:
        offs.append(offs[-1] + sz)
    assert w.shape[1] == offs[-1]
    qa, ka, va, qb, kb, vb, qi, ki, wi = [w[:, offs[i]:offs[i + 1]] for i in range(len(sizes))]
    pad = jnp.zeros((d, LANES - IDX_HEADS), BF16)
    out = jnp.concatenate([qa, qb, ka, kb, ki, ki, vb, wi, pad, qi, va], axis=1)
    assert out.shape[1] == PROJ_COLS
    return out


def _in_proj(x, mod3, g, w_perm, tables):
    nbatch, seq, d = x.shape
    tm = _tile(seq, 512)
    tab_spec = pl.BlockSpec((None, tm, LANES), lambda b, i, j: (b, i, 0))
    return pl.pallas_call(
        _in_proj_kernel,
        grid=(nbatch, seq // tm, PROJ_COLS // PROJ_TILE),
        in_specs=[pl.BlockSpec((None, tm, d), lambda b, i, j: (b, i, 0)),
                  pl.BlockSpec((None, ADA_CHUNKS, d), lambda b, i, j: (b, 0, 0)),
                  pl.BlockSpec((1, d), lambda b, i, j: (0, 0)),
                  pl.BlockSpec((d, PROJ_TILE), lambda b, i, j: (0, j))] + [tab_spec] * 5,
        out_specs=[pl.BlockSpec((None, tm, PROJ_TILE), lambda b, i, j: (b, i, j)),
                   pl.BlockSpec((None, tm, LANES), lambda b, i, j: (b, i, 0))],
        out_shape=[jax.ShapeDtypeStruct((nbatch, seq, PROJ_COLS), BF16),
                   jax.ShapeDtypeStruct((nbatch, seq, LANES), F32)],
        scratch_shapes=[pltpu.VMEM((tm, d), BF16), pltpu.VMEM((tm, PROJ_TILE), F32)],
        compiler_params=_params(("parallel", "parallel", "arbitrary")),
        name="in_proj",
    )(x, mod3, g.reshape(1, d), w_perm, *tables)


def _attn_a_kernel(*refs, nblocks, has_prev, final):
    q_ref, k_ref, v_ref = refs[:3]
    pos = 3
    if has_prev:
        op_ref, lp_ref = refs[pos:pos + 2]
        pos += 2
    o_ref = refs[pos]
    l_ref = None if final else refs[pos + 1]
    bias_scr = refs[-1]

    rows = A_GROUP * WIN_BLK
    row = lax.broadcasted_iota(jnp.int32, (rows, 2 * WIN_BLK), 0) & (WIN_BLK - 1)
    col = lax.broadcasted_iota(jnp.int32, (rows, 2 * WIN_BLK), 1)
    band = (col >= row) & (col <= row + WIN_BLK)
    bias_scr[0] = jnp.where(band & (col >= WIN_BLK), 0.0, NEG)
    bias_scr[1] = jnp.where(band, 0.0, NEG)

    def body(n, carry):
        r0 = pl.multiple_of(n * WIN_BLK, WIN_BLK)
        rp = pl.multiple_of(jnp.maximum(n - 1, 0) * WIN_BLK, WIN_BLK)
        qb = q_ref[pl.ds(r0, WIN_BLK), :]
        q4 = jnp.concatenate([qb[:, g * HEAD_DIM:(g + 1) * HEAD_DIM] for g in range(A_GROUP)], axis=0)
        kc = jnp.concatenate([k_ref[pl.ds(rp, WIN_BLK), :], k_ref[pl.ds(r0, WIN_BLK), :]], axis=0)
        vc = jnp.concatenate([v_ref[pl.ds(rp, WIN_BLK), :], v_ref[pl.ds(r0, WIN_BLK), :]], axis=0)
        s = lax.dot_general(q4, kc, (((1,), (1,)), ((), ())), preferred_element_type=F32)
        s = s + bias_scr[jnp.minimum(n, 1)]
        m = jnp.max(s, axis=-1, keepdims=True)
        p = jnp.exp(s - m)
        l = jnp.sum(p, axis=-1, keepdims=True)
        o = jnp.dot(p.astype(BF16), vc, preferred_element_type=F32) / l
        lse = m + jnp.log(l)
        for g in range(A_GROUP):
            og = o[g * WIN_BLK:(g + 1) * WIN_BLK]
            lg = lse[g * WIN_BLK:(g + 1) * WIN_BLK]
            cols = slice(g * HEAD_DIM, (g + 1) * HEAD_DIM)
            if has_prev:
                opg = op_ref[pl.ds(r0, WIN_BLK), cols]
                lpg = lp_ref[pl.ds(r0, WIN_BLK), g:g + 1]
                mx = jnp.maximum(lpg, lg)
                tot = mx + jnp.log(jnp.exp(lpg - mx) + jnp.exp(lg - mx))
                og = opg * jnp.exp(lpg - tot) + og * jnp.exp(lg - tot)
                lg = tot
            o_ref[pl.ds(r0, WIN_BLK), cols] = og.astype(o_ref.dtype)
            if not final:
                l_ref[pl.ds(r0, WIN_BLK), g:g + 1] = lg
        return carry

    lax.fori_loop(0, nblocks, body, 0)


def _attn_a_branch(proj, dil, prev, final):
    nbatch, seq, _ = proj.shape
    rows = seq // dil
    assert rows % WIN_BLK == 0
    pv = proj.reshape(nbatch, rows, dil * PROJ_COLS)
    qw = A_GROUP * HEAD_DIM
    in_specs = [
        pl.BlockSpec((None, rows, qw), lambda b, r, h: (b, 0, r * (PROJ_COLS // qw) + QA_OFF // qw + h)),
        pl.BlockSpec((None, rows, HEAD_DIM),
                     lambda b, r, h: (b, 0, r * (PROJ_COLS // HEAD_DIM) + KA_OFF // HEAD_DIM + h)),
        pl.BlockSpec((None, rows, HEAD_DIM),
                     lambda b, r, h: (b, 0, r * (PROJ_COLS // HEAD_DIM) + VA_OFF // HEAD_DIM + h)),
    ]
    o_spec = pl.BlockSpec((None, rows, qw), lambda b, r, h: (b, 0, r * A_KV_HEADS + h))
    l_spec = pl.BlockSpec((None, None, rows, LANES), lambda b, r, h: (b, h, 0, r))
    args = [pv, pv, pv]
    if prev is not None:
        in_specs += [o_spec, l_spec]
        args += [prev[0].reshape(nbatch, rows, dil * A_HEADS * HEAD_DIM),
                 prev[1].reshape(nbatch, A_KV_HEADS, rows, dil * LANES)]
    o_shape = jax.ShapeDtypeStruct((nbatch, rows, dil * A_HEADS * HEAD_DIM), BF16 if final else F32)
    l_shape = jax.ShapeDtypeStruct((nbatch, A_KV_HEADS, rows, dil * LANES), F32)
    outs = pl.pallas_call(
        functools.partial(_attn_a_kernel, nblocks=rows // WIN_BLK, has_prev=prev is not None, final=final),
        grid=(nbatch, dil, A_KV_HEADS),
        in_specs=in_specs,
        out_specs=[o_spec] if final else [o_spec, l_spec],
        out_shape=[o_shape] if final else [o_shape, l_shape],
        scratch_shapes=[pltpu.VMEM((2, A_GROUP * WIN_BLK, 2 * WIN_BLK), F32)],
        compiler_params=_params(("parallel", "parallel", "parallel")),
        name=f"attn_a_d{dil}",
    )(*args)
    o = outs[0].reshape(nbatch, seq, A_HEADS * HEAD_DIM)
    if final:
        return o
    return o, outs[1].reshape(nbatch, A_KV_HEADS, seq, LANES)


def _attn_a(proj):
    state = None
    for idx, (window, dil) in enumerate(DILATED_BRANCHES):
        assert window // dil == WIN_BLK
        final = idx == len(DILATED_BRANCHES) - 1
        state = _attn_a_branch(proj, dil, state, final)
    return state


KEY_CHUNK = 256


def _dsa_kernel(qb_ref, qi_ref, wi_ref, k_ref, kk_ref, v_ref, o_ref, key_scr, *, topk):
    i = pl.program_id(1)
    nchunks = (i + 2) // 2
    halves = KEY_CHUNK // LANES

    wi = wi_ref[...] * (IDX_HEADS ** -0.5 * IDX_DIM ** -0.5)
    lane = lax.broadcasted_iota(jnp.int32, (Q_BLK, LANES), 1)
    qi = qi_ref[...]
    heads = []
    for h in range(IDX_HEADS):
        blk = qi[:, (h // 2) * LANES:(h // 2 + 1) * LANES]
        keep = (lane < IDX_DIM) if h % 2 == 0 else (lane >= IDX_DIM)
        heads.append(jnp.where(keep, blk, jnp.zeros_like(blk)))
    qi_all = jnp.concatenate(heads, axis=0)
    qpos = i * Q_BLK + lax.broadcasted_iota(jnp.int32, (Q_BLK, KEY_CHUNK), 0)
    kiota = lax.broadcasted_iota(jnp.int32, (Q_BLK, KEY_CHUNK), 1)

    def score_chunk(c, carry):
        k0 = pl.multiple_of(c * KEY_CHUNK, KEY_CHUNK)
        z = lax.dot_general(qi_all, kk_ref[pl.ds(k0, KEY_CHUNK), :], (((1,), (1,)), ((), ())),
                            preferred_element_type=F32)
        sc = jnp.zeros((Q_BLK, KEY_CHUNK), F32)
        for h in range(IDX_HEADS):
            sc = sc + wi[:, h:h + 1] * jnp.maximum(z[h * Q_BLK:(h + 1) * Q_BLK], 0.0)
        bits = pltpu.bitcast(sc, jnp.int32)
        key = bits ^ ((bits >> 31) & jnp.int32(0x7FFFFFFF))
        key_scr[c] = jnp.where(k0 + kiota <= qpos, key, jnp.int32(INT_MIN))
        return carry

    lax.fori_loop(0, nchunks, score_chunk, 0)

    def bit_body(t, ans):
        trial = ans | jnp.left_shift(jnp.int32(1), 31 - t)
        thr = trial ^ jnp.int32(INT_MIN)

        def count_chunk(c, acc):
            key = key_scr[c]
            for q in range(halves):
                acc = acc + jnp.where(key[:, q * LANES:(q + 1) * LANES] >= thr, 1.0, 0.0)
            return acc

        acc = lax.fori_loop(0, nchunks, count_chunk, jnp.zeros((Q_BLK, LANES), F32))
        cnt = jnp.sum(acc, axis=1, keepdims=True)
        return jnp.where(cnt >= float(topk), trial, ans)

    ans = lax.fori_loop(0, 32, bit_body, jnp.zeros((Q_BLK, LANES), jnp.int32))
    tau = jnp.maximum(ans ^ jnp.int32(INT_MIN), jnp.int32(INT_MIN + 1))
    tau2 = jnp.concatenate([tau] * halves, axis=1)

    qb = qb_ref[...]
    q_all = jnp.concatenate([qb[:, h * HEAD_DIM:(h + 1) * HEAD_DIM] for h in range(B_HEADS)], axis=0)
    rows = B_HEADS * Q_BLK

    def att_chunk(c, carry):
        m, l, acc = carry
        k0 = pl.multiple_of(c * KEY_CHUNK, KEY_CHUNK)
        s = lax.dot_general(q_all, k_ref[pl.ds(k0, KEY_CHUNK), :], (((1,), (1,)), ((), ())),
                            preferred_element_type=F32)
        bias = jnp.where(key_scr[c] >= tau2, 0.0, NEG)
        s = (s.reshape(B_HEADS, Q_BLK, KEY_CHUNK) + bias[None]).reshape(rows, KEY_CHUNK)
        m_new = jnp.maximum(m, jnp.max(s, axis=-1, keepdims=True))
        alpha = jnp.exp(m - m_new)
        p = jnp.exp(s - m_new)
        l = alpha * l + jnp.sum(p, axis=-1, keepdims=True)
        acc = alpha * acc + jnp.dot(p.astype(BF16), v_ref[pl.ds(k0, KEY_CHUNK), :],
                                    preferred_element_type=F32)
        return m_new, l, acc

    init = (jnp.full((rows, 1), NEG, F32), jnp.zeros((rows, 1), F32), jnp.zeros((rows, HEAD_DIM), F32))
    _, l, acc = lax.fori_loop(0, nchunks, att_chunk, init)
    out = acc / l
    for h in range(B_HEADS):
        o_ref[:, h * HEAD_DIM:(h + 1) * HEAD_DIM] = out[h * Q_BLK:(h + 1) * Q_BLK].astype(o_ref.dtype)


def _dsa(proj, aux):
    nbatch, seq, _ = proj.shape
    assert seq % KEY_CHUNK == 0
    topk = min(DSA_TOPK_MAX, seq // 4)
    qw, iw = B_HEADS * HEAD_DIM, IDX_HEADS * IDX_DIM
    full = lambda off: pl.BlockSpec((None, seq, LANES), lambda b, i: (b, 0, off // LANES))
    return pl.pallas_call(
        functools.partial(_dsa_kernel, topk=topk),
        grid=(nbatch, seq // Q_BLK),
        in_specs=[pl.BlockSpec((None, Q_BLK, qw), lambda b, i: (b, i, QB_OFF // qw)),
                  pl.BlockSpec((None, Q_BLK, iw), lambda b, i: (b, i, QI_OFF // iw)),
                  pl.BlockSpec((None, Q_BLK, LANES), lambda b, i: (b, i, 0)),
                  full(KB_OFF), full(KIKI_OFF), full(VB_OFF)],
        out_specs=pl.BlockSpec((None, Q_BLK, qw), lambda b, i: (b, i, 0)),
        out_shape=jax.ShapeDtypeStruct((nbatch, seq, qw), BF16),
        scratch_shapes=[pltpu.VMEM((seq // KEY_CHUNK, Q_BLK, KEY_CHUNK), jnp.int32)],
        compiler_params=_params(("parallel", "arbitrary")),
        name="dsa",
    )(proj, proj, aux, proj, proj, proj)


def _out_proj_kernel(oa_ref, ob_ref, w_ref, x_ref, mod_ref, o_ref):
    ka = oa_ref.shape[-1]
    y = jnp.dot(oa_ref[...], w_ref[:ka, :], preferred_element_type=F32)
    y = y + jnp.dot(ob_ref[...], w_ref[ka:, :], preferred_element_type=F32)
    o_ref[...] = x_ref[...] + mod_ref[2:3, :] * y


def _out_proj(o_a, o_b, w_out, x, mod3):
    nbatch, seq, d = x.shape
    ka, kb = o_a.shape[-1], o_b.shape[-1]
    tm, tn = _tile(seq, 512), _tile(d, 512)
    return pl.pallas_call(
        _out_proj_kernel,
        grid=(nbatch, seq // tm, d // tn),
        in_specs=[pl.BlockSpec((None, tm, ka), lambda b, i, j: (b, i, 0)),
                  pl.BlockSpec((None, tm, kb), lambda b, i, j: (b, i, 0)),
                  pl.BlockSpec((ka + kb, tn), lambda b, i, j: (0, j)),
                  pl.BlockSpec((None, tm, tn), lambda b, i, j: (b, i, j)),
                  pl.BlockSpec((None, ADA_CHUNKS, tn), lambda b, i, j: (b, 0, j))],
        out_specs=pl.BlockSpec((None, tm, tn), lambda b, i, j: (b, i, j)),
        out_shape=jax.ShapeDtypeStruct((nbatch, seq, d), F32),
        compiler_params=_params(("parallel", "parallel", "parallel")),
        name="out_proj",
    )(o_a, o_b, w_out, x, mod3)


def _peer_q_kernel(x_ref, mod_ref, g_ref, w_ref, keys_ref, h_ref, st_ref, h_scr):
    j = pl.program_id(2)

    @pl.when(j == 0)
    def _():
        h = _norm_mod(x_ref[...], g_ref[...], mod_ref[4:5, :], mod_ref[3:4, :]).astype(BF16)
        h_scr[...] = h
        h_ref[...] = h

    q = jnp.dot(h_scr[...], w_ref[...], preferred_element_type=F32)
    for blk in range(q.shape[1] // PEER_HALF):
        qh = q[:, blk * PEER_HALF:(blk + 1) * PEER_HALF]
        st_ref[blk] = lax.dot_general(keys_ref[blk], qh, (((1,), (1,)), ((), ())),
                                      preferred_element_type=F32, precision=lax.Precision.HIGHEST)


def _peer_q(x1, mod3, g, w_q, keys):
    nbatch, seq, d = x1.shape
    nq = w_q.shape[1]
    tm, tn = _tile(seq, 512), 4 * PEER_HALF
    nhp = nq // PEER_HALF
    nt = seq // tm
    return pl.pallas_call(
        _peer_q_kernel,
        grid=(nbatch, nt, nq // tn),
        in_specs=[pl.BlockSpec((None, tm, d), lambda b, i, j: (b, i, 0)),
                  pl.BlockSpec((None, ADA_CHUNKS, d), lambda b, i, j: (b, 0, 0)),
                  pl.BlockSpec((1, d), lambda b, i, j: (0, 0)),
                  pl.BlockSpec((d, tn), lambda b, i, j: (0, j)),
                  pl.BlockSpec((tn // PEER_HALF, PEER_NKEYS, PEER_HALF), lambda b, i, j: (j, 0, 0))],
        out_specs=[pl.BlockSpec((None, tm, d), lambda b, i, j: (b, i, 0)),
                   pl.BlockSpec((tn // PEER_HALF, PEER_NKEYS, tm), lambda b, i, j: (j, 0, b * nt + i))],
        out_shape=[jax.ShapeDtypeStruct((nbatch, seq, d), BF16),
                   jax.ShapeDtypeStruct((nhp, PEER_NKEYS, nbatch * seq), F32)],
        scratch_shapes=[pltpu.VMEM((tm, d), BF16)],
        compiler_params=_params(("parallel", "parallel", "arbitrary")),
        name="peer_q",
    )(x1, mod3, g.reshape(1, d), w_q, keys)


def _top_rows(x, count, one_at_a_time):
    rows = []
    ridx = lax.broadcasted_iota(jnp.int32, x.shape, 0)
    for _ in range(count):
        m = jnp.max(x, axis=0, keepdims=True)
        rows.append(m)
        hit = x == m
        if one_at_a_time:
            first = jnp.min(jnp.where(hit, ridx, x.shape[0]), axis=0, keepdims=True)
            hit = ridx == first
        x = jnp.where(hit, -jnp.inf, x)
    removed = jnp.sum(jnp.where(x == -jnp.inf, 1.0, 0.0), axis=0, keepdims=True)
    return rows, removed


def _peer_group_stats(x0, x1, one_at_a_time):
    half = PEER_TOPK // 2
    v0, r0 = _top_rows(x0, PEER_TOPK, one_at_a_time)
    v1, r1 = _top_rows(x1, PEER_TOPK, one_at_a_time)
    v0_all = jnp.concatenate(v0, axis=0)
    v1_all = jnp.concatenate(v1, axis=0)
    cand = [v0[0] + v1_all]
    cand += [v0[a] + v1_all[:half] for a in range(1, half)]
    cand += [v0_all[half:] + v1[0]]
    top, _ = _top_rows(jnp.concatenate(cand, axis=0), PEER_TOPK, True)
    z = jnp.zeros_like(top[0])
    for t in top:
        z = z + jnp.exp(t - top[0])
    stats = jnp.concatenate([top[-1], v0[0], v1[0], 1.0 / z] + [jnp.zeros_like(z)] * 4, axis=0)
    repeated = jnp.max(jnp.maximum(r0, r1)) > float(PEER_TOPK)
    return stats, repeated


def _peer_topk_kernel(st_ref, stats_ref):
    ngroups = st_ref.shape[-1] // LANES

    def per_head(h, carry):
        for tg in range(ngroups):
            cols = slice(tg * LANES, (tg + 1) * LANES)
            stats, repeated = _peer_group_stats(st_ref[2 * h, :, cols], st_ref[2 * h + 1, :, cols], False)
            stats_ref[h, :, cols] = stats

            @pl.when(repeated)
            def _():
                stats_ref[h, :, cols] = _peer_group_stats(st_ref[2 * h, :, cols],
                                                          st_ref[2 * h + 1, :, cols], True)[0]
        return carry

    lax.fori_loop(0, PEER_HEADS, per_head, 0)


def _peer_topk(st):
    nhp, nkeys, ntok = st.shape
    tmk = _tile(ntok, 256)
    return pl.pallas_call(
        _peer_topk_kernel,
        grid=(ntok // tmk,),
        in_specs=[pl.BlockSpec((nhp, nkeys, tmk), lambda i: (0, 0, i))],
        out_specs=pl.BlockSpec((PEER_HEADS, 8, tmk), lambda i: (0, 0, i)),
        out_shape=jax.ShapeDtypeStruct((PEER_HEADS, 8, ntok), F32),
        compiler_params=_params(("parallel",)),
        name="peer_topk",
    )(st)


def _peer_dense_kernel(x_ref, u_ref, v_ref, st_ref, stats_ref, o_ref, e1_scr, act_scr):
    e = pl.program_id(1)
    te = u_ref.shape[0]
    ni = te // PEER_NKEYS

    @pl.when(e == 0)
    def _():
        o_ref[...] = jnp.zeros_like(o_ref)
        for h in range(PEER_HEADS):
            e1_scr[h] = jnp.exp(st_ref[2 * h + 1] - stats_ref[h, 2:3, :]) * stats_ref[h, 3:4, :]

    a_t = lax.dot_general(u_ref[...], x_ref[...], (((1,), (1,)), ((), ())),
                          preferred_element_type=F32)
    for il in range(ni):
        i = e * ni + il
        g = jnp.zeros((PEER_NKEYS, a_t.shape[1]), F32)
        for h in range(PEER_HEADS):
            s0 = st_ref[2 * h, pl.ds(i, 1), :]
            e0 = jnp.exp(s0 - stats_ref[h, 1:2, :])
            sel = (s0 + st_ref[2 * h + 1]) >= stats_ref[h, 0:1, :]
            g = g + jnp.where(sel, e0 * e1_scr[h], 0.0)
        a = a_t[il * PEER_NKEYS:(il + 1) * PEER_NKEYS]
        act = 0.5 * a * (1.0 + lax.erf(a * (2.0 ** -0.5))) * g
        act_scr[il * PEER_NKEYS:(il + 1) * PEER_NKEYS, :] = act.astype(BF16)
    o_ref[...] += lax.dot_general(act_scr[...], v_ref[...], (((0,), (0,)), ((), ())),
                                  preferred_element_type=F32)


def _peer_dense(h2, u_tab, v_tab, st, stats):
    ntok, d = h2.shape
    nexp = u_tab.shape[0]
    tm, te = _tile(ntok, 512), 4 * PEER_NKEYS
    nhp = st.shape[0]
    return pl.pallas_call(
        _peer_dense_kernel,
        grid=(ntok // tm, nexp // te),
        in_specs=[pl.BlockSpec((tm, d), lambda i, e: (i, 0)),
                  pl.BlockSpec((te, d), lambda i, e: (e, 0)),
                  pl.BlockSpec((te, d), lambda i, e: (e, 0)),
                  pl.BlockSpec((nhp, PEER_NKEYS, tm), lambda i, e: (0, 0, i)),
                  pl.BlockSpec((PEER_HEADS, 8, tm), lambda i, e: (0, 0, i))],
        out_specs=pl.BlockSpec((tm, d), lambda i, e: (i, 0)),
        out_shape=jax.ShapeDtypeStruct((ntok, d), F32),
        scratch_shapes=[pltpu.VMEM((PEER_HEADS, PEER_NKEYS, tm), F32), pltpu.VMEM((te, tm), BF16)],
        compiler_params=_params(("parallel", "arbitrary")),
        name="peer_dense",
    )(h2, u_tab, v_tab, st, stats)


def _final_kernel(x_ref, p_ref, mod_ref, g_ref, o_ref, *, normalize):
    y = x_ref[...] + mod_ref[5:6, :] * p_ref[...]
    if normalize:
        y = (y * lax.rsqrt(jnp.mean(y * y, axis=-1, keepdims=True) + NORM_EPS)) * g_ref[...]
    o_ref[...] = y


def _final(x1, peer_out, mod3, g, normalize):
    nbatch, seq, d = x1.shape
    tm = _tile(seq, 256)
    spec = pl.BlockSpec((None, tm, d), lambda b, i: (b, i, 0))
    return pl.pallas_call(
        functools.partial(_final_kernel, normalize=normalize),
        grid=(nbatch, seq // tm),
        in_specs=[spec, spec,
                  pl.BlockSpec((None, ADA_CHUNKS, d), lambda b, i: (b, 0, 0)),
                  pl.BlockSpec((1, d), lambda b, i: (0, 0))],
        out_specs=spec,
        out_shape=jax.ShapeDtypeStruct((nbatch, seq, d), F32),
        compiler_params=_params(("parallel", "parallel")),
        name="final",
    )(x1, peer_out, mod3, g.reshape(1, d))


def kernel(x, c, positions, ln1_g, ln2_g, w_ada, b_ada, w_in, w_out, peer_wq, peer_sub_keys, peer_u,
           peer_v, lnf_g):
    nbatch, seq, d = x.shape
    depth = w_ada.shape[0]
    tables = _rope_tables(positions)
    for layer in range(depth):
        mod3 = _ada(c, w_ada[layer], b_ada[layer]).reshape(nbatch, ADA_CHUNKS, d)
        proj, aux = _in_proj(x, mod3, ln1_g[layer], _permute_w_in(w_in[layer]), tables)
        o_a = _attn_a(proj)
        o_b = _dsa(proj, aux)
        x = _out_proj(o_a, o_b, w_out[layer].astype(BF16), x, mod3)
        keys = peer_sub_keys[layer].reshape(2 * PEER_HEADS, PEER_NKEYS, PEER_HALF)
        h2, st = _peer_q(x, mod3, ln2_g[layer], peer_wq[layer].astype(BF16), keys)
        stats = _peer_topk(st)
        peer_out = _peer_dense(h2.reshape(nbatch * seq, d), peer_u[layer].astype(BF16),
                               peer_v[layer].astype(BF16), st, stats)
        x = _final(x, peer_out.reshape(nbatch, seq, d), mod3, lnf_g, normalize=layer + 1 == depth)
    return x
```

```python
import functools
import math

import jax
import jax.numpy as jnp
from jax import lax
from jax.experimental import pallas as pl
from jax.experimental.pallas import tpu as pltpu

F32 = jnp.float32
BF16 = jnp.bfloat16

HEAD_DIM = 128
A_HEADS = 16
A_KV_HEADS = 4
A_GROUP = A_HEADS // A_KV_HEADS
DILATED_BRANCHES = ((128, 1), (512, 4), (2048, 16))
WIN_BLK = 128
B_HEADS = 16
IDX_HEADS = 16
IDX_DIM = 64
DSA_TOPK_MAX = 256
Q_BLK = 128
PEER_HEADS = 8
PEER_NKEYS = 128
PEER_HALF = 128
PEER_TOPK = 16
ROPE_THETA = 10000.0
NORM_EPS = 1e-6
NEG = -1e30
ADA_CHUNKS = 6
INT_MIN = -(2 ** 31)

LANES = 128
VMEM_LIMIT = 56 * 1024 * 1024

PROJ_TILE = 512
A_TILES = (A_HEADS + 2 * A_KV_HEADS) * HEAD_DIM // PROJ_TILE
KA_TILE = A_HEADS * HEAD_DIM // PROJ_TILE
VA_TILE = KA_TILE + 1
PA_HEADS = A_HEADS + 2 * A_KV_HEADS
QB_OFF = 0
QI_OFF = QB_OFF + B_HEADS * HEAD_DIM
MIX_OFF = QI_OFF + IDX_HEADS * IDX_DIM
KB_OFF = MIX_OFF
KIKI_OFF = MIX_OFF + LANES
VB_OFF = MIX_OFF + 2 * LANES
WI_OFF = MIX_OFF + 3 * LANES
P_COLS = MIX_OFF + PROJ_TILE
PROJ_TILES = A_TILES + P_COLS // PROJ_TILE


def _params(semantics):
    return pltpu.CompilerParams(dimension_semantics=semantics, vmem_limit_bytes=VMEM_LIMIT)


def _tile(n, pref):
    return pref if n % pref == 0 else n


def _norm_mod(x, g, scale, shift):
    xf = x.astype(F32)
    y = xf * lax.rsqrt(jnp.mean(xf * xf, axis=-1, keepdims=True) + NORM_EPS)
    return (y * g) * (1.0 + scale) + shift


def _ada_kernel(ct_ref, w_ref, b_ref, o_ref, *, nbatch):
    @pl.when(pl.program_id(1) == 0)
    def _():
        o_ref[...] = jnp.broadcast_to(b_ref[...], o_ref.shape)

    ct = ct_ref[...]
    s = ct * jax.nn.sigmoid(ct)
    w = w_ref[...]
    rows = [jnp.sum(w * s[:, b:b + 1], axis=0, keepdims=True) for b in range(nbatch)]
    o_ref[...] += jnp.concatenate(rows, axis=0)


def _ada(c, w_ada, b_ada):
    nbatch, d = c.shape
    n = w_ada.shape[1]
    tk, tn = _tile(d, 512), _tile(n, 2048)
    return pl.pallas_call(
        functools.partial(_ada_kernel, nbatch=nbatch),
        grid=(n // tn, d // tk),
        in_specs=[pl.BlockSpec((tk, nbatch), lambda j, k: (k, 0)),
                  pl.BlockSpec((tk, tn), lambda j, k: (k, j)),
                  pl.BlockSpec((1, tn), lambda j, k: (0, j))],
        out_specs=pl.BlockSpec((nbatch, tn), lambda j, k: (0, j)),
        out_shape=jax.ShapeDtypeStruct((nbatch, n), F32),
        compiler_params=_params(("parallel", "arbitrary")),
        name="ada",
    )(c.T, w_ada, b_ada.reshape(1, n))


def _rope_kernel(pos_ref, ch_ref, sh_ref, ci_ref, sa_ref, sb_ref):
    pos = pos_ref[...].astype(F32)
    lane = lax.broadcasted_iota(jnp.int32, (1, LANES), 1)
    expo_h = -((2 * (lane & 63)).astype(F32)) / HEAD_DIM
    expo_i = -((2 * (lane & 31)).astype(F32)) / IDX_DIM
    inv = jnp.power(ROPE_THETA, jnp.where(lane < 64, expo_h, expo_i))
    ang = pos * inv
    c = jnp.cos(ang)
    s = jnp.sin(ang)
    ch_ref[...] = jnp.where(lane < 64, c, pltpu.roll(c, 64, 1))
    sh_ref[...] = jnp.where(lane < 64, -s, pltpu.roll(s, 64, 1))
    grp = lane >> 5
    c64, c96, c32 = pltpu.roll(c, 64, 1), pltpu.roll(c, 96, 1), pltpu.roll(c, 32, 1)
    s64, s96, s32 = pltpu.roll(s, 64, 1), pltpu.roll(s, 96, 1), pltpu.roll(s, 32, 1)
    ci_ref[...] = jnp.where(grp == 0, c64, jnp.where(grp == 1, c96, jnp.where(grp == 2, c, c32)))
    sa_ref[...] = jnp.where(grp == 0, -s64, jnp.where(grp == 2, -s, 0.0))
    sb_ref[...] = jnp.where(grp == 1, s96, jnp.where(grp == 3, s32, 0.0))


def _rope_tables(positions):
    nbatch, seq = positions.shape
    ts = _tile(seq, 1024)
    spec = pl.BlockSpec((None, ts, LANES), lambda b, i: (b, i, 0))
    shape = jax.ShapeDtypeStruct((nbatch, seq, LANES), F32)
    return pl.pallas_call(
        _rope_kernel,
        grid=(nbatch, seq // ts),
        in_specs=[pl.BlockSpec((None, ts, 1), lambda b, i: (b, i, 0))],
        out_specs=[spec] * 5,
        out_shape=[shape] * 5,
        compiler_params=_params(("parallel", "parallel")),
        name="rope",
    )(positions.reshape(nbatch, seq, 1))


def _rope_head(a, ch, sh):
    return a * ch + pltpu.roll(a, 64, 1) * sh


def _rope_idx(a, ci, sa, sb):
    return a * ci + pltpu.roll(a, 96, 1) * sa + pltpu.roll(a, 32, 1) * sb


def _in_proj_kernel(x_ref, mod_ref, g_ref, w_ref, ch_ref, sh_ref, ci_ref, sa_ref, sb_ref,
                    pa_ref, p_ref, aux_ref, h_scr, acc_scr):
    j = pl.program_id(2)

    @pl.when(j == 0)
    def _():
        h_scr[...] = _norm_mod(x_ref[...], g_ref[...], mod_ref[1:2, :], mod_ref[0:1, :]).astype(BF16)

    acc_scr[...] = jnp.dot(h_scr[...], w_ref[...], preferred_element_type=F32)
    nblk = PROJ_TILE // LANES
    qscale = HEAD_DIM ** -0.5
    qb_tile = A_TILES + QB_OFF // PROJ_TILE
    qi_tile = A_TILES + QI_OFF // PROJ_TILE
    mix_tile = A_TILES + MIX_OFF // PROJ_TILE

    def blk(q):
        return acc_scr[:, q * LANES:(q + 1) * LANES]

    def put(q, val):
        p_ref[:, q * LANES:(q + 1) * LANES] = val.astype(BF16)

    @pl.when(j < KA_TILE)
    def _():
        for q in range(nblk):
            pa_ref[q] = _rope_head(blk(q), ch_ref[...], sh_ref[...]) * qscale

    @pl.when(j == KA_TILE)
    def _():
        for q in range(nblk):
            pa_ref[q] = _rope_head(blk(q), ch_ref[...], sh_ref[...])

    @pl.when(j == VA_TILE)
    def _():
        for q in range(nblk):
            pa_ref[q] = blk(q)

    @pl.when((j >= qb_tile) & (j < qi_tile))
    def _():
        for q in range(nblk):
            put(q, _rope_head(blk(q), ch_ref[...], sh_ref[...]) * qscale)

    @pl.when((j >= qi_tile) & (j < mix_tile))
    def _():
        for q in range(nblk):
            put(q, _rope_idx(blk(q), ci_ref[...], sa_ref[...], sb_ref[...]))

    @pl.when(j == mix_tile)
    def _():
        put(0, _rope_head(blk(0), ch_ref[...], sh_ref[...]))
        put(1, _rope_idx(blk(1), ci_ref[...], sa_ref[...], sb_ref[...]))
        put(2, blk(2))
        put(3, blk(3))
        aux_ref[...] = blk(3)


def _permute_w_in(w_in):
    d = w_in.shape[0]
    w = w_in.astype(BF16)
    sizes = (A_HEADS * HEAD_DIM, A_KV_HEADS * HEAD_DIM, A_KV_HEADS * HEAD_DIM, B_HEADS * HEAD_DIM,
             HEAD_DIM, HEAD_DIM, IDX_HEADS * IDX_DIM, IDX_DIM, IDX_HEADS)
    offs = [0]
    for sz in sizes:
        offs.append(offs[-1] + sz)
    assert w.shape[1] == offs[-1]
    qa, ka, va, qb, kb, vb, qi, ki, wi = [w[:, offs[i]:offs[i + 1]] for i in range(len(sizes))]
    pad = jnp.zeros((d, LANES - IDX_HEADS), BF16)
    out = jnp.concatenate([qa, ka, va, qb, qi, kb, ki, ki, vb, wi, pad], axis=1)
    assert out.shape[1] == PROJ_TILES * PROJ_TILE
    return out


def _in_proj(x, mod3, g, w_perm, tables):
    nbatch, seq, d = x.shape
    tm = _tile(seq, 512)
    nblk = PROJ_TILE // LANES
    tab_spec = pl.BlockSpec((None, tm, LANES), lambda b, i, j: (b, i, 0))
    return pl.pallas_call(
        _in_proj_kernel,
        grid=(nbatch, seq // tm, PROJ_TILES),
        in_specs=[pl.BlockSpec((None, tm, d), lambda b, i, j: (b, i, 0)),
                  pl.BlockSpec((None, ADA_CHUNKS, d), lambda b, i, j: (b, 0, 0)),
                  pl.BlockSpec((1, d), lambda b, i, j: (0, 0)),
                  pl.BlockSpec((d, PROJ_TILE), lambda b, i, j: (0, j))] + [tab_spec] * 5,
        out_specs=[pl.BlockSpec((None, nblk, tm, LANES), lambda b, i, j: (b, jnp.minimum(j, A_TILES - 1), i, 0)),
                   pl.BlockSpec((None, tm, PROJ_TILE), lambda b, i, j: (b, i, jnp.maximum(j - A_TILES, 0))),
                   pl.BlockSpec((None, tm, LANES), lambda b, i, j: (b, i, 0))],
        out_shape=[jax.ShapeDtypeStruct((nbatch, PA_HEADS, seq, LANES), F32),
                   jax.ShapeDtypeStruct((nbatch, seq, P_COLS), BF16),
                   jax.ShapeDtypeStruct((nbatch, seq, LANES), F32)],
        scratch_shapes=[pltpu.VMEM((tm, d), BF16), pltpu.VMEM((tm, PROJ_TILE), F32)],
        compiler_params=_params(("parallel", "parallel", "arbitrary")),
        name="in_proj",
    )(x, mod3, g.reshape(1, d), w_perm, *tables)


def _attn_a_kernel(q_ref, k_ref, v_ref, o_ref, acc_scr, lse_scr, bias_scr, *, dilations):
    half = pl.program_id(2)
    qrows = q_ref.shape[1]
    nblk = qrows // WIN_BLK

    rows = A_GROUP * WIN_BLK
    row = lax.broadcasted_iota(jnp.int32, (rows, 2 * WIN_BLK), 0) & (WIN_BLK - 1)
    col = lax.broadcasted_iota(jnp.int32, (rows, 2 * WIN_BLK), 1)
    band = (col >= row) & (col <= row + WIN_BLK)
    bias_scr[0] = jnp.where(band & (col >= WIN_BLK), 0.0, NEG)
    bias_scr[1] = jnp.where(band, 0.0, NEG)

    def rows_at(start, dil):
        return pl.ds(start, WIN_BLK) if dil == 1 else pl.ds(start, WIN_BLK, stride=dil)

    for idx, dil in enumerate(dilations):
        per_res = nblk // dil
        first, last = idx == 0, idx == len(dilations) - 1

        def body(blk, carry, dil=dil, per_res=per_res, first=first, last=last):
            r = blk // per_res
            n_loc = blk - r * per_res
            n = half * per_res + n_loc
            q0 = r + dil * WIN_BLK * n_loc
            k0 = r + dil * WIN_BLK * n
            kp = r + dil * WIN_BLK * jnp.maximum(n - 1, 0)
            q4 = jnp.concatenate([q_ref[g, rows_at(q0, dil), :] for g in range(A_GROUP)], axis=0).astype(BF16)
            kc = jnp.concatenate([k_ref[rows_at(kp, dil), :], k_ref[rows_at(k0, dil), :]], axis=0).astype(BF16)
            vc = jnp.concatenate([v_ref[rows_at(kp, dil), :], v_ref[rows_at(k0, dil), :]], axis=0).astype(BF16)
            s = lax.dot_general(q4, kc, (((1,), (1,)), ((), ())), preferred_element_type=F32)
            s = s + bias_scr[jnp.minimum(n, 1)]
            m = jnp.max(s, axis=-1, keepdims=True)
            p = jnp.exp(s - m)
            l = jnp.sum(p, axis=-1, keepdims=True)
            o = jnp.dot(p.astype(BF16), vc, preferred_element_type=F32) / l
            lse = m + jnp.log(l)
            for g in range(A_GROUP):
                og = o[g * WIN_BLK:(g + 1) * WIN_BLK]
                lg = jnp.broadcast_to(lse[g * WIN_BLK:(g + 1) * WIN_BLK], (WIN_BLK, HEAD_DIM))
                if not first:
                    opg = acc_scr[g, rows_at(q0, dil), :]
                    lpg = lse_scr[g, rows_at(q0, dil), :]
                    mx = jnp.maximum(lpg, lg)
                    tot = mx + jnp.log(jnp.exp(lpg - mx) + jnp.exp(lg - mx))
                    og = opg * jnp.exp(lpg - tot) + og * jnp.exp(lg - tot)
                    lg = tot
                if last:
                    o_ref[pl.ds(pl.multiple_of(q0, WIN_BLK), WIN_BLK), g * HEAD_DIM:(g + 1) * HEAD_DIM] = (
                        og.astype(o_ref.dtype))
                else:
                    acc_scr[g, rows_at(q0, dil), :] = og
                    lse_scr[g, rows_at(q0, dil), :] = lg
            return carry

        lax.fori_loop(0, nblk, body, 0)


def _attn_a(pa):
    nbatch, _, seq, _ = pa.shape
    dilations = tuple(sorted((dil for _, dil in DILATED_BRANCHES), reverse=True))
    assert dilations[-1] == 1 and all(w // dil == WIN_BLK for w, dil in DILATED_BRANCHES)
    nhalf = 2 if seq % (2 * WIN_BLK * dilations[0]) == 0 else 1
    qrows = seq // nhalf
    assert qrows % (WIN_BLK * dilations[0]) == 0
    return pl.pallas_call(
        functools.partial(_attn_a_kernel, dilations=dilations),
        grid=(nbatch, A_KV_HEADS, nhalf),
        in_specs=[pl.BlockSpec((None, A_GROUP, qrows, HEAD_DIM), lambda b, h, t: (b, h, t, 0)),
                  pl.BlockSpec((None, None, seq, HEAD_DIM), lambda b, h, t: (b, A_HEADS + h, 0, 0)),
                  pl.BlockSpec((None, None, seq, HEAD_DIM),
                               lambda b, h, t: (b, A_HEADS + A_KV_HEADS + h, 0, 0))],
        out_specs=pl.BlockSpec((None, qrows, A_GROUP * HEAD_DIM), lambda b, h, t: (b, t, h)),
        out_shape=jax.ShapeDtypeStruct((nbatch, seq, A_HEADS * HEAD_DIM), BF16),
        scratch_shapes=[pltpu.VMEM((A_GROUP, qrows, HEAD_DIM), F32),
                        pltpu.VMEM((A_GROUP, qrows, HEAD_DIM), F32),
                        pltpu.VMEM((2, A_GROUP * WIN_BLK, 2 * WIN_BLK), F32)],
        compiler_params=_params(("parallel", "parallel", "arbitrary")),
        name="attn_a",
    )(pa, pa, pa)


KEY_CHUNK = 256


def _dsa_kernel(qb_ref, qi_ref, wi_ref, k_ref, kk_ref, v_ref, o_ref,
                key_scr, wib_scr, q_scr, p_scr, m_scr, l_scr, acc_scr, *, topk):
    i = pl.program_id(1)
    nchunks = (i + 2) // 2
    halves = KEY_CHUNK // LANES

    wi = wi_ref[...] * (IDX_HEADS ** -0.5 * IDX_DIM ** -0.5)
    lane = lax.broadcasted_iota(jnp.int32, (Q_BLK, LANES), 1)
    qi = qi_ref[...]
    for h in range(IDX_HEADS):
        blk = qi[:, (h // 2) * LANES:(h // 2 + 1) * LANES]
        keep = (lane < IDX_DIM) if h % 2 == 0 else (lane >= IDX_DIM)
        q_scr[h * Q_BLK:(h + 1) * Q_BLK, :] = jnp.where(keep, blk, jnp.zeros_like(blk))
        wib_scr[h] = jnp.broadcast_to(wi[:, h:h + 1], (Q_BLK, LANES))
    qpos = i * Q_BLK + lax.broadcasted_iota(jnp.int32, (Q_BLK, KEY_CHUNK), 0)
    kiota = lax.broadcasted_iota(jnp.int32, (Q_BLK, KEY_CHUNK), 1)

    def score_chunk(c, carry):
        k0 = pl.multiple_of(c * KEY_CHUNK, KEY_CHUNK)
        z = lax.dot_general(q_scr[...], kk_ref[pl.ds(k0, KEY_CHUNK), :], (((1,), (1,)), ((), ())),
                            preferred_element_type=F32)
        sc = jnp.zeros((Q_BLK, KEY_CHUNK), F32)
        for h in range(IDX_HEADS):
            w = wib_scr[h]
            sc = sc + jnp.concatenate([w] * halves, axis=1) * jnp.maximum(z[h * Q_BLK:(h + 1) * Q_BLK], 0.0)
        bits = pltpu.bitcast(sc, jnp.int32)
        key = bits ^ ((bits >> 31) & jnp.int32(0x7FFFFFFF))
        key_scr[c] = jnp.where(k0 + kiota <= qpos, key, jnp.int32(INT_MIN))
        return carry

    lax.fori_loop(0, nchunks, score_chunk, 0)

    def bit_body(t, ans):
        trial = ans | jnp.left_shift(jnp.int32(1), 31 - t)
        thr = trial ^ jnp.int32(INT_MIN)

        def count_chunk(c, acc):
            key = key_scr[c]
            for q in range(halves):
                acc = acc + jnp.where(key[:, q * LANES:(q + 1) * LANES] >= thr, 1.0, 0.0)
            return acc

        acc = lax.fori_loop(0, nchunks, count_chunk, jnp.zeros((Q_BLK, LANES), F32))
        cnt = jnp.sum(acc, axis=1, keepdims=True)
        return jnp.where(cnt >= float(topk), trial, ans)

    ans = lax.fori_loop(0, 32, bit_body, jnp.zeros((Q_BLK, LANES), jnp.int32))
    tau = jnp.maximum(ans ^ jnp.int32(INT_MIN), jnp.int32(INT_MIN + 1))
    tau2 = jnp.concatenate([tau] * halves, axis=1)

    for h in range(B_HEADS):
        q_scr[h * Q_BLK:(h + 1) * Q_BLK, :] = qb_ref[:, h * HEAD_DIM:(h + 1) * HEAD_DIM]
    m_scr[...] = jnp.full(m_scr.shape, NEG, F32)
    l_scr[...] = jnp.zeros(l_scr.shape, F32)
    acc_scr[...] = jnp.zeros(acc_scr.shape, F32)

    def att_chunk(c, carry):
        k0 = pl.multiple_of(c * KEY_CHUNK, KEY_CHUNK)
        s = lax.dot_general(q_scr[...], k_ref[pl.ds(k0, KEY_CHUNK), :], (((1,), (1,)), ((), ())),
                            preferred_element_type=F32)
        bias = jnp.where(key_scr[c] >= tau2, 0.0, NEG)
        for h in range(B_HEADS):
            hr = slice(h * Q_BLK, (h + 1) * Q_BLK)
            sh = s[hr] + bias
            m_old = m_scr[hr, :]
            m_new = jnp.maximum(m_old, jnp.max(sh, axis=-1, keepdims=True))
            alpha = jnp.exp(m_old - m_new)
            p = jnp.exp(sh - jnp.concatenate([m_new] * halves, axis=1))
            psum = p[:, :LANES]
            for q in range(1, halves):
                psum = psum + p[:, q * LANES:(q + 1) * LANES]
            l_scr[hr, :] = alpha * l_scr[hr, :] + psum
            m_scr[hr, :] = m_new
            p_scr[hr, :] = p.astype(BF16)
            acc_scr[hr, :] = alpha * acc_scr[hr, :]
        acc_scr[...] += jnp.dot(p_scr[...], v_ref[pl.ds(k0, KEY_CHUNK), :], preferred_element_type=F32)
        return carry

    lax.fori_loop(0, nchunks, att_chunk, 0)
    for h in range(B_HEADS):
        hr = slice(h * Q_BLK, (h + 1) * Q_BLK)
        l = jnp.sum(l_scr[hr, :], axis=-1, keepdims=True)
        o_ref[:, h * HEAD_DIM:(h + 1) * HEAD_DIM] = (acc_scr[hr, :] / l).astype(o_ref.dtype)


def _dsa(proj, aux):
    nbatch, seq, _ = proj.shape
    assert seq % KEY_CHUNK == 0
    topk = min(DSA_TOPK_MAX, seq // 4)
    qw, iw = B_HEADS * HEAD_DIM, IDX_HEADS * IDX_DIM
    full = lambda off: pl.BlockSpec((None, seq, LANES), lambda b, i: (b, 0, off // LANES))
    return pl.pallas_call(
        functools.partial(_dsa_kernel, topk=topk),
        grid=(nbatch, seq // Q_BLK),
        in_specs=[pl.BlockSpec((None, Q_BLK, qw), lambda b, i: (b, i, QB_OFF // qw)),
                  pl.BlockSpec((None, Q_BLK, iw), lambda b, i: (b, i, QI_OFF // iw)),
                  pl.BlockSpec((None, Q_BLK, LANES), lambda b, i: (b, i, 0)),
                  full(KB_OFF), full(KIKI_OFF), full(VB_OFF)],
        out_specs=pl.BlockSpec((None, Q_BLK, qw), lambda b, i: (b, i, 0)),
        out_shape=jax.ShapeDtypeStruct((nbatch, seq, qw), BF16),
        scratch_shapes=[pltpu.VMEM((seq // KEY_CHUNK, Q_BLK, KEY_CHUNK), jnp.int32),
                        pltpu.VMEM((IDX_HEADS, Q_BLK, LANES), F32),
                        pltpu.VMEM((B_HEADS * Q_BLK, HEAD_DIM), BF16),
                        pltpu.VMEM((B_HEADS * Q_BLK, KEY_CHUNK), BF16),
                        pltpu.VMEM((B_HEADS * Q_BLK, LANES), F32),
                        pltpu.VMEM((B_HEADS * Q_BLK, LANES), F32),
                        pltpu.VMEM((B_HEADS * Q_BLK, HEAD_DIM), F32)],
        compiler_params=_params(("parallel", "arbitrary")),
        name="dsa",
    )(proj, proj, aux, proj, proj, proj)


def _out_proj_kernel(oa_ref, ob_ref, w_ref, x_ref, mod_ref, o_ref):
    ka = oa_ref.shape[-1]
    y = jnp.dot(oa_ref[...], w_ref[:ka, :], preferred_element_type=F32)
    y = y + jnp.dot(ob_ref[...], w_ref[ka:, :], preferred_element_type=F32)
    o_ref[...] = x_ref[...] + mod_ref[2:3, :] * y


def _out_proj(o_a, o_b, w_out, x, mod3):
    nbatch, seq, d = x.shape
    ka, kb = o_a.shape[-1], o_b.shape[-1]
    tm, tn = _tile(seq, 512), _tile(d, 512)
    return pl.pallas_call(
        _out_proj_kernel,
        grid=(nbatch, seq // tm, d // tn),
        in_specs=[pl.BlockSpec((None, tm, ka), lambda b, i, j: (b, i, 0)),
                  pl.BlockSpec((None, tm, kb), lambda b, i, j: (b, i, 0)),
                  pl.BlockSpec((ka + kb, tn), lambda b, i, j: (0, j)),
                  pl.BlockSpec((None, tm, tn), lambda b, i, j: (b, i, j)),
                  pl.BlockSpec((None, ADA_CHUNKS, tn), lambda b, i, j: (b, 0, j))],
        out_specs=pl.BlockSpec((None, tm, tn), lambda b, i, j: (b, i, j)),
        out_shape=jax.ShapeDtypeStruct((nbatch, seq, d), F32),
        compiler_params=_params(("parallel", "parallel", "parallel")),
        name="out_proj",
    )(o_a, o_b, w_out, x, mod3)


def _peer_q_kernel(x_ref, mod_ref, g_ref, w_ref, keys_ref, h_ref, st_ref, h_scr):
    j = pl.program_id(2)

    @pl.when(j == 0)
    def _():
        h = _norm_mod(x_ref[...], g_ref[...], mod_ref[4:5, :], mod_ref[3:4, :]).astype(BF16)
        h_scr[...] = h
        h_ref[...] = h

    q = jnp.dot(h_scr[...], w_ref[...], preferred_element_type=F32)
    for blk in range(q.shape[1] // PEER_HALF):
        qh = q[:, blk * PEER_HALF:(blk + 1) * PEER_HALF]
        st_ref[blk] = lax.dot_general(keys_ref[blk], qh, (((1,), (1,)), ((), ())),
                                      preferred_element_type=F32, precision=lax.Precision.HIGHEST)


def _peer_q(x1, mod3, g, w_q, keys):
    nbatch, seq, d = x1.shape
    nq = w_q.shape[1]
    tm, tn = _tile(seq, 512), 4 * PEER_HALF
    nhp = nq // PEER_HALF
    nt = seq // tm
    return pl.pallas_call(
        _peer_q_kernel,
        grid=(nbatch, nt, nq // tn),
        in_specs=[pl.BlockSpec((None, tm, d), lambda b, i, j: (b, i, 0)),
                  pl.BlockSpec((None, ADA_CHUNKS, d), lambda b, i, j: (b, 0, 0)),
                  pl.BlockSpec((1, d), lambda b, i, j: (0, 0)),
                  pl.BlockSpec((d, tn), lambda b, i, j: (0, j)),
                  pl.BlockSpec((tn // PEER_HALF, PEER_NKEYS, PEER_HALF), lambda b, i, j: (j, 0, 0))],
        out_specs=[pl.BlockSpec((None, tm, d), lambda b, i, j: (b, i, 0)),
                   pl.BlockSpec((tn // PEER_HALF, PEER_NKEYS, tm), lambda b, i, j: (j, 0, b * nt + i))],
        out_shape=[jax.ShapeDtypeStruct((nbatch, seq, d), BF16),
                   jax.ShapeDtypeStruct((nhp, PEER_NKEYS, nbatch * seq), F32)],
        scratch_shapes=[pltpu.VMEM((tm, d), BF16)],
        compiler_params=_params(("parallel", "parallel", "arbitrary")),
        name="peer_q",
    )(x1, mod3, g.reshape(1, d), w_q, keys)


def _top_rows(x, count, one_at_a_time):
    rows = []
    ridx = lax.broadcasted_iota(jnp.int32, x.shape, 0)
    for _ in range(count):
        m = jnp.max(x, axis=0, keepdims=True)
        rows.append(m)
        hit = x == m
        if one_at_a_time:
            first = jnp.min(jnp.where(hit, ridx, x.shape[0]), axis=0, keepdims=True)
            hit = ridx == first
        x = jnp.where(hit, -jnp.inf, x)
    removed = jnp.sum(jnp.where(x == -jnp.inf, 1.0, 0.0), axis=0, keepdims=True)
    return rows, removed


def _peer_group_stats(x0, x1, one_at_a_time):
    half = PEER_TOPK // 2
    v0, r0 = _top_rows(x0, PEER_TOPK, one_at_a_time)
    v1, r1 = _top_rows(x1, PEER_TOPK, one_at_a_time)
    v0_all = jnp.concatenate(v0, axis=0)
    v1_all = jnp.concatenate(v1, axis=0)
    cand = [v0[0] + v1_all]
    cand += [v0[a] + v1_all[:half] for a in range(1, half)]
    cand += [v0_all[half:] + v1[0]]
    top, _ = _top_rows(jnp.concatenate(cand, axis=0), PEER_TOPK, True)
    z = jnp.zeros_like(top[0])
    for t in top:
        z = z + jnp.exp(t - top[0])
    stats = jnp.concatenate([top[-1], v0[0], v1[0], 1.0 / z] + [jnp.zeros_like(z)] * 4, axis=0)
    repeated = jnp.max(jnp.maximum(r0, r1)) > float(PEER_TOPK)
    return stats, repeated


def _peer_topk_kernel(st_ref, stats_ref):
    ngroups = st_ref.shape[-1] // LANES

    def per_head(h, carry):
        for tg in range(ngroups):
            cols = slice(tg * LANES, (tg + 1) * LANES)
            stats, repeated = _peer_group_stats(st_ref[2 * h, :, cols], st_ref[2 * h + 1, :, cols], False)
            stats_ref[h, :, cols] = stats

            @pl.when(repeated)
            def _():
                stats_ref[h, :, cols] = _peer_group_stats(st_ref[2 * h, :, cols],
                                                          st_ref[2 * h + 1, :, cols], True)[0]
        return carry

    lax.fori_loop(0, PEER_HEADS, per_head, 0)


def _peer_topk(st):
    nhp, nkeys, ntok = st.shape
    tmk = _tile(ntok, 256)
    return pl.pallas_call(
        _peer_topk_kernel,
        grid=(ntok // tmk,),
        in_specs=[pl.BlockSpec((nhp, nkeys, tmk), lambda i: (0, 0, i))],
        out_specs=pl.BlockSpec((PEER_HEADS, 8, tmk), lambda i: (0, 0, i)),
        out_shape=jax.ShapeDtypeStruct((PEER_HEADS, 8, ntok), F32),
        compiler_params=_params(("parallel",)),
        name="peer_topk",
    )(st)


def _peer_dense_kernel(x_ref, u_ref, v_ref, st_ref, stats_ref, o_ref, e1_scr, at_scr, act_scr):
    e = pl.program_id(1)
    te, tm = u_ref.shape[0], x_ref.shape[0]
    ni = te // PEER_NKEYS

    @pl.when(e == 0)
    def _():
        o_ref[...] = jnp.zeros_like(o_ref)
        for h in range(PEER_HEADS):
            e1_scr[h] = jnp.exp(st_ref[2 * h + 1] - stats_ref[h, 2:3, :]) * stats_ref[h, 3:4, :]

    at_scr[...] = lax.dot_general(u_ref[...], x_ref[...], (((1,), (1,)), ((), ())),
                                  preferred_element_type=F32)
    for il in range(ni):
        i = e * ni + il
        for tc in range(tm // LANES):
            ts = slice(tc * LANES, (tc + 1) * LANES)
            g = jnp.zeros((PEER_NKEYS, LANES), F32)
            for h in range(PEER_HEADS):
                s0 = st_ref[2 * h, pl.ds(i, 1), :][:, ts]
                e0 = jnp.exp(s0 - stats_ref[h, 1:2, ts])
                sel = (s0 + st_ref[2 * h + 1, :, ts]) >= stats_ref[h, 0:1, ts]
                g = g + jnp.where(sel, e0 * e1_scr[h, :, ts], 0.0)
            a = at_scr[il * PEER_NKEYS:(il + 1) * PEER_NKEYS, ts]
            act = 0.5 * a * (1.0 + lax.erf(a * (2.0 ** -0.5))) * g
            act_scr[ts, il * PEER_NKEYS:(il + 1) * PEER_NKEYS] = act.T.astype(BF16)
    o_ref[...] += jnp.dot(act_scr[...], v_ref[...], preferred_element_type=F32)


def _peer_dense(h2, u_tab, v_tab, st, stats):
    ntok, d = h2.shape
    nexp = u_tab.shape[0]
    tm, te = _tile(ntok, 512), 4 * PEER_NKEYS
    nhp = st.shape[0]
    return pl.pallas_call(
        _peer_dense_kernel,
        grid=(ntok // tm, nexp // te),
        in_specs=[pl.BlockSpec((tm, d), lambda i, e: (i, 0)),
                  pl.BlockSpec((te, d), lambda i, e: (e, 0)),
                  pl.BlockSpec((te, d), lambda i, e: (e, 0)),
                  pl.BlockSpec((nhp, PEER_NKEYS, tm), lambda i, e: (0, 0, i)),
                  pl.BlockSpec((PEER_HEADS, 8, tm), lambda i, e: (0, 0, i))],
        out_specs=pl.BlockSpec((tm, d), lambda i, e: (i, 0)),
        out_shape=jax.ShapeDtypeStruct((ntok, d), F32),
        scratch_shapes=[pltpu.VMEM((PEER_HEADS, PEER_NKEYS, tm), F32),
                        pltpu.VMEM((te, tm), F32), pltpu.VMEM((tm, te), BF16)],
        compiler_params=_params(("parallel", "arbitrary")),
        name="peer_dense",
    )(h2, u_tab, v_tab, st, stats)


def _final_kernel(x_ref, p_ref, mod_ref, g_ref, o_ref, *, normalize):
    y = x_ref[...] + mod_ref[5:6, :] * p_ref[...]
    if normalize:
        y = (y * lax.rsqrt(jnp.mean(y * y, axis=-1, keepdims=True) + NORM_EPS)) * g_ref[...]
    o_ref[...] = y


def _final(x1, peer_out, mod3, g, normalize):
    nbatch, seq, d = x1.shape
    tm = _tile(seq, 256)
    spec = pl.BlockSpec((None, tm, d), lambda b, i: (b, i, 0))
    return pl.pallas_call(
        functools.partial(_final_kernel, normalize=normalize),
        grid=(nbatch, seq // tm),
        in_specs=[spec, spec,
                  pl.BlockSpec((None, ADA_CHUNKS, d), lambda b, i: (b, 0, 0)),
                  pl.BlockSpec((1, d), lambda b, i: (0, 0))],
        out_specs=spec,
        out_shape=jax.ShapeDtypeStruct((nbatch, seq, d), F32),
        compiler_params=_params(("parallel", "parallel")),
        name="final",
    )(x1, peer_out, mod3, g.reshape(1, d))


def kernel(x, c, positions, ln1_g, ln2_g, w_ada, b_ada, w_in, w_out, peer_wq, peer_sub_keys, peer_u,
           peer_v, lnf_g):
    nbatch, seq, d = x.shape
    depth = w_ada.shape[0]
    tables = _rope_tables(positions)
    for layer in range(depth):
        mod3 = _ada(c, w_ada[layer], b_ada[layer]).reshape(nbatch, ADA_CHUNKS, d)
        pa, proj, aux = _in_proj(x, mod3, ln1_g[layer], _permute_w_in(w_in[layer]), tables)
        o_a = _attn_a(pa)
        o_b = _dsa(proj, aux)
        x = _out_proj(o_a, o_b, w_out[layer].astype(BF16), x, mod3)
        keys = peer_sub_keys[layer].reshape(2 * PEER_HEADS, PEER_NKEYS, PEER_HALF)
        h2, st = _peer_q(x, mod3, ln2_g[layer], peer_wq[layer].astype(BF16), keys)
        stats = _peer_topk(st)
        peer_out = _peer_dense(h2.reshape(nbatch * seq, d), peer_u[layer].astype(BF16),
                               peer_v[layer].astype(BF16), st, stats)
        x = _final(x, peer_out.reshape(nbatch, seq, d), mod3, lnf_g, normalize=layer + 1 == depth)
    return x
```

```python
import functools
import math

import jax
import jax.numpy as jnp
from jax import lax
from jax.experimental import pallas as pl
from jax.experimental.pallas import tpu as pltpu

F32 = jnp.float32
BF16 = jnp.bfloat16

HEAD_DIM = 128
A_HEADS = 16
A_KV_HEADS = 4
A_GROUP = A_HEADS // A_KV_HEADS
DILATED_BRANCHES = ((128, 1), (512, 4), (2048, 16))
WIN_BLK = 128
B_HEADS = 16
IDX_HEADS = 16
IDX_DIM = 64
DSA_TOPK_MAX = 256
Q_BLK = 128
PEER_HEADS = 8
PEER_NKEYS = 128
PEER_HALF = 128
PEER_TOPK = 16
ROPE_THETA = 10000.0
NORM_EPS = 1e-6
NEG = -1e30
ADA_CHUNKS = 6
INT_MIN = -(2 ** 31)

LANES = 128
VMEM_LIMIT = 56 * 1024 * 1024

PROJ_TILE = 512
A_TILES = (A_HEADS + 2 * A_KV_HEADS) * HEAD_DIM // PROJ_TILE
KA_TILE = A_HEADS * HEAD_DIM // PROJ_TILE
VA_TILE = KA_TILE + 1
PA_HEADS = A_HEADS + 2 * A_KV_HEADS
QB_OFF = 0
QI_OFF = QB_OFF + B_HEADS * HEAD_DIM
MIX_OFF = QI_OFF + IDX_HEADS * IDX_DIM
KB_OFF = MIX_OFF
KIKI_OFF = MIX_OFF + LANES
VB_OFF = MIX_OFF + 2 * LANES
WI_OFF = MIX_OFF + 3 * LANES
P_COLS = MIX_OFF + PROJ_TILE
PROJ_TILES = A_TILES + P_COLS // PROJ_TILE


def _params(semantics):
    return pltpu.CompilerParams(dimension_semantics=semantics, vmem_limit_bytes=VMEM_LIMIT)


def _tile(n, pref):
    return pref if n % pref == 0 else n


def _norm_mod(x, g, scale, shift):
    xf = x.astype(F32)
    y = xf * lax.rsqrt(jnp.mean(xf * xf, axis=-1, keepdims=True) + NORM_EPS)
    return (y * g) * (1.0 + scale) + shift


def _ada_kernel(ct_ref, w_ref, b_ref, o_ref, *, nbatch):
    @pl.when(pl.program_id(1) == 0)
    def _():
        o_ref[...] = jnp.broadcast_to(b_ref[...], o_ref.shape)

    ct = ct_ref[...]
    s = ct * jax.nn.sigmoid(ct)
    w = w_ref[...]
    rows = [jnp.sum(w * s[:, b:b + 1], axis=0, keepdims=True) for b in range(nbatch)]
    o_ref[...] += jnp.concatenate(rows, axis=0)


def _ada(c, w_ada, b_ada):
    nbatch, d = c.shape
    n = w_ada.shape[1]
    tk, tn = _tile(d, 512), _tile(n, 2048)
    return pl.pallas_call(
        functools.partial(_ada_kernel, nbatch=nbatch),
        grid=(n // tn, d // tk),
        in_specs=[pl.BlockSpec((tk, nbatch), lambda j, k: (k, 0)),
                  pl.BlockSpec((tk, tn), lambda j, k: (k, j)),
                  pl.BlockSpec((1, tn), lambda j, k: (0, j))],
        out_specs=pl.BlockSpec((nbatch, tn), lambda j, k: (0, j)),
        out_shape=jax.ShapeDtypeStruct((nbatch, n), F32),
        compiler_params=_params(("parallel", "arbitrary")),
        name="ada",
    )(c.T, w_ada, b_ada.reshape(1, n))


def _rope_kernel(pos_ref, ch_ref, sh_ref, ci_ref, sa_ref, sb_ref):
    pos = pos_ref[...].astype(F32)
    lane = lax.broadcasted_iota(jnp.int32, (1, LANES), 1)
    expo_h = -((2 * (lane & 63)).astype(F32)) / HEAD_DIM
    expo_i = -((2 * (lane & 31)).astype(F32)) / IDX_DIM
    inv = jnp.power(ROPE_THETA, jnp.where(lane < 64, expo_h, expo_i))
    ang = pos * inv
    c = jnp.cos(ang)
    s = jnp.sin(ang)
    ch_ref[...] = jnp.where(lane < 64, c, pltpu.roll(c, 64, 1))
    sh_ref[...] = jnp.where(lane < 64, -s, pltpu.roll(s, 64, 1))
    grp = lane >> 5
    c64, c96, c32 = pltpu.roll(c, 64, 1), pltpu.roll(c, 96, 1), pltpu.roll(c, 32, 1)
    s64, s96, s32 = pltpu.roll(s, 64, 1), pltpu.roll(s, 96, 1), pltpu.roll(s, 32, 1)
    ci_ref[...] = jnp.where(grp == 0, c64, jnp.where(grp == 1, c96, jnp.where(grp == 2, c, c32)))
    sa_ref[...] = jnp.where(grp == 0, -s64, jnp.where(grp == 2, -s, 0.0))
    sb_ref[...] = jnp.where(grp == 1, s96, jnp.where(grp == 3, s32, 0.0))


def _rope_tables(positions):
    nbatch, seq = positions.shape
    ts = _tile(seq, 1024)
    spec = pl.BlockSpec((None, ts, LANES), lambda b, i: (b, i, 0))
    shape = jax.ShapeDtypeStruct((nbatch, seq, LANES), F32)
    return pl.pallas_call(
        _rope_kernel,
        grid=(nbatch, seq // ts),
        in_specs=[pl.BlockSpec((None, ts, 1), lambda b, i: (b, i, 0))],
        out_specs=[spec] * 5,
        out_shape=[shape] * 5,
        compiler_params=_params(("parallel", "parallel")),
        name="rope",
    )(positions.reshape(nbatch, seq, 1))


def _rope_head(a, ch, sh):
    return a * ch + pltpu.roll(a, 64, 1) * sh


def _rope_idx(a, ci, sa, sb):
    return a * ci + pltpu.roll(a, 96, 1) * sa + pltpu.roll(a, 32, 1) * sb


def _in_proj_kernel(x_ref, mod_ref, g_ref, w_ref, ch_ref, sh_ref, ci_ref, sa_ref, sb_ref,
                    pa_ref, p_ref, aux_ref, h_scr, acc_scr):
    j = pl.program_id(2)

    @pl.when(j == 0)
    def _():
        h_scr[...] = _norm_mod(x_ref[...], g_ref[...], mod_ref[1:2, :], mod_ref[0:1, :]).astype(BF16)

    acc_scr[...] = jnp.dot(h_scr[...], w_ref[...], preferred_element_type=F32)
    nblk = PROJ_TILE // LANES
    qscale = HEAD_DIM ** -0.5 * math.log2(math.e)
    qb_tile = A_TILES + QB_OFF // PROJ_TILE
    qi_tile = A_TILES + QI_OFF // PROJ_TILE
    mix_tile = A_TILES + MIX_OFF // PROJ_TILE

    def blk(q):
        return acc_scr[:, q * LANES:(q + 1) * LANES]

    def put(q, val):
        p_ref[:, q * LANES:(q + 1) * LANES] = val.astype(BF16)

    @pl.when(j < KA_TILE)
    def _():
        for q in range(nblk):
            pa_ref[q] = _rope_head(blk(q), ch_ref[...], sh_ref[...]) * qscale

    @pl.when(j == KA_TILE)
    def _():
        for q in range(nblk):
            pa_ref[q] = _rope_head(blk(q), ch_ref[...], sh_ref[...])

    @pl.when(j == VA_TILE)
    def _():
        for q in range(nblk):
            pa_ref[q] = blk(q)

    @pl.when((j >= qb_tile) & (j < qi_tile))
    def _():
        for q in range(nblk):
            put(q, _rope_head(blk(q), ch_ref[...], sh_ref[...]) * qscale)

    @pl.when((j >= qi_tile) & (j < mix_tile))
    def _():
        for q in range(nblk):
            put(q, _rope_idx(blk(q), ci_ref[...], sa_ref[...], sb_ref[...]))

    @pl.when(j == mix_tile)
    def _():
        put(0, _rope_head(blk(0), ch_ref[...], sh_ref[...]))
        put(1, _rope_idx(blk(1), ci_ref[...], sa_ref[...], sb_ref[...]))
        put(2, blk(2))
        put(3, blk(3))
        aux_ref[...] = blk(3)


def _permute_w_in(w_in):
    d = w_in.shape[0]
    w = w_in.astype(BF16)
    sizes = (A_HEADS * HEAD_DIM, A_KV_HEADS * HEAD_DIM, A_KV_HEADS * HEAD_DIM, B_HEADS * HEAD_DIM,
             HEAD_DIM, HEAD_DIM, IDX_HEADS * IDX_DIM, IDX_DIM, IDX_HEADS)
    offs = [0]
    for sz in sizes:
        offs.append(offs[-1] + sz)
    assert w.shape[1] == offs[-1]
    qa, ka, va, qb, kb, vb, qi, ki, wi = [w[:, offs[i]:offs[i + 1]] for i in range(len(sizes))]
    pad = jnp.zeros((d, LANES - IDX_HEADS), BF16)
    out = jnp.concatenate([qa, ka, va, qb, qi, kb, ki, ki, vb, wi, pad], axis=1)
    assert out.shape[1] == PROJ_TILES * PROJ_TILE
    return out


def _in_proj(x, mod3, g, w_perm, tables):
    nbatch, seq, d = x.shape
    tm = _tile(seq, 512)
    nblk = PROJ_TILE // LANES
    tab_spec = pl.BlockSpec((None, tm, LANES), lambda b, i, j: (b, i, 0))
    return pl.pallas_call(
        _in_proj_kernel,
        grid=(nbatch, seq // tm, PROJ_TILES),
        in_specs=[pl.BlockSpec((None, tm, d), lambda b, i, j: (b, i, 0)),
                  pl.BlockSpec((None, ADA_CHUNKS, d), lambda b, i, j: (b, 0, 0)),
                  pl.BlockSpec((1, d), lambda b, i, j: (0, 0)),
                  pl.BlockSpec((d, PROJ_TILE), lambda b, i, j: (0, j))] + [tab_spec] * 5,
        out_specs=[pl.BlockSpec((None, nblk, tm, LANES), lambda b, i, j: (b, jnp.minimum(j, A_TILES - 1), i, 0)),
                   pl.BlockSpec((None, tm, PROJ_TILE), lambda b, i, j: (b, i, jnp.maximum(j - A_TILES, 0))),
                   pl.BlockSpec((None, tm, LANES), lambda b, i, j: (b, i, 0))],
        out_shape=[jax.ShapeDtypeStruct((nbatch, PA_HEADS, seq, LANES), F32),
                   jax.ShapeDtypeStruct((nbatch, seq, P_COLS), BF16),
                   jax.ShapeDtypeStruct((nbatch, seq, LANES), F32)],
        scratch_shapes=[pltpu.VMEM((tm, d), BF16), pltpu.VMEM((tm, PROJ_TILE), F32)],
        compiler_params=_params(("parallel", "parallel", "arbitrary")),
        name="in_proj",
    )(x, mod3, g.reshape(1, d), w_perm, *tables)


def _attn_a_kernel(q_ref, k_ref, v_ref, o_ref, acc_scr, lse_scr, bias_scr, *, dilations):
    half = pl.program_id(2)
    qrows = q_ref.shape[1]
    nblk = qrows // WIN_BLK

    rows = A_GROUP * WIN_BLK
    row = lax.broadcasted_iota(jnp.int32, (rows, 2 * WIN_BLK), 0) & (WIN_BLK - 1)
    col = lax.broadcasted_iota(jnp.int32, (rows, 2 * WIN_BLK), 1)
    band = (col >= row) & (col <= row + WIN_BLK)
    bias_scr[0] = jnp.where(band & (col >= WIN_BLK), 0.0, NEG)
    bias_scr[1] = jnp.where(band, 0.0, NEG)

    def rows_at(start, dil):
        return pl.ds(start, WIN_BLK) if dil == 1 else pl.ds(start, WIN_BLK, stride=dil)

    for idx, dil in enumerate(dilations):
        per_res = nblk // dil
        first, last = idx == 0, idx == len(dilations) - 1

        def body(blk, carry, dil=dil, per_res=per_res, first=first, last=last):
            r = blk // per_res
            n_loc = blk - r * per_res
            n = half * per_res + n_loc
            q0 = r + dil * WIN_BLK * n_loc
            k0 = r + dil * WIN_BLK * n
            kp = r + dil * WIN_BLK * jnp.maximum(n - 1, 0)
            q4 = jnp.concatenate([q_ref[g, rows_at(q0, dil), :] for g in range(A_GROUP)], axis=0).astype(BF16)
            kc = jnp.concatenate([k_ref[rows_at(kp, dil), :], k_ref[rows_at(k0, dil), :]], axis=0).astype(BF16)
            vc = jnp.concatenate([v_ref[rows_at(kp, dil), :], v_ref[rows_at(k0, dil), :]], axis=0).astype(BF16)
            s = lax.dot_general(q4, kc, (((1,), (1,)), ((), ())), preferred_element_type=F32)
            s = s + bias_scr[jnp.minimum(n, 1)]
            m = jnp.max(s, axis=-1, keepdims=True)
            p = jnp.exp2(s - m)
            l = jnp.sum(p, axis=-1, keepdims=True)
            o = jnp.dot(p.astype(BF16), vc, preferred_element_type=F32) / l
            lse = m + jnp.log2(l)
            for g in range(A_GROUP):
                og = o[g * WIN_BLK:(g + 1) * WIN_BLK]
                lg = jnp.broadcast_to(lse[g * WIN_BLK:(g + 1) * WIN_BLK], (WIN_BLK, HEAD_DIM))
                if not first:
                    opg = acc_scr[g, rows_at(q0, dil), :]
                    lpg = lse_scr[g, rows_at(q0, dil), :]
                    w_prev = 1.0 / (1.0 + jnp.exp2(lg - lpg))
                    og = og + (opg - og) * w_prev
                    if not last:
                        lg = jnp.maximum(lpg, lg) + jnp.log2(1.0 + jnp.exp2(-jnp.abs(lg - lpg)))
                if last:
                    o_ref[pl.ds(pl.multiple_of(q0, WIN_BLK), WIN_BLK), g * HEAD_DIM:(g + 1) * HEAD_DIM] = (
                        og.astype(o_ref.dtype))
                else:
                    acc_scr[g, rows_at(q0, dil), :] = og
                    lse_scr[g, rows_at(q0, dil), :] = lg
            return carry

        lax.fori_loop(0, nblk, body, 0, unroll=2)


def _attn_a(pa):
    nbatch, _, seq, _ = pa.shape
    dilations = tuple(sorted((dil for _, dil in DILATED_BRANCHES), reverse=True))
    assert dilations[-1] == 1 and all(w // dil == WIN_BLK for w, dil in DILATED_BRANCHES)
    nhalf = 2 if seq % (2 * WIN_BLK * dilations[0]) == 0 else 1
    qrows = seq // nhalf
    assert qrows % (WIN_BLK * dilations[0]) == 0
    return pl.pallas_call(
        functools.partial(_attn_a_kernel, dilations=dilations),
        grid=(nbatch, A_KV_HEADS, nhalf),
        in_specs=[pl.BlockSpec((None, A_GROUP, qrows, HEAD_DIM), lambda b, h, t: (b, h, t, 0)),
                  pl.BlockSpec((None, None, seq, HEAD_DIM), lambda b, h, t: (b, A_HEADS + h, 0, 0)),
                  pl.BlockSpec((None, None, seq, HEAD_DIM),
                               lambda b, h, t: (b, A_HEADS + A_KV_HEADS + h, 0, 0))],
        out_specs=pl.BlockSpec((None, qrows, A_GROUP * HEAD_DIM), lambda b, h, t: (b, t, h)),
        out_shape=jax.ShapeDtypeStruct((nbatch, seq, A_HEADS * HEAD_DIM), BF16),
        scratch_shapes=[pltpu.VMEM((A_GROUP, qrows, HEAD_DIM), F32),
                        pltpu.VMEM((A_GROUP, qrows, HEAD_DIM), F32),
                        pltpu.VMEM((2, A_GROUP * WIN_BLK, 2 * WIN_BLK), F32)],
        compiler_params=_params(("parallel", "parallel", "arbitrary")),
        name="attn_a",
    )(pa, pa, pa)


KEY_CHUNK = 256


def _dsa_kernel(qb_ref, qi_ref, wi_ref, k_ref, kk_ref, v_ref, o_ref,
                key_scr, wib_scr, q_scr, p_scr, m_scr, l_scr, acc_scr, *, topk):
    i = pl.program_id(1)
    nchunks = (i + 2) // 2
    halves = KEY_CHUNK // LANES

    wi = wi_ref[...] * (IDX_HEADS ** -0.5 * IDX_DIM ** -0.5)
    lane = lax.broadcasted_iota(jnp.int32, (Q_BLK, LANES), 1)
    qi = qi_ref[...]
    for h in range(IDX_HEADS):
        blk = qi[:, (h // 2) * LANES:(h // 2 + 1) * LANES]
        keep = (lane < IDX_DIM) if h % 2 == 0 else (lane >= IDX_DIM)
        q_scr[h * Q_BLK:(h + 1) * Q_BLK, :] = jnp.where(keep, blk, jnp.zeros_like(blk))
        wib_scr[h] = jnp.broadcast_to(wi[:, h:h + 1], (Q_BLK, LANES))
    qpos = i * Q_BLK + lax.broadcasted_iota(jnp.int32, (Q_BLK, KEY_CHUNK), 0)
    kiota = lax.broadcasted_iota(jnp.int32, (Q_BLK, KEY_CHUNK), 1)

    def score_chunk(c, carry):
        k0 = pl.multiple_of(c * KEY_CHUNK, KEY_CHUNK)
        z = lax.dot_general(q_scr[...], kk_ref[pl.ds(k0, KEY_CHUNK), :], (((1,), (1,)), ((), ())),
                            preferred_element_type=F32)
        sc = jnp.zeros((Q_BLK, KEY_CHUNK), F32)
        for h in range(IDX_HEADS):
            w = wib_scr[h]
            sc = sc + jnp.concatenate([w] * halves, axis=1) * jnp.maximum(z[h * Q_BLK:(h + 1) * Q_BLK], 0.0)
        bits = pltpu.bitcast(sc, jnp.int32)
        key = bits ^ ((bits >> 31) & jnp.int32(0x7FFFFFFF))
        key_scr[c] = jnp.where(k0 + kiota <= qpos, key, jnp.int32(INT_MIN))
        return carry

    lax.fori_loop(0, nchunks, score_chunk, 0)

    def bit_body(t, ans):
        trial = ans | jnp.left_shift(jnp.int32(1), 31 - t)
        thr = trial ^ jnp.int32(INT_MIN)

        def count_chunk(c, acc):
            key = key_scr[c]
            for q in range(halves):
                acc = acc + jnp.where(key[:, q * LANES:(q + 1) * LANES] >= thr, 1.0, 0.0)
            return acc

        acc = lax.fori_loop(0, nchunks, count_chunk, jnp.zeros((Q_BLK, LANES), F32))
        cnt = jnp.sum(acc, axis=1, keepdims=True)
        return jnp.where(cnt >= float(topk), trial, ans)

    ans = lax.fori_loop(0, 32, bit_body, jnp.zeros((Q_BLK, LANES), jnp.int32))
    tau = jnp.maximum(ans ^ jnp.int32(INT_MIN), jnp.int32(INT_MIN + 1))
    tau2 = jnp.concatenate([tau] * halves, axis=1)

    for h in range(B_HEADS):
        q_scr[h * Q_BLK:(h + 1) * Q_BLK, :] = qb_ref[:, h * HEAD_DIM:(h + 1) * HEAD_DIM]
    m_scr[...] = jnp.full(m_scr.shape, NEG, F32)
    l_scr[...] = jnp.zeros(l_scr.shape, F32)
    acc_scr[...] = jnp.zeros(acc_scr.shape, F32)

    def att_chunk(c, carry):
        k0 = pl.multiple_of(c * KEY_CHUNK, KEY_CHUNK)
        s = lax.dot_general(q_scr[...], k_ref[pl.ds(k0, KEY_CHUNK), :], (((1,), (1,)), ((), ())),
                            preferred_element_type=F32)
        bias = jnp.where(key_scr[c] >= tau2, 0.0, NEG)
        for h in range(B_HEADS):
            hr = slice(h * Q_BLK, (h + 1) * Q_BLK)
            sh = s[hr] + bias
            m_old = m_scr[hr, :]
            m_new = jnp.maximum(m_old, jnp.max(sh, axis=-1, keepdims=True))
            alpha = jnp.exp2(m_old - m_new)
            p = jnp.exp2(sh - jnp.concatenate([m_new] * halves, axis=1))
            psum = p[:, :LANES]
            for q in range(1, halves):
                psum = psum + p[:, q * LANES:(q + 1) * LANES]
            l_scr[hr, :] = alpha * l_scr[hr, :] + psum
            m_scr[hr, :] = m_new
            p_scr[hr, :] = p.astype(BF16)
            acc_scr[hr, :] = alpha * acc_scr[hr, :]
        acc_scr[...] += jnp.dot(p_scr[...], v_ref[pl.ds(k0, KEY_CHUNK), :], preferred_element_type=F32)
        return carry

    lax.fori_loop(0, nchunks, att_chunk, 0)
    for h in range(B_HEADS):
        hr = slice(h * Q_BLK, (h + 1) * Q_BLK)
        l = jnp.sum(l_scr[hr, :], axis=-1, keepdims=True)
        o_ref[:, h * HEAD_DIM:(h + 1) * HEAD_DIM] = (acc_scr[hr, :] / l).astype(o_ref.dtype)


def _dsa(proj, aux):
    nbatch, seq, _ = proj.shape
    assert seq % KEY_CHUNK == 0
    topk = min(DSA_TOPK_MAX, seq // 4)
    qw, iw = B_HEADS * HEAD_DIM, IDX_HEADS * IDX_DIM
    full = lambda off: pl.BlockSpec((None, seq, LANES), lambda b, i: (b, 0, off // LANES))
    return pl.pallas_call(
        functools.partial(_dsa_kernel, topk=topk),
        grid=(nbatch, seq // Q_BLK),
        in_specs=[pl.BlockSpec((None, Q_BLK, qw), lambda b, i: (b, i, QB_OFF // qw)),
                  pl.BlockSpec((None, Q_BLK, iw), lambda b, i: (b, i, QI_OFF // iw)),
                  pl.BlockSpec((None, Q_BLK, LANES), lambda b, i: (b, i, 0)),
                  full(KB_OFF), full(KIKI_OFF), full(VB_OFF)],
        out_specs=pl.BlockSpec((None, Q_BLK, qw), lambda b, i: (b, i, 0)),
        out_shape=jax.ShapeDtypeStruct((nbatch, seq, qw), BF16),
        scratch_shapes=[pltpu.VMEM((seq // KEY_CHUNK, Q_BLK, KEY_CHUNK), jnp.int32),
                        pltpu.VMEM((IDX_HEADS, Q_BLK, LANES), F32),
                        pltpu.VMEM((B_HEADS * Q_BLK, HEAD_DIM), BF16),
                        pltpu.VMEM((B_HEADS * Q_BLK, KEY_CHUNK), BF16),
                        pltpu.VMEM((B_HEADS * Q_BLK, LANES), F32),
                        pltpu.VMEM((B_HEADS * Q_BLK, LANES), F32),
                        pltpu.VMEM((B_HEADS * Q_BLK, HEAD_DIM), F32)],
        compiler_params=_params(("parallel", "arbitrary")),
        name="dsa",
    )(proj, proj, aux, proj, proj, proj)


def _out_proj_kernel(oa_ref, ob_ref, w_ref, x_ref, mod_ref, o_ref):
    ka = oa_ref.shape[-1]
    y = jnp.dot(oa_ref[...], w_ref[:ka, :], preferred_element_type=F32)
    y = y + jnp.dot(ob_ref[...], w_ref[ka:, :], preferred_element_type=F32)
    o_ref[...] = x_ref[...] + mod_ref[2:3, :] * y


def _out_proj(o_a, o_b, w_out, x, mod3):
    nbatch, seq, d = x.shape
    ka, kb = o_a.shape[-1], o_b.shape[-1]
    tm, tn = _tile(seq, 512), _tile(d, 512)
    return pl.pallas_call(
        _out_proj_kernel,
        grid=(nbatch, seq // tm, d // tn),
        in_specs=[pl.BlockSpec((None, tm, ka), lambda b, i, j: (b, i, 0)),
                  pl.BlockSpec((None, tm, kb), lambda b, i, j: (b, i, 0)),
                  pl.BlockSpec((ka + kb, tn), lambda b, i, j: (0, j)),
                  pl.BlockSpec((None, tm, tn), lambda b, i, j: (b, i, j)),
                  pl.BlockSpec((None, ADA_CHUNKS, tn), lambda b, i, j: (b, 0, j))],
        out_specs=pl.BlockSpec((None, tm, tn), lambda b, i, j: (b, i, j)),
        out_shape=jax.ShapeDtypeStruct((nbatch, seq, d), F32),
        compiler_params=_params(("parallel", "parallel", "parallel")),
        name="out_proj",
    )(o_a, o_b, w_out, x, mod3)


def _peer_q_kernel(x_ref, mod_ref, g_ref, w_ref, keys_ref, h_ref, st_ref, h_scr):
    j = pl.program_id(2)

    @pl.when(j == 0)
    def _():
        h = _norm_mod(x_ref[...], g_ref[...], mod_ref[4:5, :], mod_ref[3:4, :]).astype(BF16)
        h_scr[...] = h
        h_ref[...] = h

    q = jnp.dot(h_scr[...], w_ref[...], preferred_element_type=F32)
    for blk in range(q.shape[1] // PEER_HALF):
        qh = q[:, blk * PEER_HALF:(blk + 1) * PEER_HALF]
        st_ref[blk] = lax.dot_general(keys_ref[blk], qh, (((1,), (1,)), ((), ())),
                                      preferred_element_type=F32, precision=lax.Precision.HIGHEST)


def _peer_q(x1, mod3, g, w_q, keys):
    nbatch, seq, d = x1.shape
    nq = w_q.shape[1]
    tm, tn = _tile(seq, 512), 4 * PEER_HALF
    nhp = nq // PEER_HALF
    nt = seq // tm
    return pl.pallas_call(
        _peer_q_kernel,
        grid=(nbatch, nt, nq // tn),
        in_specs=[pl.BlockSpec((None, tm, d), lambda b, i, j: (b, i, 0)),
                  pl.BlockSpec((None, ADA_CHUNKS, d), lambda b, i, j: (b, 0, 0)),
                  pl.BlockSpec((1, d), lambda b, i, j: (0, 0)),
                  pl.BlockSpec((d, tn), lambda b, i, j: (0, j)),
                  pl.BlockSpec((tn // PEER_HALF, PEER_NKEYS, PEER_HALF), lambda b, i, j: (j, 0, 0))],
        out_specs=[pl.BlockSpec((None, tm, d), lambda b, i, j: (b, i, 0)),
                   pl.BlockSpec((tn // PEER_HALF, PEER_NKEYS, tm), lambda b, i, j: (j, 0, b * nt + i))],
        out_shape=[jax.ShapeDtypeStruct((nbatch, seq, d), BF16),
                   jax.ShapeDtypeStruct((nhp, PEER_NKEYS, nbatch * seq), F32)],
        scratch_shapes=[pltpu.VMEM((tm, d), BF16)],
        compiler_params=_params(("parallel", "parallel", "arbitrary")),
        name="peer_q",
    )(x1, mod3, g.reshape(1, d), w_q, keys)


def _top_rows(x, count, one_at_a_time):
    rows = []
    ridx = lax.broadcasted_iota(jnp.int32, x.shape, 0)
    for _ in range(count):
        m = jnp.max(x, axis=0, keepdims=True)
        rows.append(m)
        hit = x == m
        if one_at_a_time:
            first = jnp.min(jnp.where(hit, ridx, x.shape[0]), axis=0, keepdims=True)
            hit = ridx == first
        x = jnp.where(hit, -jnp.inf, x)
    removed = jnp.sum(jnp.where(x == -jnp.inf, 1.0, 0.0), axis=0, keepdims=True)
    return rows, removed


def _peer_group_stats(x0, x1, one_at_a_time):
    half = PEER_TOPK // 2
    v0, r0 = _top_rows(x0, PEER_TOPK, one_at_a_time)
    v1, r1 = _top_rows(x1, PEER_TOPK, one_at_a_time)
    v0_all = jnp.concatenate(v0, axis=0)
    v1_all = jnp.concatenate(v1, axis=0)
    cand = [v0[0] + v1_all]
    cand += [v0[a] + v1_all[:half] for a in range(1, half)]
    cand += [v0_all[half:] + v1[0]]
    top, rc = _top_rows(jnp.concatenate(cand, axis=0), PEER_TOPK, one_at_a_time)
    z = jnp.zeros_like(top[0])
    for t in top:
        z = z + jnp.exp(t - top[0])
    stats = jnp.concatenate([top[-1], v0[0], v1[0], 1.0 / z] + [jnp.zeros_like(z)] * 4, axis=0)
    repeated = jnp.max(jnp.maximum(jnp.maximum(r0, r1), rc)) > float(PEER_TOPK)
    return stats, repeated


def _peer_topk_kernel(st_ref, stats_ref):
    ngroups = st_ref.shape[-1] // LANES

    def groups(h, one_at_a_time):
        out = []
        for tg in range(ngroups):
            cols = slice(tg * LANES, (tg + 1) * LANES)
            out.append(_peer_group_stats(st_ref[2 * h, :, cols], st_ref[2 * h + 1, :, cols], one_at_a_time))
        return out

    def per_head(h, carry):
        fast = groups(h, False)
        repeated = fast[0][1]
        for tg in range(ngroups):
            stats_ref[h, :, tg * LANES:(tg + 1) * LANES] = fast[tg][0]
            repeated = repeated | fast[tg][1]

        @pl.when(repeated)
        def _():
            for tg, (stats, _) in enumerate(groups(h, True)):
                stats_ref[h, :, tg * LANES:(tg + 1) * LANES] = stats
        return carry

    lax.fori_loop(0, PEER_HEADS, per_head, 0)


def _peer_topk(st):
    nhp, nkeys, ntok = st.shape
    tmk = _tile(ntok, 256)
    return pl.pallas_call(
        _peer_topk_kernel,
        grid=(ntok // tmk,),
        in_specs=[pl.BlockSpec((nhp, nkeys, tmk), lambda i: (0, 0, i))],
        out_specs=pl.BlockSpec((PEER_HEADS, 8, tmk), lambda i: (0, 0, i)),
        out_shape=jax.ShapeDtypeStruct((PEER_HEADS, 8, ntok), F32),
        compiler_params=_params(("parallel",)),
        name="peer_topk",
    )(st)


def _peer_dense_kernel(x_ref, u_ref, v_ref, st_ref, stats_ref, o_ref, e1_scr, at_scr, act_scr):
    e = pl.program_id(1)
    te, tm = u_ref.shape[0], x_ref.shape[0]
    ni = te // PEER_NKEYS

    @pl.when(e == 0)
    def _():
        o_ref[...] = jnp.zeros_like(o_ref)
        for h in range(PEER_HEADS):
            e1_scr[h] = jnp.exp(st_ref[2 * h + 1] - stats_ref[h, 2:3, :]) * stats_ref[h, 3:4, :]

    at_scr[...] = lax.dot_general(u_ref[...], x_ref[...], (((1,), (1,)), ((), ())),
                                  preferred_element_type=F32)
    for il in range(ni):
        i = e * ni + il
        for tc in range(tm // LANES):
            ts = slice(tc * LANES, (tc + 1) * LANES)
            g = jnp.zeros((PEER_NKEYS, LANES), F32)
            for h in range(PEER_HEADS):
                s0 = st_ref[2 * h, pl.ds(i, 1), :][:, ts]
                e0 = jnp.exp(s0 - stats_ref[h, 1:2, ts])
                sel = (s0 + st_ref[2 * h + 1, :, ts]) >= stats_ref[h, 0:1, ts]
                g = g + jnp.where(sel, e0 * e1_scr[h, :, ts], 0.0)
            a = at_scr[il * PEER_NKEYS:(il + 1) * PEER_NKEYS, ts]
            act = 0.5 * a * (1.0 + lax.erf(a * (2.0 ** -0.5))) * g
            act_scr[ts, il * PEER_NKEYS:(il + 1) * PEER_NKEYS] = act.T.astype(BF16)
    o_ref[...] += jnp.dot(act_scr[...], v_ref[...], preferred_element_type=F32)


def _peer_dense(h2, u_tab, v_tab, st, stats):
    ntok, d = h2.shape
    nexp = u_tab.shape[0]
    tm, te = _tile(ntok, 512), 4 * PEER_NKEYS
    nhp = st.shape[0]
    return pl.pallas_call(
        _peer_dense_kernel,
        grid=(ntok // tm, nexp // te),
        in_specs=[pl.BlockSpec((tm, d), lambda i, e: (i, 0)),
                  pl.BlockSpec((te, d), lambda i, e: (e, 0)),
                  pl.BlockSpec((te, d), lambda i, e: (e, 0)),
                  pl.BlockSpec((nhp, PEER_NKEYS, tm), lambda i, e: (0, 0, i)),
                  pl.BlockSpec((PEER_HEADS, 8, tm), lambda i, e: (0, 0, i))],
        out_specs=pl.BlockSpec((tm, d), lambda i, e: (i, 0)),
        out_shape=jax.ShapeDtypeStruct((ntok, d), F32),
        scratch_shapes=[pltpu.VMEM((PEER_HEADS, PEER_NKEYS, tm), F32),
                        pltpu.VMEM((te, tm), F32), pltpu.VMEM((tm, te), BF16)],
        compiler_params=_params(("parallel", "arbitrary")),
        name="peer_dense",
    )(h2, u_tab, v_tab, st, stats)


def _final_kernel(x_ref, p_ref, mod_ref, g_ref, o_ref, *, normalize):
    y = x_ref[...] + mod_ref[5:6, :] * p_ref[...]
    if normalize:
        y = (y * lax.rsqrt(jnp.mean(y * y, axis=-1, keepdims=True) + NORM_EPS)) * g_ref[...]
    o_ref[...] = y


def _final(x1, peer_out, mod3, g, normalize):
    nbatch, seq, d = x1.shape
    tm = _tile(seq, 256)
    spec = pl.BlockSpec((None, tm, d), lambda b, i: (b, i, 0))
    return pl.pallas_call(
        functools.partial(_final_kernel, normalize=normalize),
        grid=(nbatch, seq // tm),
        in_specs=[spec, spec,
                  pl.BlockSpec((None, ADA_CHUNKS, d), lambda b, i: (b, 0, 0)),
                  pl.BlockSpec((1, d), lambda b, i: (0, 0))],
        out_specs=spec,
        out_shape=jax.ShapeDtypeStruct((nbatch, seq, d), F32),
        compiler_params=_params(("parallel", "parallel")),
        name="final",
    )(x1, peer_out, mod3, g.reshape(1, d))


def kernel(x, c, positions, ln1_g, ln2_g, w_ada, b_ada, w_in, w_out, peer_wq, peer_sub_keys, peer_u,
           peer_v, lnf_g):
    nbatch, seq, d = x.shape
    depth = w_ada.shape[0]
    tables = _rope_tables(positions)
    for layer in range(depth):
        mod3 = _ada(c, w_ada[layer], b_ada[layer]).reshape(nbatch, ADA_CHUNKS, d)
        pa, proj, aux = _in_proj(x, mod3, ln1_g[layer], _permute_w_in(w_in[layer]), tables)
        o_a = _attn_a(pa)
        o_b = _dsa(proj, aux)
        x = _out_proj(o_a, o_b, w_out[layer].astype(BF16), x, mod3)
        keys = peer_sub_keys[layer].reshape(2 * PEER_HEADS, PEER_NKEYS, PEER_HALF)
        h2, st = _peer_q(x, mod3, ln2_g[layer], peer_wq[layer].astype(BF16), keys)
        stats = _peer_topk(st)
        peer_out = _peer_dense(h2.reshape(nbatch * seq, d), peer_u[layer].astype(BF16),
                               peer_v[layer].astype(BF16), st, stats)
        x = _final(x, peer_out.reshape(nbatch, seq, d), mod3, lnf_g, normalize=layer + 1 == depth)
    return x
```

```python
import functools
import math

import jax
import jax.numpy as jnp
from jax import lax
from jax.experimental import pallas as pl
from jax.experimental.pallas import tpu as pltpu

F32 = jnp.float32
BF16 = jnp.bfloat16

HEAD_DIM = 128
A_HEADS = 16
A_KV_HEADS = 4
A_GROUP = A_HEADS // A_KV_HEADS
DILATED_BRANCHES = ((128, 1), (512, 4), (2048, 16))
WIN_BLK = 128
B_HEADS = 16
IDX_HEADS = 16
IDX_DIM = 64
DSA_TOPK_MAX = 256
Q_BLK = 128
PEER_HEADS = 8
PEER_NKEYS = 128
PEER_HALF = 128
PEER_TOPK = 16
ROPE_THETA = 10000.0
NORM_EPS = 1e-6
NEG = -1e30
ADA_CHUNKS = 6
INT_MIN = -(2 ** 31)

LANES = 128
VMEM_LIMIT = 56 * 1024 * 1024

PROJ_TILE = 512
IN_COLS = ((A_HEADS + 2 * A_KV_HEADS + B_HEADS + 2) * HEAD_DIM + IDX_HEADS * IDX_DIM + IDX_DIM + IDX_HEADS)
PROJ_TILES = -(-IN_COLS // PROJ_TILE)
A_TILES = (A_HEADS + 2 * A_KV_HEADS) * HEAD_DIM // PROJ_TILE
KA_TILE = A_HEADS * HEAD_DIM // PROJ_TILE
VA_TILE = KA_TILE + 1
PA_HEADS = A_HEADS + 2 * A_KV_HEADS
QB_OFF = 0
KB_OFF = QB_OFF + B_HEADS * HEAD_DIM
VB_OFF = KB_OFF + HEAD_DIM
QI_OFF = VB_OFF + HEAD_DIM
KIKI_OFF = QI_OFF + IDX_HEADS * IDX_DIM
P_COLS = (PROJ_TILES - A_TILES) * PROJ_TILE
WI_LANE = IDX_DIM


def _params(semantics):
    return pltpu.CompilerParams(dimension_semantics=semantics, vmem_limit_bytes=VMEM_LIMIT)


def _tile(n, pref):
    return pref if n % pref == 0 else n


def _norm_mod(x, g, scale, shift):
    xf = x.astype(F32)
    y = xf * lax.rsqrt(jnp.mean(xf * xf, axis=-1, keepdims=True) + NORM_EPS)
    return (y * g) * (1.0 + scale) + shift


def _ada_kernel(ct_ref, w_ref, b_ref, o_ref, *, nbatch):
    @pl.when(pl.program_id(1) == 0)
    def _():
        o_ref[...] = jnp.broadcast_to(b_ref[...], o_ref.shape)

    ct = ct_ref[...]
    s = ct * jax.nn.sigmoid(ct)
    w = w_ref[...]
    rows = [jnp.sum(w * s[:, b:b + 1], axis=0, keepdims=True) for b in range(nbatch)]
    o_ref[...] += jnp.concatenate(rows, axis=0)


def _ada(c, w_ada, b_ada):
    nbatch, d = c.shape
    n = w_ada.shape[1]
    tk, tn = _tile(d, 512), _tile(n, 2048)
    return pl.pallas_call(
        functools.partial(_ada_kernel, nbatch=nbatch),
        grid=(n // tn, d // tk),
        in_specs=[pl.BlockSpec((tk, nbatch), lambda j, k: (k, 0)),
                  pl.BlockSpec((tk, tn), lambda j, k: (k, j)),
                  pl.BlockSpec((1, tn), lambda j, k: (0, j))],
        out_specs=pl.BlockSpec((nbatch, tn), lambda j, k: (0, j)),
        out_shape=jax.ShapeDtypeStruct((nbatch, n), F32),
        compiler_params=_params(("parallel", "arbitrary")),
        name="ada",
    )(c.T, w_ada, b_ada.reshape(1, n))


def _rope_kernel(pos_ref, ch_ref, sh_ref, ci_ref, sa_ref, sb_ref):
    pos = pos_ref[...].astype(F32)
    lane = lax.broadcasted_iota(jnp.int32, (1, LANES), 1)
    expo_h = -((2 * (lane & 63)).astype(F32)) / HEAD_DIM
    expo_i = -((2 * (lane & 31)).astype(F32)) / IDX_DIM
    inv = jnp.power(ROPE_THETA, jnp.where(lane < 64, expo_h, expo_i))
    ang = pos * inv
    c = jnp.cos(ang)
    s = jnp.sin(ang)
    ch_ref[...] = jnp.where(lane < 64, c, pltpu.roll(c, 64, 1))
    sh_ref[...] = jnp.where(lane < 64, -s, pltpu.roll(s, 64, 1))
    grp = lane >> 5
    c64, c96, c32 = pltpu.roll(c, 64, 1), pltpu.roll(c, 96, 1), pltpu.roll(c, 32, 1)
    s64, s96, s32 = pltpu.roll(s, 64, 1), pltpu.roll(s, 96, 1), pltpu.roll(s, 32, 1)
    ci_ref[...] = jnp.where(grp == 0, c64, jnp.where(grp == 1, c96, jnp.where(grp == 2, c, c32)))
    sa_ref[...] = jnp.where(grp == 0, -s64, jnp.where(grp == 2, -s, 0.0))
    sb_ref[...] = jnp.where(grp == 1, s96, jnp.where(grp == 3, s32, 0.0))


def _rope_tables(positions):
    nbatch, seq = positions.shape
    ts = _tile(seq, 1024)
    spec = pl.BlockSpec((None, ts, LANES), lambda b, i: (b, i, 0))
    shape = jax.ShapeDtypeStruct((nbatch, seq, LANES), F32)
    return pl.pallas_call(
        _rope_kernel,
        grid=(nbatch, seq // ts),
        in_specs=[pl.BlockSpec((None, ts, 1), lambda b, i: (b, i, 0))],
        out_specs=[spec] * 5,
        out_shape=[shape] * 5,
        compiler_params=_params(("parallel", "parallel")),
        name="rope",
    )(positions.reshape(nbatch, seq, 1))


def _rope_head(a, ch, sh):
    return a * ch + pltpu.roll(a, 64, 1) * sh


def _rope_idx(a, ci, sa, sb):
    return a * ci + pltpu.roll(a, 96, 1) * sa + pltpu.roll(a, 32, 1) * sb


def _in_proj_kernel(x_ref, mod_ref, g_ref, w_ref, ch_ref, sh_ref, ci_ref, sa_ref, sb_ref,
                    pa_ref, p_ref, aux_ref, h_scr, acc_scr):
    j = pl.program_id(2)

    @pl.when(j == 0)
    def _():
        h_scr[...] = _norm_mod(x_ref[...], g_ref[...], mod_ref[1:2, :], mod_ref[0:1, :]).astype(BF16)

    acc_scr[...] = jnp.dot(h_scr[...], w_ref[...], preferred_element_type=F32)
    nblk = PROJ_TILE // LANES
    qscale = HEAD_DIM ** -0.5 * math.log2(math.e)
    qb_tile = A_TILES + QB_OFF // PROJ_TILE
    mix_tile = A_TILES + KB_OFF // PROJ_TILE
    last_tile = PROJ_TILES - 1

    def blk(q):
        return acc_scr[:, q * LANES:(q + 1) * LANES]

    def put(q, val):
        p_ref[:, q * LANES:(q + 1) * LANES] = val.astype(BF16)

    def head(q):
        return _rope_head(blk(q), ch_ref[...], sh_ref[...])

    def idx(q):
        return _rope_idx(blk(q), ci_ref[...], sa_ref[...], sb_ref[...])

    @pl.when(j < KA_TILE)
    def _():
        for q in range(nblk):
            pa_ref[q] = head(q) * qscale

    @pl.when(j == KA_TILE)
    def _():
        for q in range(nblk):
            pa_ref[q] = head(q)

    @pl.when(j == VA_TILE)
    def _():
        for q in range(nblk):
            pa_ref[q] = blk(q)

    @pl.when((j >= qb_tile) & (j < mix_tile))
    def _():
        for q in range(nblk):
            put(q, head(q) * qscale)

    @pl.when(j == mix_tile)
    def _():
        put(0, head(0))
        put(1, blk(1))
        put(2, idx(2))
        put(3, idx(3))

    @pl.when((j > mix_tile) & (j < last_tile))
    def _():
        for q in range(nblk):
            put(q, idx(q))

    @pl.when(j == last_tile)
    def _():
        put(0, idx(0))
        put(1, idx(1))
        ki = idx(2)
        lane = lax.broadcasted_iota(jnp.int32, ki.shape, 1)
        put(2, jnp.where(lane < IDX_DIM, ki, pltpu.roll(ki, IDX_DIM, 1)))
        put(3, blk(3))
        aux_ref[...] = blk(2)


def _prep_w_in(w_in):
    assert w_in.shape[1] == IN_COLS
    return jnp.pad(w_in.astype(BF16), ((0, 0), (0, PROJ_TILES * PROJ_TILE - IN_COLS)))


def _in_proj(x, mod3, g, w_perm, tables):
    nbatch, seq, d = x.shape
    tm = _tile(seq, 512)
    nblk = PROJ_TILE // LANES
    tab_spec = pl.BlockSpec((None, tm, LANES), lambda b, i, j: (b, i, 0))
    return pl.pallas_call(
        _in_proj_kernel,
        grid=(nbatch, seq // tm, PROJ_TILES),
        in_specs=[pl.BlockSpec((None, tm, d), lambda b, i, j: (b, i, 0)),
                  pl.BlockSpec((None, ADA_CHUNKS, d), lambda b, i, j: (b, 0, 0)),
                  pl.BlockSpec((1, d), lambda b, i, j: (0, 0)),
                  pl.BlockSpec((d, PROJ_TILE), lambda b, i, j: (0, j))] + [tab_spec] * 5,
        out_specs=[pl.BlockSpec((None, nblk, tm, LANES), lambda b, i, j: (b, jnp.minimum(j, A_TILES - 1), i, 0)),
                   pl.BlockSpec((None, tm, PROJ_TILE), lambda b, i, j: (b, i, jnp.maximum(j - A_TILES, 0))),
                   pl.BlockSpec((None, tm, LANES), lambda b, i, j: (b, i, 0))],
        out_shape=[jax.ShapeDtypeStruct((nbatch, PA_HEADS, seq, LANES), F32),
                   jax.ShapeDtypeStruct((nbatch, seq, P_COLS), BF16),
                   jax.ShapeDtypeStruct((nbatch, seq, LANES), F32)],
        scratch_shapes=[pltpu.VMEM((tm, d), BF16), pltpu.VMEM((tm, PROJ_TILE), F32)],
        compiler_params=_params(("parallel", "parallel", "arbitrary")),
        name="in_proj",
    )(x, mod3, g.reshape(1, d), w_perm, *tables)


def _attn_a_kernel(q_ref, k_ref, v_ref, o_ref, acc_scr, lse_scr, bias_scr, *, dilations):
    half = pl.program_id(2)
    qrows = q_ref.shape[1]
    nblk = qrows // WIN_BLK

    rows = A_GROUP * WIN_BLK
    row = lax.broadcasted_iota(jnp.int32, (rows, 2 * WIN_BLK), 0) & (WIN_BLK - 1)
    col = lax.broadcasted_iota(jnp.int32, (rows, 2 * WIN_BLK), 1)
    band = (col >= row) & (col <= row + WIN_BLK)
    bias_scr[0] = jnp.where(band & (col >= WIN_BLK), 0.0, NEG)
    bias_scr[1] = jnp.where(band, 0.0, NEG)

    def rows_at(start, dil):
        return pl.ds(start, WIN_BLK) if dil == 1 else pl.ds(start, WIN_BLK, stride=dil)

    for idx, dil in enumerate(dilations):
        per_res = nblk // dil
        first, last = idx == 0, idx == len(dilations) - 1

        def body(blk, carry, dil=dil, per_res=per_res, first=first, last=last):
            r = blk // per_res
            n_loc = blk - r * per_res
            n = half * per_res + n_loc
            q0 = r + dil * WIN_BLK * n_loc
            k0 = r + dil * WIN_BLK * n
            kp = r + dil * WIN_BLK * jnp.maximum(n - 1, 0)
            q4 = jnp.concatenate([q_ref[g, rows_at(q0, dil), :] for g in range(A_GROUP)], axis=0).astype(BF16)
            kc = jnp.concatenate([k_ref[rows_at(kp, dil), :], k_ref[rows_at(k0, dil), :]], axis=0).astype(BF16)
            vc = jnp.concatenate([v_ref[rows_at(kp, dil), :], v_ref[rows_at(k0, dil), :]], axis=0).astype(BF16)
            s = lax.dot_general(q4, kc, (((1,), (1,)), ((), ())), preferred_element_type=F32)
            s = s + bias_scr[jnp.minimum(n, 1)]
            m = jnp.max(s, axis=-1, keepdims=True)
            p = jnp.exp2(s - m)
            l = jnp.sum(p, axis=-1, keepdims=True)
            o = jnp.dot(p.astype(BF16), vc, preferred_element_type=F32) / l
            lse = m + jnp.log2(l)
            for g in range(A_GROUP):
                og = o[g * WIN_BLK:(g + 1) * WIN_BLK]
                lg = jnp.broadcast_to(lse[g * WIN_BLK:(g + 1) * WIN_BLK], (WIN_BLK, HEAD_DIM))
                if not first:
                    opg = acc_scr[g, rows_at(q0, dil), :]
                    lpg = lse_scr[g, rows_at(q0, dil), :]
                    w_prev = 1.0 / (1.0 + jnp.exp2(lg - lpg))
                    og = og + (opg - og) * w_prev
                    if not last:
                        lg = jnp.maximum(lpg, lg) + jnp.log2(1.0 + jnp.exp2(-jnp.abs(lg - lpg)))
                if last:
                    o_ref[pl.ds(pl.multiple_of(q0, WIN_BLK), WIN_BLK), g * HEAD_DIM:(g + 1) * HEAD_DIM] = (
                        og.astype(o_ref.dtype))
                else:
                    acc_scr[g, rows_at(q0, dil), :] = og
                    lse_scr[g, rows_at(q0, dil), :] = lg
            return carry

        lax.fori_loop(0, nblk, body, 0, unroll=2)


def _attn_a(pa):
    nbatch, _, seq, _ = pa.shape
    dilations = tuple(sorted((dil for _, dil in DILATED_BRANCHES), reverse=True))
    assert dilations[-1] == 1 and all(w // dil == WIN_BLK for w, dil in DILATED_BRANCHES)
    nhalf = 2 if seq % (2 * WIN_BLK * dilations[0]) == 0 else 1
    qrows = seq // nhalf
    assert qrows % (WIN_BLK * dilations[0]) == 0
    return pl.pallas_call(
        functools.partial(_attn_a_kernel, dilations=dilations),
        grid=(nbatch, A_KV_HEADS, nhalf),
        in_specs=[pl.BlockSpec((None, A_GROUP, qrows, HEAD_DIM), lambda b, h, t: (b, h, t, 0)),
                  pl.BlockSpec((None, None, seq, HEAD_DIM), lambda b, h, t: (b, A_HEADS + h, 0, 0)),
                  pl.BlockSpec((None, None, seq, HEAD_DIM),
                               lambda b, h, t: (b, A_HEADS + A_KV_HEADS + h, 0, 0))],
        out_specs=pl.BlockSpec((None, qrows, A_GROUP * HEAD_DIM), lambda b, h, t: (b, t, h)),
        out_shape=jax.ShapeDtypeStruct((nbatch, seq, A_HEADS * HEAD_DIM), BF16),
        scratch_shapes=[pltpu.VMEM((A_GROUP, qrows, HEAD_DIM), F32),
                        pltpu.VMEM((A_GROUP, qrows, HEAD_DIM), F32),
                        pltpu.VMEM((2, A_GROUP * WIN_BLK, 2 * WIN_BLK), F32)],
        compiler_params=_params(("parallel", "parallel", "arbitrary")),
        name="attn_a",
    )(pa, pa, pa)


KEY_CHUNK = 256


def _ordered_to_float(key):
    return pltpu.bitcast(key ^ ((key >> 31) & jnp.int32(0x7FFFFFFF)), F32)


LOWEST_FINITE_KEY = INT_MIN + 0x00800000


def _dsa_kernel(p_ref, wi_ref, k_ref, kk_ref, v_ref, o_ref,
                sc_scr, wib_scr, q_scr, p_scr, m_scr, l_scr, acc_scr, *, topk):
    i = pl.program_id(1)
    nchunks = (i + 2) // 2
    halves = KEY_CHUNK // LANES

    wi = wi_ref[...] * (IDX_HEADS ** -0.5 * IDX_DIM ** -0.5)
    lane = lax.broadcasted_iota(jnp.int32, (Q_BLK, LANES), 1)
    for h in range(IDX_HEADS):
        blk = p_ref[:, QI_OFF + (h // 2) * LANES:QI_OFF + (h // 2 + 1) * LANES]
        keep = (lane < IDX_DIM) if h % 2 == 0 else (lane >= IDX_DIM)
        q_scr[h * Q_BLK:(h + 1) * Q_BLK, :] = jnp.where(keep, blk, jnp.zeros_like(blk))
        wib_scr[h] = jnp.broadcast_to(wi[:, WI_LANE + h:WI_LANE + h + 1], (Q_BLK, LANES))
    qpos = i * Q_BLK + lax.broadcasted_iota(jnp.int32, (Q_BLK, KEY_CHUNK), 0)
    kiota = lax.broadcasted_iota(jnp.int32, (Q_BLK, KEY_CHUNK), 1)

    def score_chunk(c, carry):
        k0 = pl.multiple_of(c * KEY_CHUNK, KEY_CHUNK)
        z = lax.dot_general(q_scr[...], kk_ref[pl.ds(k0, KEY_CHUNK), :], (((1,), (1,)), ((), ())),
                            preferred_element_type=F32)
        sc = jnp.zeros((Q_BLK, KEY_CHUNK), F32)
        for h in range(IDX_HEADS):
            w = wib_scr[h]
            sc = sc + jnp.concatenate([w] * halves, axis=1) * jnp.maximum(z[h * Q_BLK:(h + 1) * Q_BLK], 0.0)
        sc_scr[c] = jnp.where(k0 + kiota <= qpos, sc, -jnp.inf)
        return carry

    lax.fori_loop(0, nchunks, score_chunk, 0)

    def bit_body(t, ans):
        trial = ans | jnp.left_shift(jnp.int32(1), 31 - t)
        thr = _ordered_to_float(trial ^ jnp.int32(INT_MIN))

        def count_chunk(c, acc):
            sc = sc_scr[c]
            for q in range(halves):
                acc = acc + jnp.where(sc[:, q * LANES:(q + 1) * LANES] >= thr, 1.0, 0.0)
            return acc

        acc = lax.fori_loop(0, nchunks, count_chunk, jnp.zeros((Q_BLK, LANES), F32))
        cnt = jnp.sum(acc, axis=1, keepdims=True)
        return jnp.where(cnt >= float(topk), trial, ans)

    ans = lax.fori_loop(0, 32, bit_body, jnp.zeros((Q_BLK, LANES), jnp.int32))
    tau = _ordered_to_float(jnp.maximum(ans ^ jnp.int32(INT_MIN), jnp.int32(LOWEST_FINITE_KEY)))
    tau2 = jnp.concatenate([tau] * halves, axis=1)

    for h in range(B_HEADS):
        q_scr[h * Q_BLK:(h + 1) * Q_BLK, :] = p_ref[:, QB_OFF + h * HEAD_DIM:QB_OFF + (h + 1) * HEAD_DIM]
    m_scr[...] = jnp.full(m_scr.shape, NEG, F32)
    l_scr[...] = jnp.zeros(l_scr.shape, F32)
    acc_scr[...] = jnp.zeros(acc_scr.shape, F32)

    def att_chunk(c, carry):
        k0 = pl.multiple_of(c * KEY_CHUNK, KEY_CHUNK)
        s = lax.dot_general(q_scr[...], k_ref[pl.ds(k0, KEY_CHUNK), :], (((1,), (1,)), ((), ())),
                            preferred_element_type=F32)
        bias = jnp.where(sc_scr[c] >= tau2, 0.0, NEG)
        for h in range(B_HEADS):
            hr = slice(h * Q_BLK, (h + 1) * Q_BLK)
            sh = s[hr] + bias
            m_old = m_scr[hr, :]
            m_new = jnp.maximum(m_old, jnp.max(sh, axis=-1, keepdims=True))
            alpha = jnp.exp2(m_old - m_new)
            p = jnp.exp2(sh - jnp.concatenate([m_new] * halves, axis=1))
            psum = p[:, :LANES]
            for q in range(1, halves):
                psum = psum + p[:, q * LANES:(q + 1) * LANES]
            l_scr[hr, :] = alpha * l_scr[hr, :] + psum
            m_scr[hr, :] = m_new
            p_scr[hr, :] = p.astype(BF16)
            acc_scr[hr, :] = alpha * acc_scr[hr, :]
        acc_scr[...] += jnp.dot(p_scr[...], v_ref[pl.ds(k0, KEY_CHUNK), :], preferred_element_type=F32)
        return carry

    lax.fori_loop(0, nchunks, att_chunk, 0)
    for h in range(B_HEADS):
        hr = slice(h * Q_BLK, (h + 1) * Q_BLK)
        l = jnp.sum(l_scr[hr, :], axis=-1, keepdims=True)
        o_ref[:, h * HEAD_DIM:(h + 1) * HEAD_DIM] = (acc_scr[hr, :] / l).astype(o_ref.dtype)


def _dsa(proj, aux):
    nbatch, seq, _ = proj.shape
    assert seq % KEY_CHUNK == 0
    topk = min(DSA_TOPK_MAX, seq // 4)
    qw = B_HEADS * HEAD_DIM
    full = lambda off: pl.BlockSpec((None, seq, LANES), lambda b, i: (b, 0, off // LANES))
    return pl.pallas_call(
        functools.partial(_dsa_kernel, topk=topk),
        grid=(nbatch, seq // Q_BLK),
        in_specs=[pl.BlockSpec((None, Q_BLK, P_COLS), lambda b, i: (b, i, 0)),
                  pl.BlockSpec((None, Q_BLK, LANES), lambda b, i: (b, i, 0)),
                  full(KB_OFF), full(KIKI_OFF), full(VB_OFF)],
        out_specs=pl.BlockSpec((None, Q_BLK, qw), lambda b, i: (b, i, 0)),
        out_shape=jax.ShapeDtypeStruct((nbatch, seq, qw), BF16),
        scratch_shapes=[pltpu.VMEM((seq // KEY_CHUNK, Q_BLK, KEY_CHUNK), F32),
                        pltpu.VMEM((IDX_HEADS, Q_BLK, LANES), F32),
                        pltpu.VMEM((B_HEADS * Q_BLK, HEAD_DIM), BF16),
                        pltpu.VMEM((B_HEADS * Q_BLK, KEY_CHUNK), BF16),
                        pltpu.VMEM((B_HEADS * Q_BLK, LANES), F32),
                        pltpu.VMEM((B_HEADS * Q_BLK, LANES), F32),
                        pltpu.VMEM((B_HEADS * Q_BLK, HEAD_DIM), F32)],
        compiler_params=_params(("parallel", "arbitrary")),
        name="dsa",
    )(proj, aux, proj, proj, proj)


def _out_proj_kernel(oa_ref, ob_ref, w_ref, x_ref, mod_ref, o_ref):
    ka = oa_ref.shape[-1]
    y = jnp.dot(oa_ref[...], w_ref[:ka, :], preferred_element_type=F32)
    y = y + jnp.dot(ob_ref[...], w_ref[ka:, :], preferred_element_type=F32)
    o_ref[...] = x_ref[...] + mod_ref[2:3, :] * y


def _out_proj(o_a, o_b, w_out, x, mod3):
    nbatch, seq, d = x.shape
    ka, kb = o_a.shape[-1], o_b.shape[-1]
    tm, tn = _tile(seq, 512), _tile(d, 512)
    return pl.pallas_call(
        _out_proj_kernel,
        grid=(nbatch, seq // tm, d // tn),
        in_specs=[pl.BlockSpec((None, tm, ka), lambda b, i, j: (b, i, 0)),
                  pl.BlockSpec((None, tm, kb), lambda b, i, j: (b, i, 0)),
                  pl.BlockSpec((ka + kb, tn), lambda b, i, j: (0, j)),
                  pl.BlockSpec((None, tm, tn), lambda b, i, j: (b, i, j)),
                  pl.BlockSpec((None, ADA_CHUNKS, tn), lambda b, i, j: (b, 0, j))],
        out_specs=pl.BlockSpec((None, tm, tn), lambda b, i, j: (b, i, j)),
        out_shape=jax.ShapeDtypeStruct((nbatch, seq, d), F32),
        compiler_params=_params(("parallel", "parallel", "parallel")),
        name="out_proj",
    )(o_a, o_b, w_out, x, mod3)


def _peer_q_kernel(x_ref, mod_ref, g_ref, w_ref, keys_ref, h_ref, st_ref, h_scr):
    j = pl.program_id(2)

    @pl.when(j == 0)
    def _():
        h = _norm_mod(x_ref[...], g_ref[...], mod_ref[4:5, :], mod_ref[3:4, :]).astype(BF16)
        h_scr[...] = h
        h_ref[...] = h

    q = jnp.dot(h_scr[...], w_ref[...], preferred_element_type=F32)
    for blk in range(q.shape[1] // PEER_HALF):
        qh = q[:, blk * PEER_HALF:(blk + 1) * PEER_HALF]
        st_ref[blk] = lax.dot_general(keys_ref[blk], qh, (((1,), (1,)), ((), ())),
                                      preferred_element_type=F32, precision=lax.Precision.HIGHEST)


def _peer_q(x1, mod3, g, w_q, keys):
    nbatch, seq, d = x1.shape
    nq = w_q.shape[1]
    tm, tn = _tile(seq, 512), 4 * PEER_HALF
    nhp = nq // PEER_HALF
    nt = seq // tm
    return pl.pallas_call(
        _peer_q_kernel,
        grid=(nbatch, nt, nq // tn),
        in_specs=[pl.BlockSpec((None, tm, d), lambda b, i, j: (b, i, 0)),
                  pl.BlockSpec((None, ADA_CHUNKS, d), lambda b, i, j: (b, 0, 0)),
                  pl.BlockSpec((1, d), lambda b, i, j: (0, 0)),
                  pl.BlockSpec((d, tn), lambda b, i, j: (0, j)),
                  pl.BlockSpec((tn // PEER_HALF, PEER_NKEYS, PEER_HALF), lambda b, i, j: (j, 0, 0))],
        out_specs=[pl.BlockSpec((None, tm, d), lambda b, i, j: (b, i, 0)),
                   pl.BlockSpec((tn // PEER_HALF, PEER_NKEYS, tm), lambda b, i, j: (j, 0, b * nt + i))],
        out_shape=[jax.ShapeDtypeStruct((nbatch, seq, d), BF16),
                   jax.ShapeDtypeStruct((nhp, PEER_NKEYS, nbatch * seq), F32)],
        scratch_shapes=[pltpu.VMEM((tm, d), BF16)],
        compiler_params=_params(("parallel", "parallel", "arbitrary")),
        name="peer_q",
    )(x1, mod3, g.reshape(1, d), w_q, keys)


def _top_rows(x, count, one_at_a_time):
    rows = []
    ridx = lax.broadcasted_iota(jnp.int32, x.shape, 0)
    for _ in range(count):
        m = jnp.max(x, axis=0, keepdims=True)
        rows.append(m)
        hit = x == m
        if one_at_a_time:
            first = jnp.min(jnp.where(hit, ridx, x.shape[0]), axis=0, keepdims=True)
            hit = ridx == first
        x = jnp.where(hit, -jnp.inf, x)
    removed = jnp.sum(jnp.where(x == -jnp.inf, 1.0, 0.0), axis=0, keepdims=True)
    return rows, removed


def _peer_group_stats(x0, x1, one_at_a_time):
    half = PEER_TOPK // 2
    v0, r0 = _top_rows(x0, PEER_TOPK, one_at_a_time)
    v1, r1 = _top_rows(x1, PEER_TOPK, one_at_a_time)
    v0_all = jnp.concatenate(v0, axis=0)
    v1_all = jnp.concatenate(v1, axis=0)
    cand = [v0[0] + v1_all]
    cand += [v0[a] + v1_all[:half] for a in range(1, half)]
    cand += [v0_all[half:] + v1[0]]
    top, rc = _top_rows(jnp.concatenate(cand, axis=0), PEER_TOPK, one_at_a_time)
    z = jnp.zeros_like(top[0])
    for t in top:
        z = z + jnp.exp(t - top[0])
    stats = jnp.concatenate([top[-1], v0[0], v1[0], 1.0 / z] + [jnp.zeros_like(z)] * 4, axis=0)
    repeated = jnp.max(jnp.maximum(jnp.maximum(r0, r1), rc)) > float(PEER_TOPK)
    return stats, repeated


def _peer_topk_kernel(st_ref, stats_ref):
    ngroups = st_ref.shape[-1] // LANES

    def groups(h, one_at_a_time):
        out = []
        for tg in range(ngroups):
            cols = slice(tg * LANES, (tg + 1) * LANES)
            out.append(_peer_group_stats(st_ref[2 * h, :, cols], st_ref[2 * h + 1, :, cols], one_at_a_time))
        return out

    def per_head(h, carry):
        fast = groups(h, False)
        repeated = fast[0][1]
        for tg in range(ngroups):
            stats_ref[h, :, tg * LANES:(tg + 1) * LANES] = fast[tg][0]
            repeated = repeated | fast[tg][1]

        @pl.when(repeated)
        def _():
            for tg, (stats, _) in enumerate(groups(h, True)):
                stats_ref[h, :, tg * LANES:(tg + 1) * LANES] = stats
        return carry

    lax.fori_loop(0, PEER_HEADS, per_head, 0)


def _peer_topk(st):
    nhp, nkeys, ntok = st.shape
    tmk = _tile(ntok, 256)
    return pl.pallas_call(
        _peer_topk_kernel,
        grid=(ntok // tmk,),
        in_specs=[pl.BlockSpec((nhp, nkeys, tmk), lambda i: (0, 0, i))],
        out_specs=pl.BlockSpec((PEER_HEADS, 8, tmk), lambda i: (0, 0, i)),
        out_shape=jax.ShapeDtypeStruct((PEER_HEADS, 8, ntok), F32),
        compiler_params=_params(("parallel",)),
        name="peer_topk",
    )(st)


def _peer_dense_kernel(x_ref, u_ref, v_ref, st_ref, stats_ref, o_ref, e1_scr, at_scr, act_scr):
    e = pl.program_id(1)
    te, tm = u_ref.shape[0], x_ref.shape[0]
    ni = te // PEER_NKEYS

    @pl.when(e == 0)
    def _():
        o_ref[...] = jnp.zeros_like(o_ref)
        for h in range(PEER_HEADS):
            e1_scr[h] = jnp.exp(st_ref[2 * h + 1] - stats_ref[h, 2:3, :]) * stats_ref[h, 3:4, :]

    at_scr[...] = lax.dot_general(u_ref[...], x_ref[...], (((1,), (1,)), ((), ())),
                                  preferred_element_type=F32)
    for il in range(ni):
        i = e * ni + il
        for tc in range(tm // LANES):
            ts = slice(tc * LANES, (tc + 1) * LANES)
            g = jnp.zeros((PEER_NKEYS, LANES), F32)
            for h in range(PEER_HEADS):
                s0 = st_ref[2 * h, pl.ds(i, 1), :][:, ts]
                e0 = jnp.exp(s0 - stats_ref[h, 1:2, ts])
                sel = (s0 + st_ref[2 * h + 1, :, ts]) >= stats_ref[h, 0:1, ts]
                g = g + jnp.where(sel, e0 * e1_scr[h, :, ts], 0.0)
            a = at_scr[il * PEER_NKEYS:(il + 1) * PEER_NKEYS, ts]
            act = 0.5 * a * (1.0 + lax.erf(a * (2.0 ** -0.5))) * g
            act_scr[ts, il * PEER_NKEYS:(il + 1) * PEER_NKEYS] = act.T.astype(BF16)
    o_ref[...] += jnp.dot(act_scr[...], v_ref[...], preferred_element_type=F32)


def _peer_dense(h2, u_tab, v_tab, st, stats):
    ntok, d = h2.shape
    nexp = u_tab.shape[0]
    tm, te = _tile(ntok, 512), 4 * PEER_NKEYS
    nhp = st.shape[0]
    return pl.pallas_call(
        _peer_dense_kernel,
        grid=(ntok // tm, nexp // te),
        in_specs=[pl.BlockSpec((tm, d), lambda i, e: (i, 0)),
                  pl.BlockSpec((te, d), lambda i, e: (e, 0)),
                  pl.BlockSpec((te, d), lambda i, e: (e, 0)),
                  pl.BlockSpec((nhp, PEER_NKEYS, tm), lambda i, e: (0, 0, i)),
                  pl.BlockSpec((PEER_HEADS, 8, tm), lambda i, e: (0, 0, i))],
        out_specs=pl.BlockSpec((tm, d), lambda i, e: (i, 0)),
        out_shape=jax.ShapeDtypeStruct((ntok, d), F32),
        scratch_shapes=[pltpu.VMEM((PEER_HEADS, PEER_NKEYS, tm), F32),
                        pltpu.VMEM((te, tm), F32), pltpu.VMEM((tm, te), BF16)],
        compiler_params=_params(("parallel", "arbitrary")),
        name="peer_dense",
    )(h2, u_tab, v_tab, st, stats)


def _final_kernel(x_ref, p_ref, mod_ref, g_ref, o_ref, *, normalize):
    y = x_ref[...] + mod_ref[5:6, :] * p_ref[...]
    if normalize:
        y = (y * lax.rsqrt(jnp.mean(y * y, axis=-1, keepdims=True) + NORM_EPS)) * g_ref[...]
    o_ref[...] = y


def _final(x1, peer_out, mod3, g, normalize):
    nbatch, seq, d = x1.shape
    tm = _tile(seq, 256)
    spec = pl.BlockSpec((None, tm, d), lambda b, i: (b, i, 0))
    return pl.pallas_call(
        functools.partial(_final_kernel, normalize=normalize),
        grid=(nbatch, seq // tm),
        in_specs=[spec, spec,
                  pl.BlockSpec((None, ADA_CHUNKS, d), lambda b, i: (b, 0, 0)),
                  pl.BlockSpec((1, d), lambda b, i: (0, 0))],
        out_specs=spec,
        out_shape=jax.ShapeDtypeStruct((nbatch, seq, d), F32),
        compiler_params=_params(("parallel", "parallel")),
        name="final",
    )(x1, peer_out, mod3, g.reshape(1, d))


def kernel(x, c, positions, ln1_g, ln2_g, w_ada, b_ada, w_in, w_out, peer_wq, peer_sub_keys, peer_u,
           peer_v, lnf_g):
    nbatch, seq, d = x.shape
    depth = w_ada.shape[0]
    tables = _rope_tables(positions)
    for layer in range(depth):
        mod3 = _ada(c, w_ada[layer], b_ada[layer]).reshape(nbatch, ADA_CHUNKS, d)
        pa, proj, aux = _in_proj(x, mod3, ln1_g[layer], _prep_w_in(w_in[layer]), tables)
        o_a = _attn_a(pa)
        o_b = _dsa(proj, aux)
        x = _out_proj(o_a, o_b, w_out[layer].astype(BF16), x, mod3)
        keys = peer_sub_keys[layer].reshape(2 * PEER_HEADS, PEER_NKEYS, PEER_HALF)
        h2, st = _peer_q(x, mod3, ln2_g[layer], peer_wq[layer].astype(BF16), keys)
        stats = _peer_topk(st)
        peer_out = _peer_dense(h2.reshape(nbatch * seq, d), peer_u[layer].astype(BF16),
                               peer_v[layer].astype(BF16), st, stats)
        x = _final(x, peer_out.reshape(nbatch, seq, d), mod3, lnf_g, normalize=layer + 1 == depth)
    return x
```

```python
import functools
import math

import jax
import jax.numpy as jnp
from jax import lax
from jax.experimental import pallas as pl
from jax.experimental.pallas import tpu as pltpu

F32 = jnp.float32
BF16 = jnp.bfloat16

HEAD_DIM = 128
A_HEADS = 16
A_KV_HEADS = 4
A_GROUP = A_HEADS // A_KV_HEADS
DILATED_BRANCHES = ((128, 1), (512, 4), (2048, 16))
WIN_BLK = 128
B_HEADS = 16
IDX_HEADS = 16
IDX_DIM = 64
DSA_TOPK_MAX = 256
Q_BLK = 128
PEER_HEADS = 8
PEER_NKEYS = 128
PEER_HALF = 128
PEER_TOPK = 16
ROPE_THETA = 10000.0
NORM_EPS = 1e-6
NEG = -1e30
ADA_CHUNKS = 6
INT_MIN = -(2 ** 31)

LANES = 128
VMEM_LIMIT = 56 * 1024 * 1024

PROJ_TILE = 512
IN_COLS = ((A_HEADS + 2 * A_KV_HEADS + B_HEADS + 2) * HEAD_DIM + IDX_HEADS * IDX_DIM + IDX_DIM + IDX_HEADS)
PROJ_TILES = -(-IN_COLS // PROJ_TILE)
A_TILES = (A_HEADS + 2 * A_KV_HEADS) * HEAD_DIM // PROJ_TILE
KA_TILE = A_HEADS * HEAD_DIM // PROJ_TILE
VA_TILE = KA_TILE + 1
PA_HEADS = A_HEADS + 2 * A_KV_HEADS
QB_OFF = 0
KB_OFF = QB_OFF + B_HEADS * HEAD_DIM
VB_OFF = KB_OFF + HEAD_DIM
QI_OFF = VB_OFF + HEAD_DIM
KIKI_OFF = QI_OFF + IDX_HEADS * IDX_DIM
P_COLS = (PROJ_TILES - A_TILES) * PROJ_TILE
WI_LANE = IDX_DIM


def _params(semantics):
    return pltpu.CompilerParams(dimension_semantics=semantics, vmem_limit_bytes=VMEM_LIMIT)


def _tile(n, pref):
    return pref if n % pref == 0 else n


def _norm_mod(x, g, scale, shift):
    xf = x.astype(F32)
    y = xf * lax.rsqrt(jnp.mean(xf * xf, axis=-1, keepdims=True) + NORM_EPS)
    return (y * g) * (1.0 + scale) + shift


def _ada_kernel(ct_ref, w_ref, b_ref, o_ref, *, nbatch):
    @pl.when(pl.program_id(1) == 0)
    def _():
        o_ref[...] = jnp.broadcast_to(b_ref[...], o_ref.shape)

    ct = ct_ref[...]
    s = ct * jax.nn.sigmoid(ct)
    w = w_ref[...]
    rows = [jnp.sum(w * s[:, b:b + 1], axis=0, keepdims=True) for b in range(nbatch)]
    o_ref[...] += jnp.concatenate(rows, axis=0)


def _ada(c, w_ada, b_ada):
    nbatch, d = c.shape
    n = w_ada.shape[1]
    tk, tn = _tile(d, 1024), _tile(n, 2048)
    return pl.pallas_call(
        functools.partial(_ada_kernel, nbatch=nbatch),
        grid=(n // tn, d // tk),
        in_specs=[pl.BlockSpec((tk, nbatch), lambda j, k: (k, 0)),
                  pl.BlockSpec((tk, tn), lambda j, k: (k, j)),
                  pl.BlockSpec((1, tn), lambda j, k: (0, j))],
        out_specs=pl.BlockSpec((nbatch, tn), lambda j, k: (0, j)),
        out_shape=jax.ShapeDtypeStruct((nbatch, n), F32),
        compiler_params=_params(("parallel", "arbitrary")),
        name="ada",
    )(c.T, w_ada, b_ada.reshape(1, n))


def _rope_kernel(pos_ref, ch_ref, sh_ref, ci_ref, sa_ref, sb_ref):
    pos = pos_ref[...].astype(F32)
    lane = lax.broadcasted_iota(jnp.int32, (1, LANES), 1)
    expo_h = -((2 * (lane & 63)).astype(F32)) / HEAD_DIM
    expo_i = -((2 * (lane & 31)).astype(F32)) / IDX_DIM
    inv = jnp.power(ROPE_THETA, jnp.where(lane < 64, expo_h, expo_i))
    ang = pos * inv
    c = jnp.cos(ang)
    s = jnp.sin(ang)
    ch_ref[...] = jnp.where(lane < 64, c, pltpu.roll(c, 64, 1))
    sh_ref[...] = jnp.where(lane < 64, -s, pltpu.roll(s, 64, 1))
    grp = lane >> 5
    c64, c96, c32 = pltpu.roll(c, 64, 1), pltpu.roll(c, 96, 1), pltpu.roll(c, 32, 1)
    s64, s96, s32 = pltpu.roll(s, 64, 1), pltpu.roll(s, 96, 1), pltpu.roll(s, 32, 1)
    ci_ref[...] = jnp.where(grp == 0, c64, jnp.where(grp == 1, c96, jnp.where(grp == 2, c, c32)))
    sa_ref[...] = jnp.where(grp == 0, -s64, jnp.where(grp == 2, -s, 0.0))
    sb_ref[...] = jnp.where(grp == 1, s96, jnp.where(grp == 3, s32, 0.0))


def _rope_tables(positions):
    nbatch, seq = positions.shape
    ts = _tile(seq, 1024)
    spec = pl.BlockSpec((None, ts, LANES), lambda b, i: (b, i, 0))
    shape = jax.ShapeDtypeStruct((nbatch, seq, LANES), F32)
    return pl.pallas_call(
        _rope_kernel,
        grid=(nbatch, seq // ts),
        in_specs=[pl.BlockSpec((None, ts, 1), lambda b, i: (b, i, 0))],
        out_specs=[spec] * 5,
        out_shape=[shape] * 5,
        compiler_params=_params(("parallel", "parallel")),
        name="rope",
    )(positions.reshape(nbatch, seq, 1))


def _rope_head(a, ch, sh):
    return a * ch + pltpu.roll(a, 64, 1) * sh


def _rope_idx(a, ci, sa, sb):
    return a * ci + pltpu.roll(a, 96, 1) * sa + pltpu.roll(a, 32, 1) * sb


def _in_proj_kernel(x_ref, mod_ref, g_ref, w_ref, ch_ref, sh_ref, ci_ref, sa_ref, sb_ref,
                    pa_ref, p_ref, aux_ref, h_scr, acc_scr):
    j = pl.program_id(2)

    @pl.when(j == 0)
    def _():
        h_scr[...] = _norm_mod(x_ref[...], g_ref[...], mod_ref[1:2, :], mod_ref[0:1, :]).astype(BF16)

    acc_scr[...] = jnp.dot(h_scr[...], w_ref[...], preferred_element_type=F32)
    nblk = PROJ_TILE // LANES
    qscale = HEAD_DIM ** -0.5 * math.log2(math.e)
    qb_tile = A_TILES + QB_OFF // PROJ_TILE
    mix_tile = A_TILES + KB_OFF // PROJ_TILE
    last_tile = PROJ_TILES - 1

    def blk(q):
        return acc_scr[:, q * LANES:(q + 1) * LANES]

    def put(q, val):
        p_ref[:, q * LANES:(q + 1) * LANES] = val.astype(BF16)

    def head(q):
        return _rope_head(blk(q), ch_ref[...], sh_ref[...])

    def idx(q):
        return _rope_idx(blk(q), ci_ref[...], sa_ref[...], sb_ref[...])

    @pl.when(j < KA_TILE)
    def _():
        for q in range(nblk):
            pa_ref[q] = head(q) * qscale

    @pl.when(j == KA_TILE)
    def _():
        for q in range(nblk):
            pa_ref[q] = head(q)

    @pl.when(j == VA_TILE)
    def _():
        for q in range(nblk):
            pa_ref[q] = blk(q)

    @pl.when((j >= qb_tile) & (j < mix_tile))
    def _():
        for q in range(nblk):
            put(q, head(q) * qscale)

    @pl.when(j == mix_tile)
    def _():
        put(0, head(0))
        put(1, blk(1))
        put(2, idx(2))
        put(3, idx(3))

    @pl.when((j > mix_tile) & (j < last_tile))
    def _():
        for q in range(nblk):
            put(q, idx(q))

    @pl.when(j == last_tile)
    def _():
        put(0, idx(0))
        put(1, idx(1))
        ki = idx(2)
        lane = lax.broadcasted_iota(jnp.int32, ki.shape, 1)
        put(2, jnp.where(lane < IDX_DIM, ki, pltpu.roll(ki, IDX_DIM, 1)))
        put(3, blk(3))
        aux_ref[...] = blk(2)


def _prep_w_in(w_in):
    assert w_in.shape[1] == IN_COLS
    return jnp.pad(w_in.astype(BF16), ((0, 0), (0, PROJ_TILES * PROJ_TILE - IN_COLS)))


def _in_proj(x, mod3, g, w_perm, tables):
    nbatch, seq, d = x.shape
    tm = _tile(seq, 512)
    nblk = PROJ_TILE // LANES
    tab_spec = pl.BlockSpec((None, tm, LANES), lambda b, i, j: (b, i, 0))
    return pl.pallas_call(
        _in_proj_kernel,
        grid=(nbatch, seq // tm, PROJ_TILES),
        in_specs=[pl.BlockSpec((None, tm, d), lambda b, i, j: (b, i, 0)),
                  pl.BlockSpec((None, ADA_CHUNKS, d), lambda b, i, j: (b, 0, 0)),
                  pl.BlockSpec((1, d), lambda b, i, j: (0, 0)),
                  pl.BlockSpec((d, PROJ_TILE), lambda b, i, j: (0, j))] + [tab_spec] * 5,
        out_specs=[pl.BlockSpec((None, nblk, tm, LANES), lambda b, i, j: (b, jnp.minimum(j, A_TILES - 1), i, 0)),
                   pl.BlockSpec((None, tm, PROJ_TILE), lambda b, i, j: (b, i, jnp.maximum(j - A_TILES, 0))),
                   pl.BlockSpec((None, tm, LANES), lambda b, i, j: (b, i, 0))],
        out_shape=[jax.ShapeDtypeStruct((nbatch, PA_HEADS, seq, LANES), F32),
                   jax.ShapeDtypeStruct((nbatch, seq, P_COLS), BF16),
                   jax.ShapeDtypeStruct((nbatch, seq, LANES), F32)],
        scratch_shapes=[pltpu.VMEM((tm, d), BF16), pltpu.VMEM((tm, PROJ_TILE), F32)],
        compiler_params=_params(("parallel", "parallel", "arbitrary")),
        name="in_proj",
    )(x, mod3, g.reshape(1, d), w_perm, *tables)


def _attn_a_kernel(q_ref, k_ref, v_ref, o_ref, acc_scr, lse_scr, bias_scr, *, dilations):
    half = pl.program_id(2)
    qrows = q_ref.shape[1]
    nblk = qrows // WIN_BLK

    rows = A_GROUP * WIN_BLK
    row = lax.broadcasted_iota(jnp.int32, (rows, 2 * WIN_BLK), 0) & (WIN_BLK - 1)
    col = lax.broadcasted_iota(jnp.int32, (rows, 2 * WIN_BLK), 1)
    band = (col >= row) & (col <= row + WIN_BLK)
    bias_scr[0] = jnp.where(band & (col >= WIN_BLK), 0.0, NEG)
    bias_scr[1] = jnp.where(band, 0.0, NEG)

    def rows_at(start, dil):
        return pl.ds(start, WIN_BLK) if dil == 1 else pl.ds(start, WIN_BLK, stride=dil)

    for idx, dil in enumerate(dilations):
        per_res = nblk // dil
        first, last = idx == 0, idx == len(dilations) - 1

        def body(blk, carry, dil=dil, per_res=per_res, first=first, last=last):
            r = blk // per_res
            n_loc = blk - r * per_res
            n = half * per_res + n_loc
            q0 = r + dil * WIN_BLK * n_loc
            k0 = r + dil * WIN_BLK * n
            kp = r + dil * WIN_BLK * jnp.maximum(n - 1, 0)
            q4 = jnp.concatenate([q_ref[g, rows_at(q0, dil), :] for g in range(A_GROUP)], axis=0).astype(BF16)
            kc = jnp.concatenate([k_ref[rows_at(kp, dil), :], k_ref[rows_at(k0, dil), :]], axis=0).astype(BF16)
            vc = jnp.concatenate([v_ref[rows_at(kp, dil), :], v_ref[rows_at(k0, dil), :]], axis=0).astype(BF16)
            s = lax.dot_general(q4, kc, (((1,), (1,)), ((), ())), preferred_element_type=F32)
            s = s + bias_scr[jnp.minimum(n, 1)]
            m = jnp.max(s, axis=-1, keepdims=True)
            p = jnp.exp2(s - m)
            l = jnp.sum(p, axis=-1, keepdims=True)
            o = jnp.dot(p.astype(BF16), vc, preferred_element_type=F32) / l
            lse = m + jnp.log2(l)
            for g in range(A_GROUP):
                og = o[g * WIN_BLK:(g + 1) * WIN_BLK]
                lg = jnp.broadcast_to(lse[g * WIN_BLK:(g + 1) * WIN_BLK], (WIN_BLK, HEAD_DIM))
                if not first:
                    opg = acc_scr[g, rows_at(q0, dil), :]
                    lpg = lse_scr[g, rows_at(q0, dil), :]
                    w_prev = 1.0 / (1.0 + jnp.exp2(lg - lpg))
                    og = og + (opg - og) * w_prev
                    if not last:
                        lg = jnp.maximum(lpg, lg) + jnp.log2(1.0 + jnp.exp2(-jnp.abs(lg - lpg)))
                if last:
                    o_ref[pl.ds(pl.multiple_of(q0, WIN_BLK), WIN_BLK), g * HEAD_DIM:(g + 1) * HEAD_DIM] = (
                        og.astype(o_ref.dtype))
                else:
                    acc_scr[g, rows_at(q0, dil), :] = og
                    lse_scr[g, rows_at(q0, dil), :] = lg
            return carry

        lax.fori_loop(0, nblk, body, 0, unroll=2)


def _attn_a(pa):
    nbatch, _, seq, _ = pa.shape
    dilations = tuple(sorted((dil for _, dil in DILATED_BRANCHES), reverse=True))
    assert dilations[-1] == 1 and all(w // dil == WIN_BLK for w, dil in DILATED_BRANCHES)
    nhalf = 2 if seq % (2 * WIN_BLK * dilations[0]) == 0 else 1
    qrows = seq // nhalf
    assert qrows % (WIN_BLK * dilations[0]) == 0
    return pl.pallas_call(
        functools.partial(_attn_a_kernel, dilations=dilations),
        grid=(nbatch, A_KV_HEADS, nhalf),
        in_specs=[pl.BlockSpec((None, A_GROUP, qrows, HEAD_DIM), lambda b, h, t: (b, h, t, 0)),
                  pl.BlockSpec((None, None, seq, HEAD_DIM), lambda b, h, t: (b, A_HEADS + h, 0, 0)),
                  pl.BlockSpec((None, None, seq, HEAD_DIM),
                               lambda b, h, t: (b, A_HEADS + A_KV_HEADS + h, 0, 0))],
        out_specs=pl.BlockSpec((None, qrows, A_GROUP * HEAD_DIM), lambda b, h, t: (b, t, h)),
        out_shape=jax.ShapeDtypeStruct((nbatch, seq, A_HEADS * HEAD_DIM), BF16),
        scratch_shapes=[pltpu.VMEM((A_GROUP, qrows, HEAD_DIM), F32),
                        pltpu.VMEM((A_GROUP, qrows, HEAD_DIM), F32),
                        pltpu.VMEM((2, A_GROUP * WIN_BLK, 2 * WIN_BLK), F32)],
        compiler_params=_params(("parallel", "parallel", "arbitrary")),
        name="attn_a",
    )(pa, pa, pa)


KEY_CHUNK = 256


def _ordered_to_float(key):
    return pltpu.bitcast(key ^ ((key >> 31) & jnp.int32(0x7FFFFFFF)), F32)


LOWEST_FINITE_KEY = INT_MIN + 0x00800000


def _dsa_kernel(p_ref, wi_ref, k_ref, kk_ref, v_ref, o_ref,
                sc_scr, wib_scr, q_scr, p_scr, m_scr, l_scr, acc_scr, *, topk):
    i = pl.program_id(1)
    nchunks = (i + 2) // 2
    halves = KEY_CHUNK // LANES

    wi = wi_ref[...] * (IDX_HEADS ** -0.5 * IDX_DIM ** -0.5)
    lane = lax.broadcasted_iota(jnp.int32, (Q_BLK, LANES), 1)
    for h in range(IDX_HEADS):
        blk = p_ref[:, QI_OFF + (h // 2) * LANES:QI_OFF + (h // 2 + 1) * LANES]
        keep = (lane < IDX_DIM) if h % 2 == 0 else (lane >= IDX_DIM)
        q_scr[h * Q_BLK:(h + 1) * Q_BLK, :] = jnp.where(keep, blk, jnp.zeros_like(blk))
        wib_scr[h] = jnp.broadcast_to(wi[:, WI_LANE + h:WI_LANE + h + 1], (Q_BLK, LANES))
    qpos = i * Q_BLK + lax.broadcasted_iota(jnp.int32, (Q_BLK, KEY_CHUNK), 0)
    kiota = lax.broadcasted_iota(jnp.int32, (Q_BLK, KEY_CHUNK), 1)

    def score_chunk(c, carry):
        k0 = pl.multiple_of(c * KEY_CHUNK, KEY_CHUNK)
        z = lax.dot_general(q_scr[...], kk_ref[pl.ds(k0, KEY_CHUNK), :], (((1,), (1,)), ((), ())),
                            preferred_element_type=F32)
        sc = jnp.zeros((Q_BLK, KEY_CHUNK), F32)
        for h in range(IDX_HEADS):
            w = wib_scr[h]
            sc = sc + jnp.concatenate([w] * halves, axis=1) * jnp.maximum(z[h * Q_BLK:(h + 1) * Q_BLK], 0.0)
        sc_scr[c] = jnp.where(k0 + kiota <= qpos, sc, -jnp.inf)
        return carry

    lax.fori_loop(0, nchunks, score_chunk, 0)

    def bit_body(t, ans):
        trial = ans | jnp.left_shift(jnp.int32(1), 31 - t)
        thr = _ordered_to_float(trial ^ jnp.int32(INT_MIN))

        def count_chunk(c, acc):
            sc = sc_scr[c]
            for q in range(halves):
                acc = acc + jnp.where(sc[:, q * LANES:(q + 1) * LANES] >= thr, 1.0, 0.0)
            return acc

        acc = lax.fori_loop(0, nchunks, count_chunk, jnp.zeros((Q_BLK, LANES), F32))
        cnt = jnp.sum(acc, axis=1, keepdims=True)
        return jnp.where(cnt >= float(topk), trial, ans)

    ans = lax.fori_loop(0, 32, bit_body, jnp.zeros((Q_BLK, LANES), jnp.int32))
    tau = _ordered_to_float(jnp.maximum(ans ^ jnp.int32(INT_MIN), jnp.int32(LOWEST_FINITE_KEY)))
    tau2 = jnp.concatenate([tau] * halves, axis=1)

    for h in range(B_HEADS):
        q_scr[h * Q_BLK:(h + 1) * Q_BLK, :] = p_ref[:, QB_OFF + h * HEAD_DIM:QB_OFF + (h + 1) * HEAD_DIM]
    m_scr[...] = jnp.full(m_scr.shape, NEG, F32)
    l_scr[...] = jnp.zeros(l_scr.shape, F32)
    acc_scr[...] = jnp.zeros(acc_scr.shape, F32)

    def att_chunk(c, carry):
        k0 = pl.multiple_of(c * KEY_CHUNK, KEY_CHUNK)
        s = lax.dot_general(q_scr[...], k_ref[pl.ds(k0, KEY_CHUNK), :], (((1,), (1,)), ((), ())),
                            preferred_element_type=F32)
        bias = jnp.where(sc_scr[c] >= tau2, 0.0, NEG)
        for h in range(B_HEADS):
            hr = slice(h * Q_BLK, (h + 1) * Q_BLK)
            sh = s[hr] + bias
            m_old = m_scr[hr, :]
            m_new = jnp.maximum(m_old, jnp.max(sh, axis=-1, keepdims=True))
            alpha = jnp.exp2(m_old - m_new)
            p = jnp.exp2(sh - jnp.concatenate([m_new] * halves, axis=1))
            psum = p[:, :LANES]
            for q in range(1, halves):
                psum = psum + p[:, q * LANES:(q + 1) * LANES]
            l_scr[hr, :] = alpha * l_scr[hr, :] + psum
            m_scr[hr, :] = m_new
            p_scr[hr, :] = p.astype(BF16)
            acc_scr[hr, :] = alpha * acc_scr[hr, :]
        acc_scr[...] += jnp.dot(p_scr[...], v_ref[pl.ds(k0, KEY_CHUNK), :], preferred_element_type=F32)
        return carry

    lax.fori_loop(0, nchunks, att_chunk, 0)
    for h in range(B_HEADS):
        hr = slice(h * Q_BLK, (h + 1) * Q_BLK)
        l = jnp.sum(l_scr[hr, :], axis=-1, keepdims=True)
        o_ref[:, h * HEAD_DIM:(h + 1) * HEAD_DIM] = (acc_scr[hr, :] / l).astype(o_ref.dtype)


def _dsa(proj, aux):
    nbatch, seq, _ = proj.shape
    assert seq % KEY_CHUNK == 0
    topk = min(DSA_TOPK_MAX, seq // 4)
    qw = B_HEADS * HEAD_DIM
    full = lambda off: pl.BlockSpec((None, seq, LANES), lambda b, i: (b, 0, off // LANES))
    return pl.pallas_call(
        functools.partial(_dsa_kernel, topk=topk),
        grid=(nbatch, seq // Q_BLK),
        in_specs=[pl.BlockSpec((None, Q_BLK, P_COLS), lambda b, i: (b, i, 0)),
                  pl.BlockSpec((None, Q_BLK, LANES), lambda b, i: (b, i, 0)),
                  full(KB_OFF), full(KIKI_OFF), full(VB_OFF)],
        out_specs=pl.BlockSpec((None, Q_BLK, qw), lambda b, i: (b, i, 0)),
        out_shape=jax.ShapeDtypeStruct((nbatch, seq, qw), BF16),
        scratch_shapes=[pltpu.VMEM((seq // KEY_CHUNK, Q_BLK, KEY_CHUNK), F32),
                        pltpu.VMEM((IDX_HEADS, Q_BLK, LANES), F32),
                        pltpu.VMEM((B_HEADS * Q_BLK, HEAD_DIM), BF16),
                        pltpu.VMEM((B_HEADS * Q_BLK, KEY_CHUNK), BF16),
                        pltpu.VMEM((B_HEADS * Q_BLK, LANES), F32),
                        pltpu.VMEM((B_HEADS * Q_BLK, LANES), F32),
                        pltpu.VMEM((B_HEADS * Q_BLK, HEAD_DIM), F32)],
        compiler_params=_params(("parallel", "arbitrary")),
        name="dsa",
    )(proj, aux, proj, proj, proj)


def _out_proj_kernel(oa_ref, ob_ref, w_ref, x_ref, mod_ref, o_ref):
    ka = oa_ref.shape[-1]
    y = jnp.dot(oa_ref[...], w_ref[:ka, :], preferred_element_type=F32)
    y = y + jnp.dot(ob_ref[...], w_ref[ka:, :], preferred_element_type=F32)
    o_ref[...] = x_ref[...] + mod_ref[2:3, :] * y


def _out_proj(o_a, o_b, w_out, x, mod3):
    nbatch, seq, d = x.shape
    ka, kb = o_a.shape[-1], o_b.shape[-1]
    tm, tn = _tile(seq, 1024), _tile(d, 512)
    return pl.pallas_call(
        _out_proj_kernel,
        grid=(nbatch, seq // tm, d // tn),
        in_specs=[pl.BlockSpec((None, tm, ka), lambda b, i, j: (b, i, 0)),
                  pl.BlockSpec((None, tm, kb), lambda b, i, j: (b, i, 0)),
                  pl.BlockSpec((ka + kb, tn), lambda b, i, j: (0, j)),
                  pl.BlockSpec((None, tm, tn), lambda b, i, j: (b, i, j)),
                  pl.BlockSpec((None, ADA_CHUNKS, tn), lambda b, i, j: (b, 0, j))],
        out_specs=pl.BlockSpec((None, tm, tn), lambda b, i, j: (b, i, j)),
        out_shape=jax.ShapeDtypeStruct((nbatch, seq, d), F32),
        compiler_params=_params(("parallel", "parallel", "parallel")),
        name="out_proj",
    )(o_a, o_b, w_out, x, mod3)


def _peer_q_kernel(x_ref, mod_ref, g_ref, w_ref, keys_ref, h_ref, st_ref, h_scr):
    j = pl.program_id(2)

    @pl.when(j == 0)
    def _():
        h = _norm_mod(x_ref[...], g_ref[...], mod_ref[4:5, :], mod_ref[3:4, :]).astype(BF16)
        h_scr[...] = h
        h_ref[...] = h

    q = jnp.dot(h_scr[...], w_ref[...], preferred_element_type=F32)
    for blk in range(q.shape[1] // PEER_HALF):
        qh = q[:, blk * PEER_HALF:(blk + 1) * PEER_HALF]
        st_ref[blk] = lax.dot_general(keys_ref[blk], qh, (((1,), (1,)), ((), ())),
                                      preferred_element_type=F32, precision=lax.Precision.HIGHEST)


def _peer_q(x1, mod3, g, w_q, keys):
    nbatch, seq, d = x1.shape
    nq = w_q.shape[1]
    tm, tn = _tile(seq, 512), 4 * PEER_HALF
    nhp = nq // PEER_HALF
    nt = seq // tm
    return pl.pallas_call(
        _peer_q_kernel,
        grid=(nbatch, nt, nq // tn),
        in_specs=[pl.BlockSpec((None, tm, d), lambda b, i, j: (b, i, 0)),
                  pl.BlockSpec((None, ADA_CHUNKS, d), lambda b, i, j: (b, 0, 0)),
                  pl.BlockSpec((1, d), lambda b, i, j: (0, 0)),
                  pl.BlockSpec((d, tn), lambda b, i, j: (0, j)),
                  pl.BlockSpec((tn // PEER_HALF, PEER_NKEYS, PEER_HALF), lambda b, i, j: (j, 0, 0))],
        out_specs=[pl.BlockSpec((None, tm, d), lambda b, i, j: (b, i, 0)),
                   pl.BlockSpec((tn // PEER_HALF, PEER_NKEYS, tm), lambda b, i, j: (j, 0, b * nt + i))],
        out_shape=[jax.ShapeDtypeStruct((nbatch, seq, d), BF16),
                   jax.ShapeDtypeStruct((nhp, PEER_NKEYS, nbatch * seq), F32)],
        scratch_shapes=[pltpu.VMEM((tm, d), BF16)],
        compiler_params=_params(("parallel", "parallel", "arbitrary")),
        name="peer_q",
    )(x1, mod3, g.reshape(1, d), w_q, keys)


def _top_rows(x, count, one_at_a_time):
    rows = []
    ridx = lax.broadcasted_iota(jnp.int32, x.shape, 0)
    for _ in range(count):
        m = jnp.max(x, axis=0, keepdims=True)
        rows.append(m)
        hit = x == m
        if one_at_a_time:
            first = jnp.min(jnp.where(hit, ridx, x.shape[0]), axis=0, keepdims=True)
            hit = ridx == first
        x = jnp.where(hit, -jnp.inf, x)
    removed = jnp.sum(jnp.where(x == -jnp.inf, 1.0, 0.0), axis=0, keepdims=True)
    return rows, removed


def _peer_group_stats(x0, x1, one_at_a_time):
    half = PEER_TOPK // 2
    v0, r0 = _top_rows(x0, PEER_TOPK, one_at_a_time)
    v1, r1 = _top_rows(x1, PEER_TOPK, one_at_a_time)
    v0_all = jnp.concatenate(v0, axis=0)
    v1_all = jnp.concatenate(v1, axis=0)
    cand = [v0[0] + v1_all]
    cand += [v0[a] + v1_all[:half] for a in range(1, half)]
    cand += [v0_all[half:] + v1[0]]
    top, rc = _top_rows(jnp.concatenate(cand, axis=0), PEER_TOPK, one_at_a_time)
    z = jnp.zeros_like(top[0])
    for t in top:
        z = z + jnp.exp(t - top[0])
    stats = jnp.concatenate([top[-1], v0[0], v1[0], 1.0 / z] + [jnp.zeros_like(z)] * 4, axis=0)
    repeated = jnp.max(jnp.maximum(jnp.maximum(r0, r1), rc)) > float(PEER_TOPK)
    return stats, repeated


def _peer_topk_kernel(st_ref, stats_ref):
    ngroups = st_ref.shape[-1] // LANES

    def groups(h, one_at_a_time):
        out = []
        for tg in range(ngroups):
            cols = slice(tg * LANES, (tg + 1) * LANES)
            out.append(_peer_group_stats(st_ref[2 * h, :, cols], st_ref[2 * h + 1, :, cols], one_at_a_time))
        return out

    def per_head(h, carry):
        fast = groups(h, False)
        repeated = fast[0][1]
        for tg in range(ngroups):
            stats_ref[h, :, tg * LANES:(tg + 1) * LANES] = fast[tg][0]
            repeated = repeated | fast[tg][1]

        @pl.when(repeated)
        def _():
            for tg, (stats, _) in enumerate(groups(h, True)):
                stats_ref[h, :, tg * LANES:(tg + 1) * LANES] = stats
        return carry

    lax.fori_loop(0, PEER_HEADS, per_head, 0)


def _peer_topk(st):
    nhp, nkeys, ntok = st.shape
    tmk = _tile(ntok, 256)
    return pl.pallas_call(
        _peer_topk_kernel,
        grid=(ntok // tmk,),
        in_specs=[pl.BlockSpec((nhp, nkeys, tmk), lambda i: (0, 0, i))],
        out_specs=pl.BlockSpec((PEER_HEADS, 8, tmk), lambda i: (0, 0, i)),
        out_shape=jax.ShapeDtypeStruct((PEER_HEADS, 8, ntok), F32),
        compiler_params=_params(("parallel",)),
        name="peer_topk",
    )(st)


EXPERT_SPLIT = 2


def _peer_dense_kernel(x_ref, u_ref, v_ref, st_ref, stats_ref, o_ref, e1_scr):
    e = pl.program_id(1)
    te, tm = u_ref.shape[0], x_ref.shape[0]
    ni = te // PEER_NKEYS
    hw = te // EXPERT_SPLIT

    @pl.when(e == 0)
    def _():
        o_ref[...] = jnp.zeros_like(o_ref)
        for h in range(PEER_HEADS):
            e1_scr[h] = jnp.exp(st_ref[2 * h + 1] - stats_ref[h, 2:3, :]) * stats_ref[h, 3:4, :]

    def scores(p):
        return lax.dot_general(u_ref[p * hw:(p + 1) * hw, :], x_ref[...], (((1,), (1,)), ((), ())),
                               preferred_element_type=F32)

    def activation(p, a_t):
        cols = []
        for il in range(p * hw // PEER_NKEYS, (p + 1) * hw // PEER_NKEYS):
            i = e * ni + il
            rows = []
            for tc in range(tm // LANES):
                ts = slice(tc * LANES, (tc + 1) * LANES)
                g = jnp.zeros((PEER_NKEYS, LANES), F32)
                for h in range(PEER_HEADS):
                    s0 = st_ref[2 * h, pl.ds(i, 1), :][:, ts]
                    e0 = jnp.exp(s0 - stats_ref[h, 1:2, ts])
                    sel = (s0 + st_ref[2 * h + 1, :, ts]) >= stats_ref[h, 0:1, ts]
                    g = g + jnp.where(sel, e0 * e1_scr[h, :, ts], 0.0)
                r0 = il * PEER_NKEYS - p * hw
                a = a_t[r0:r0 + PEER_NKEYS, ts]
                act = 0.5 * a * (1.0 + lax.erf(a * (2.0 ** -0.5))) * g
                rows.append(act.T.astype(BF16))
            cols.append(jnp.concatenate(rows, axis=0))
        return jnp.concatenate(cols, axis=1)

    a_parts = [scores(p) for p in range(EXPERT_SPLIT)]
    for p in range(EXPERT_SPLIT):
        o_ref[...] += jnp.dot(activation(p, a_parts[p]), v_ref[p * hw:(p + 1) * hw, :],
                              preferred_element_type=F32)


def _peer_dense(h2, u_tab, v_tab, st, stats):
    ntok, d = h2.shape
    nexp = u_tab.shape[0]
    tm, te = _tile(ntok, 512), 4 * PEER_NKEYS
    nhp = st.shape[0]
    return pl.pallas_call(
        _peer_dense_kernel,
        grid=(ntok // tm, nexp // te),
        in_specs=[pl.BlockSpec((tm, d), lambda i, e: (i, 0)),
                  pl.BlockSpec((te, d), lambda i, e: (e, 0)),
                  pl.BlockSpec((te, d), lambda i, e: (e, 0)),
                  pl.BlockSpec((nhp, PEER_NKEYS, tm), lambda i, e: (0, 0, i)),
                  pl.BlockSpec((PEER_HEADS, 8, tm), lambda i, e: (0, 0, i))],
        out_specs=pl.BlockSpec((tm, d), lambda i, e: (i, 0)),
        out_shape=jax.ShapeDtypeStruct((ntok, d), F32),
        scratch_shapes=[pltpu.VMEM((PEER_HEADS, PEER_NKEYS, tm), F32)],
        compiler_params=_params(("parallel", "arbitrary")),
        name="peer_dense",
    )(h2, u_tab, v_tab, st, stats)


def _final_kernel(x_ref, p_ref, mod_ref, g_ref, o_ref, *, normalize):
    y = x_ref[...] + mod_ref[5:6, :] * p_ref[...]
    if normalize:
        y = (y * lax.rsqrt(jnp.mean(y * y, axis=-1, keepdims=True) + NORM_EPS)) * g_ref[...]
    o_ref[...] = y


def _final(x1, peer_out, mod3, g, normalize):
    nbatch, seq, d = x1.shape
    tm = _tile(seq, 256)
    spec = pl.BlockSpec((None, tm, d), lambda b, i: (b, i, 0))
    return pl.pallas_call(
        functools.partial(_final_kernel, normalize=normalize),
        grid=(nbatch, seq // tm),
        in_specs=[spec, spec,
                  pl.BlockSpec((None, ADA_CHUNKS, d), lambda b, i: (b, 0, 0)),
                  pl.BlockSpec((1, d), lambda b, i: (0, 0))],
        out_specs=spec,
        out_shape=jax.ShapeDtypeStruct((nbatch, seq, d), F32),
        compiler_params=_params(("parallel", "parallel")),
        name="final",
    )(x1, peer_out, mod3, g.reshape(1, d))


def kernel(x, c, positions, ln1_g, ln2_g, w_ada, b_ada, w_in, w_out, peer_wq, peer_sub_keys, peer_u,
           peer_v, lnf_g):
    nbatch, seq, d = x.shape
    depth = w_ada.shape[0]
    tables = _rope_tables(positions)
    for layer in range(depth):
        mod3 = _ada(c, w_ada[layer], b_ada[layer]).reshape(nbatch, ADA_CHUNKS, d)
        pa, proj, aux = _in_proj(x, mod3, ln1_g[layer], _prep_w_in(w_in[layer]), tables)
        o_a = _attn_a(pa)
        o_b = _dsa(proj, aux)
        x = _out_proj(o_a, o_b, w_out[layer].astype(BF16), x, mod3)
        keys = peer_sub_keys[layer].reshape(2 * PEER_HEADS, PEER_NKEYS, PEER_HALF)
        h2, st = _peer_q(x, mod3, ln2_g[layer], peer_wq[layer].astype(BF16), keys)
        stats = _peer_topk(st)
        peer_out = _peer_dense(h2.reshape(nbatch * seq, d), peer_u[layer].astype(BF16),
                               peer_v[layer].astype(BF16), st, stats)
        x = _final(x, peer_out.reshape(nbatch, seq, d), mod3, lnf_g, normalize=layer + 1 == depth)
    return x
```

```python
import functools
import math

import jax
import jax.numpy as jnp
from jax import lax
from jax.experimental import pallas as pl
from jax.experimental.pallas import tpu as pltpu

F32 = jnp.float32
BF16 = jnp.bfloat16

HEAD_DIM = 128
A_HEADS = 16
A_KV_HEADS = 4
A_GROUP = A_HEADS // A_KV_HEADS
DILATED_BRANCHES = ((128, 1), (512, 4), (2048, 16))
WIN_BLK = 128
B_HEADS = 16
IDX_HEADS = 16
IDX_DIM = 64
DSA_TOPK_MAX = 256
Q_BLK = 128
PEER_HEADS = 8
PEER_NKEYS = 128
PEER_HALF = 128
PEER_TOPK = 16
ROPE_THETA = 10000.0
NORM_EPS = 1e-6
NEG = -1e30
ADA_CHUNKS = 6
INT_MIN = -(2 ** 31)

LANES = 128
VMEM_LIMIT = 56 * 1024 * 1024

PROJ_TILE = 512
IN_COLS = ((A_HEADS + 2 * A_KV_HEADS + B_HEADS + 2) * HEAD_DIM + IDX_HEADS * IDX_DIM + IDX_DIM + IDX_HEADS)
PROJ_TILES = -(-IN_COLS // PROJ_TILE)
A_TILES = (A_HEADS + 2 * A_KV_HEADS) * HEAD_DIM // PROJ_TILE
KA_TILE = A_HEADS * HEAD_DIM // PROJ_TILE
VA_TILE = KA_TILE + 1
PA_HEADS = A_HEADS + 2 * A_KV_HEADS
QB_OFF = 0
KB_OFF = QB_OFF + B_HEADS * HEAD_DIM
VB_OFF = KB_OFF + HEAD_DIM
QI_OFF = VB_OFF + HEAD_DIM
KIKI_OFF = QI_OFF + IDX_HEADS * IDX_DIM
P_COLS = (PROJ_TILES - A_TILES) * PROJ_TILE
WI_LANE = IDX_DIM


def _params(semantics):
    return pltpu.CompilerParams(dimension_semantics=semantics, vmem_limit_bytes=VMEM_LIMIT)


def _tile(n, pref):
    return pref if n % pref == 0 else n


def _norm_mod(x, g, scale, shift):
    xf = x.astype(F32)
    y = xf * lax.rsqrt(jnp.mean(xf * xf, axis=-1, keepdims=True) + NORM_EPS)
    return (y * g) * (1.0 + scale) + shift


def _ada_kernel(ct_ref, w_ref, b_ref, o_ref, *, nbatch):
    @pl.when(pl.program_id(1) == 0)
    def _():
        o_ref[...] = jnp.broadcast_to(b_ref[...], o_ref.shape)

    ct = ct_ref[...]
    s = ct * jax.nn.sigmoid(ct)
    w = w_ref[...]
    rows = [jnp.sum(w * s[:, b:b + 1], axis=0, keepdims=True) for b in range(nbatch)]
    o_ref[...] += jnp.concatenate(rows, axis=0)


def _ada(c, w_ada, b_ada):
    nbatch, d = c.shape
    n = w_ada.shape[1]
    tk, tn = _tile(d, 1024), _tile(n, 2048)
    return pl.pallas_call(
        functools.partial(_ada_kernel, nbatch=nbatch),
        grid=(n // tn, d // tk),
        in_specs=[pl.BlockSpec((tk, nbatch), lambda j, k: (k, 0)),
                  pl.BlockSpec((tk, tn), lambda j, k: (k, j)),
                  pl.BlockSpec((1, tn), lambda j, k: (0, j))],
        out_specs=pl.BlockSpec((nbatch, tn), lambda j, k: (0, j)),
        out_shape=jax.ShapeDtypeStruct((nbatch, n), F32),
        compiler_params=_params(("parallel", "arbitrary")),
        name="ada",
    )(c.T, w_ada, b_ada.reshape(1, n))


def _rope_kernel(pos_ref, ch_ref, sh_ref, ci_ref, sa_ref, sb_ref):
    pos = pos_ref[...].astype(F32)
    lane = lax.broadcasted_iota(jnp.int32, (1, LANES), 1)
    expo_h = -((2 * (lane & 63)).astype(F32)) / HEAD_DIM
    expo_i = -((2 * (lane & 31)).astype(F32)) / IDX_DIM
    inv = jnp.power(ROPE_THETA, jnp.where(lane < 64, expo_h, expo_i))
    ang = pos * inv
    c = jnp.cos(ang)
    s = jnp.sin(ang)
    ch_ref[...] = jnp.where(lane < 64, c, pltpu.roll(c, 64, 1))
    sh_ref[...] = jnp.where(lane < 64, -s, pltpu.roll(s, 64, 1))
    grp = lane >> 5
    c64, c96, c32 = pltpu.roll(c, 64, 1), pltpu.roll(c, 96, 1), pltpu.roll(c, 32, 1)
    s64, s96, s32 = pltpu.roll(s, 64, 1), pltpu.roll(s, 96, 1), pltpu.roll(s, 32, 1)
    ci_ref[...] = jnp.where(grp == 0, c64, jnp.where(grp == 1, c96, jnp.where(grp == 2, c, c32)))
    sa_ref[...] = jnp.where(grp == 0, -s64, jnp.where(grp == 2, -s, 0.0))
    sb_ref[...] = jnp.where(grp == 1, s96, jnp.where(grp == 3, s32, 0.0))


def _rope_tables(positions):
    nbatch, seq = positions.shape
    ts = _tile(seq, 1024)
    spec = pl.BlockSpec((None, ts, LANES), lambda b, i: (b, i, 0))
    shape = jax.ShapeDtypeStruct((nbatch, seq, LANES), F32)
    return pl.pallas_call(
        _rope_kernel,
        grid=(nbatch, seq // ts),
        in_specs=[pl.BlockSpec((None, ts, 1), lambda b, i: (b, i, 0))],
        out_specs=[spec] * 5,
        out_shape=[shape] * 5,
        compiler_params=_params(("parallel", "parallel")),
        name="rope",
    )(positions.reshape(nbatch, seq, 1))


def _rope_head(a, ch, sh):
    return a * ch + pltpu.roll(a, 64, 1) * sh


def _rope_idx(a, ci, sa, sb):
    return a * ci + pltpu.roll(a, 96, 1) * sa + pltpu.roll(a, 32, 1) * sb


def _in_proj_kernel(x_ref, mod_ref, g_ref, w_ref, ch_ref, sh_ref, ci_ref, sa_ref, sb_ref,
                    pa_ref, p_ref, aux_ref, h_scr, acc_scr):
    j = pl.program_id(2)

    @pl.when(j == 0)
    def _():
        h_scr[...] = _norm_mod(x_ref[...], g_ref[...], mod_ref[1:2, :], mod_ref[0:1, :]).astype(BF16)

    acc_scr[...] = jnp.dot(h_scr[...], w_ref[...], preferred_element_type=F32)
    nblk = PROJ_TILE // LANES
    qscale = HEAD_DIM ** -0.5 * math.log2(math.e)
    qb_tile = A_TILES + QB_OFF // PROJ_TILE
    mix_tile = A_TILES + KB_OFF // PROJ_TILE
    last_tile = PROJ_TILES - 1

    def blk(q):
        return acc_scr[:, q * LANES:(q + 1) * LANES]

    def put(q, val):
        p_ref[:, q * LANES:(q + 1) * LANES] = val.astype(BF16)

    def head(q):
        return _rope_head(blk(q), ch_ref[...], sh_ref[...])

    def idx(q):
        return _rope_idx(blk(q), ci_ref[...], sa_ref[...], sb_ref[...])

    @pl.when(j < KA_TILE)
    def _():
        for q in range(nblk):
            pa_ref[q] = head(q) * qscale

    @pl.when(j == KA_TILE)
    def _():
        for q in range(nblk):
            pa_ref[q] = head(q)

    @pl.when(j == VA_TILE)
    def _():
        for q in range(nblk):
            pa_ref[q] = blk(q)

    @pl.when((j >= qb_tile) & (j < mix_tile))
    def _():
        for q in range(nblk):
            put(q, head(q) * qscale)

    @pl.when(j == mix_tile)
    def _():
        put(0, head(0))
        put(1, blk(1))
        put(2, idx(2))
        put(3, idx(3))

    @pl.when((j > mix_tile) & (j < last_tile))
    def _():
        for q in range(nblk):
            put(q, idx(q))

    @pl.when(j == last_tile)
    def _():
        put(0, idx(0))
        put(1, idx(1))
        ki = idx(2)
        lane = lax.broadcasted_iota(jnp.int32, ki.shape, 1)
        put(2, jnp.where(lane < IDX_DIM, ki, pltpu.roll(ki, IDX_DIM, 1)))
        put(3, blk(3))
        aux_ref[...] = blk(2)


def _prep_w_in(w_in):
    assert w_in.shape[1] == IN_COLS
    return jnp.pad(w_in.astype(BF16), ((0, 0), (0, PROJ_TILES * PROJ_TILE - IN_COLS)))


def _in_proj(x, mod3, g, w_perm, tables):
    nbatch, seq, d = x.shape
    tm = _tile(seq, 512)
    nblk = PROJ_TILE // LANES
    tab_spec = pl.BlockSpec((None, tm, LANES), lambda b, i, j: (b, i, 0))
    return pl.pallas_call(
        _in_proj_kernel,
        grid=(nbatch, seq // tm, PROJ_TILES),
        in_specs=[pl.BlockSpec((None, tm, d), lambda b, i, j: (b, i, 0)),
                  pl.BlockSpec((None, ADA_CHUNKS, d), lambda b, i, j: (b, 0, 0)),
                  pl.BlockSpec((1, d), lambda b, i, j: (0, 0)),
                  pl.BlockSpec((d, PROJ_TILE), lambda b, i, j: (0, j))] + [tab_spec] * 5,
        out_specs=[pl.BlockSpec((None, nblk, tm, LANES), lambda b, i, j: (b, jnp.minimum(j, A_TILES - 1), i, 0)),
                   pl.BlockSpec((None, tm, PROJ_TILE), lambda b, i, j: (b, i, jnp.maximum(j - A_TILES, 0))),
                   pl.BlockSpec((None, tm, LANES), lambda b, i, j: (b, i, 0))],
        out_shape=[jax.ShapeDtypeStruct((nbatch, PA_HEADS, seq, LANES), F32),
                   jax.ShapeDtypeStruct((nbatch, seq, P_COLS), BF16),
                   jax.ShapeDtypeStruct((nbatch, seq, LANES), F32)],
        scratch_shapes=[pltpu.VMEM((tm, d), BF16), pltpu.VMEM((tm, PROJ_TILE), F32)],
        compiler_params=_params(("parallel", "parallel", "arbitrary")),
        name="in_proj",
    )(x, mod3, g.reshape(1, d), w_perm, *tables)


def _attn_a_kernel(q_ref, k_ref, v_ref, o_ref, acc_scr, lse_scr, bias_scr, *, dilations):
    half = pl.program_id(2)
    qrows = q_ref.shape[1]
    nblk = qrows // WIN_BLK

    rows = A_GROUP * WIN_BLK
    row = lax.broadcasted_iota(jnp.int32, (rows, 2 * WIN_BLK), 0) & (WIN_BLK - 1)
    col = lax.broadcasted_iota(jnp.int32, (rows, 2 * WIN_BLK), 1)
    band = (col >= row) & (col <= row + WIN_BLK)
    bias_scr[0] = jnp.where(band & (col >= WIN_BLK), 0.0, NEG)
    bias_scr[1] = jnp.where(band, 0.0, NEG)

    def rows_at(start, dil):
        return pl.ds(start, WIN_BLK) if dil == 1 else pl.ds(start, WIN_BLK, stride=dil)

    for idx, dil in enumerate(dilations):
        per_res = nblk // dil
        first, last = idx == 0, idx == len(dilations) - 1

        def body(blk, carry, dil=dil, per_res=per_res, first=first, last=last):
            r = blk // per_res
            n_loc = blk - r * per_res
            n = half * per_res + n_loc
            q0 = r + dil * WIN_BLK * n_loc
            k0 = r + dil * WIN_BLK * n
            kp = r + dil * WIN_BLK * jnp.maximum(n - 1, 0)
            q4 = jnp.concatenate([q_ref[g, rows_at(q0, dil), :] for g in range(A_GROUP)], axis=0).astype(BF16)
            kc = jnp.concatenate([k_ref[rows_at(kp, dil), :], k_ref[rows_at(k0, dil), :]], axis=0).astype(BF16)
            vc = jnp.concatenate([v_ref[rows_at(kp, dil), :], v_ref[rows_at(k0, dil), :]], axis=0).astype(BF16)
            s = lax.dot_general(q4, kc, (((1,), (1,)), ((), ())), preferred_element_type=F32)
            s = s + bias_scr[jnp.minimum(n, 1)]
            m = jnp.max(s, axis=-1, keepdims=True)
            p = jnp.exp2(s - m)
            l = jnp.sum(p, axis=-1, keepdims=True)
            o = jnp.dot(p.astype(BF16), vc, preferred_element_type=F32) / l
            lse = m + jnp.log2(l)
            for g in range(A_GROUP):
                og = o[g * WIN_BLK:(g + 1) * WIN_BLK]
                lg = jnp.broadcast_to(lse[g * WIN_BLK:(g + 1) * WIN_BLK], (WIN_BLK, HEAD_DIM))
                if not first:
                    opg = acc_scr[g, rows_at(q0, dil), :]
                    lpg = lse_scr[g, rows_at(q0, dil), :]
                    w_prev = 1.0 / (1.0 + jnp.exp2(lg - lpg))
                    og = og + (opg - og) * w_prev
                    if not last:
                        lg = jnp.maximum(lpg, lg) + jnp.log2(1.0 + jnp.exp2(-jnp.abs(lg - lpg)))
                if last:
                    o_ref[pl.ds(pl.multiple_of(q0, WIN_BLK), WIN_BLK), g * HEAD_DIM:(g + 1) * HEAD_DIM] = (
                        og.astype(o_ref.dtype))
                else:
                    acc_scr[g, rows_at(q0, dil), :] = og
                    lse_scr[g, rows_at(q0, dil), :] = lg
            return carry

        lax.fori_loop(0, nblk, body, 0, unroll=2)


def _attn_a(pa):
    nbatch, _, seq, _ = pa.shape
    dilations = tuple(sorted((dil for _, dil in DILATED_BRANCHES), reverse=True))
    assert dilations[-1] == 1 and all(w // dil == WIN_BLK for w, dil in DILATED_BRANCHES)
    nhalf = 2 if seq % (2 * WIN_BLK * dilations[0]) == 0 else 1
    qrows = seq // nhalf
    assert qrows % (WIN_BLK * dilations[0]) == 0
    return pl.pallas_call(
        functools.partial(_attn_a_kernel, dilations=dilations),
        grid=(nbatch, A_KV_HEADS, nhalf),
        in_specs=[pl.BlockSpec((None, A_GROUP, qrows, HEAD_DIM), lambda b, h, t: (b, h, t, 0)),
                  pl.BlockSpec((None, None, seq, HEAD_DIM), lambda b, h, t: (b, A_HEADS + h, 0, 0)),
                  pl.BlockSpec((None, None, seq, HEAD_DIM),
                               lambda b, h, t: (b, A_HEADS + A_KV_HEADS + h, 0, 0))],
        out_specs=pl.BlockSpec((None, qrows, A_GROUP * HEAD_DIM), lambda b, h, t: (b, t, h)),
        out_shape=jax.ShapeDtypeStruct((nbatch, seq, A_HEADS * HEAD_DIM), BF16),
        scratch_shapes=[pltpu.VMEM((A_GROUP, qrows, HEAD_DIM), F32),
                        pltpu.VMEM((A_GROUP, qrows, HEAD_DIM), F32),
                        pltpu.VMEM((2, A_GROUP * WIN_BLK, 2 * WIN_BLK), F32)],
        compiler_params=_params(("parallel", "parallel", "arbitrary")),
        name="attn_a",
    )(pa, pa, pa)


KEY_CHUNK = 256


def _ordered_to_float(key):
    return pltpu.bitcast(key ^ ((key >> 31) & jnp.int32(0x7FFFFFFF)), F32)


LOWEST_FINITE_KEY = INT_MIN + 0x00800000


def _dsa_kernel(p_ref, wi_ref, k_ref, kk_ref, v_ref, *rest, topk, ncast):
    cast_in, rest = rest[:ncast], rest[ncast:]
    o_ref, cast_out = rest[0], rest[1:1 + ncast]
    sc_scr, wib_scr, q_scr, p_scr, m_scr, l_scr, acc_scr = rest[1 + ncast:]
    i = pl.program_id(1)
    nchunks = (i + 2) // 2
    halves = KEY_CHUNK // LANES

    for src, dst in zip(cast_in, cast_out):
        dst[...] = src[...].astype(dst.dtype)

    wi = wi_ref[...] * (IDX_HEADS ** -0.5 * IDX_DIM ** -0.5)
    lane = lax.broadcasted_iota(jnp.int32, (Q_BLK, LANES), 1)
    for h in range(IDX_HEADS):
        blk = p_ref[:, QI_OFF + (h // 2) * LANES:QI_OFF + (h // 2 + 1) * LANES]
        keep = (lane < IDX_DIM) if h % 2 == 0 else (lane >= IDX_DIM)
        q_scr[h * Q_BLK:(h + 1) * Q_BLK, :] = jnp.where(keep, blk, jnp.zeros_like(blk))
        wib_scr[h] = jnp.broadcast_to(wi[:, WI_LANE + h:WI_LANE + h + 1], (Q_BLK, LANES))
    qpos = i * Q_BLK + lax.broadcasted_iota(jnp.int32, (Q_BLK, KEY_CHUNK), 0)
    kiota = lax.broadcasted_iota(jnp.int32, (Q_BLK, KEY_CHUNK), 1)

    def score_chunk(c, carry):
        k0 = pl.multiple_of(c * KEY_CHUNK, KEY_CHUNK)
        z = lax.dot_general(q_scr[...], kk_ref[pl.ds(k0, KEY_CHUNK), :], (((1,), (1,)), ((), ())),
                            preferred_element_type=F32)
        sc = jnp.zeros((Q_BLK, KEY_CHUNK), F32)
        for h in range(IDX_HEADS):
            w = wib_scr[h]
            sc = sc + jnp.concatenate([w] * halves, axis=1) * jnp.maximum(z[h * Q_BLK:(h + 1) * Q_BLK], 0.0)
        sc_scr[c] = jnp.where(k0 + kiota <= qpos, sc, -jnp.inf)
        return carry

    lax.fori_loop(0, nchunks, score_chunk, 0)

    def bit_body(t, ans):
        trial = ans | jnp.left_shift(jnp.int32(1), 31 - t)
        thr = _ordered_to_float(trial ^ jnp.int32(INT_MIN))

        def count_chunk(c, acc):
            sc = sc_scr[c]
            for q in range(halves):
                acc = acc + jnp.where(sc[:, q * LANES:(q + 1) * LANES] >= thr, 1.0, 0.0)
            return acc

        acc = lax.fori_loop(0, nchunks, count_chunk, jnp.zeros((Q_BLK, LANES), F32))
        cnt = jnp.sum(acc, axis=1, keepdims=True)
        return jnp.where(cnt >= float(topk), trial, ans)

    ans = lax.fori_loop(0, 32, bit_body, jnp.zeros((Q_BLK, LANES), jnp.int32))
    tau = _ordered_to_float(jnp.maximum(ans ^ jnp.int32(INT_MIN), jnp.int32(LOWEST_FINITE_KEY)))
    tau2 = jnp.concatenate([tau] * halves, axis=1)

    for h in range(B_HEADS):
        q_scr[h * Q_BLK:(h + 1) * Q_BLK, :] = p_ref[:, QB_OFF + h * HEAD_DIM:QB_OFF + (h + 1) * HEAD_DIM]
    m_scr[...] = jnp.full(m_scr.shape, NEG, F32)
    l_scr[...] = jnp.zeros(l_scr.shape, F32)
    acc_scr[...] = jnp.zeros(acc_scr.shape, F32)

    def att_chunk(c, carry):
        k0 = pl.multiple_of(c * KEY_CHUNK, KEY_CHUNK)
        s = lax.dot_general(q_scr[...], k_ref[pl.ds(k0, KEY_CHUNK), :], (((1,), (1,)), ((), ())),
                            preferred_element_type=F32)
        bias = jnp.where(sc_scr[c] >= tau2, 0.0, NEG)
        for h in range(B_HEADS):
            hr = slice(h * Q_BLK, (h + 1) * Q_BLK)
            sh = s[hr] + bias
            m_old = m_scr[hr, :]
            m_new = jnp.maximum(m_old, jnp.max(sh, axis=-1, keepdims=True))
            alpha = jnp.exp2(m_old - m_new)
            p = jnp.exp2(sh - jnp.concatenate([m_new] * halves, axis=1))
            psum = p[:, :LANES]
            for q in range(1, halves):
                psum = psum + p[:, q * LANES:(q + 1) * LANES]
            l_scr[hr, :] = alpha * l_scr[hr, :] + psum
            m_scr[hr, :] = m_new
            p_scr[hr, :] = p.astype(BF16)
            acc_scr[hr, :] = alpha * acc_scr[hr, :]
        acc_scr[...] += jnp.dot(p_scr[...], v_ref[pl.ds(k0, KEY_CHUNK), :], preferred_element_type=F32)
        return carry

    lax.fori_loop(0, nchunks, att_chunk, 0)
    for h in range(B_HEADS):
        hr = slice(h * Q_BLK, (h + 1) * Q_BLK)
        l = jnp.sum(l_scr[hr, :], axis=-1, keepdims=True)
        o_ref[:, h * HEAD_DIM:(h + 1) * HEAD_DIM] = (acc_scr[hr, :] / l).astype(o_ref.dtype)


def _dsa(proj, aux, f32_weights):
    nbatch, seq, _ = proj.shape
    assert seq % KEY_CHUNK == 0
    topk = min(DSA_TOPK_MAX, seq // 4)
    qw = B_HEADS * HEAD_DIM
    nq = seq // Q_BLK
    steps = nbatch * nq
    full = lambda off: pl.BlockSpec((None, seq, LANES), lambda b, i: (b, 0, off // LANES))
    cast_specs = []
    for w in f32_weights:
        assert w.shape[0] % steps == 0
        cast_specs.append(pl.BlockSpec((w.shape[0] // steps, w.shape[1]), lambda b, i: (b * nq + i, 0)))
    outs = pl.pallas_call(
        functools.partial(_dsa_kernel, topk=topk, ncast=len(f32_weights)),
        grid=(nbatch, nq),
        in_specs=[pl.BlockSpec((None, Q_BLK, P_COLS), lambda b, i: (b, i, 0)),
                  pl.BlockSpec((None, Q_BLK, LANES), lambda b, i: (b, i, 0)),
                  full(KB_OFF), full(KIKI_OFF), full(VB_OFF)] + cast_specs,
        out_specs=[pl.BlockSpec((None, Q_BLK, qw), lambda b, i: (b, i, 0))] + cast_specs,
        out_shape=[jax.ShapeDtypeStruct((nbatch, seq, qw), BF16)]
        + [jax.ShapeDtypeStruct(w.shape, BF16) for w in f32_weights],
        scratch_shapes=[pltpu.VMEM((seq // KEY_CHUNK, Q_BLK, KEY_CHUNK), F32),
                        pltpu.VMEM((IDX_HEADS, Q_BLK, LANES), F32),
                        pltpu.VMEM((B_HEADS * Q_BLK, HEAD_DIM), BF16),
                        pltpu.VMEM((B_HEADS * Q_BLK, KEY_CHUNK), BF16),
                        pltpu.VMEM((B_HEADS * Q_BLK, LANES), F32),
                        pltpu.VMEM((B_HEADS * Q_BLK, LANES), F32),
                        pltpu.VMEM((B_HEADS * Q_BLK, HEAD_DIM), F32)],
        compiler_params=_params(("parallel", "arbitrary")),
        name="dsa",
    )(proj, aux, proj, proj, proj, *f32_weights)
    return outs[0], outs[1:]


def _out_proj_kernel(oa_ref, ob_ref, w_ref, x_ref, mod_ref, o_ref):
    ka = oa_ref.shape[-1]
    y = jnp.dot(oa_ref[...], w_ref[:ka, :], preferred_element_type=F32)
    y = y + jnp.dot(ob_ref[...], w_ref[ka:, :], preferred_element_type=F32)
    o_ref[...] = x_ref[...] + mod_ref[2:3, :] * y


def _out_proj(o_a, o_b, w_out, x, mod3):
    nbatch, seq, d = x.shape
    ka, kb = o_a.shape[-1], o_b.shape[-1]
    tm, tn = _tile(seq, 1024), _tile(d, 512)
    return pl.pallas_call(
        _out_proj_kernel,
        grid=(nbatch, seq // tm, d // tn),
        in_specs=[pl.BlockSpec((None, tm, ka), lambda b, i, j: (b, i, 0)),
                  pl.BlockSpec((None, tm, kb), lambda b, i, j: (b, i, 0)),
                  pl.BlockSpec((ka + kb, tn), lambda b, i, j: (0, j)),
                  pl.BlockSpec((None, tm, tn), lambda b, i, j: (b, i, j)),
                  pl.BlockSpec((None, ADA_CHUNKS, tn), lambda b, i, j: (b, 0, j))],
        out_specs=pl.BlockSpec((None, tm, tn), lambda b, i, j: (b, i, j)),
        out_shape=jax.ShapeDtypeStruct((nbatch, seq, d), F32),
        compiler_params=_params(("parallel", "parallel", "parallel")),
        name="out_proj",
    )(o_a, o_b, w_out, x, mod3)


def _peer_q_kernel(x_ref, mod_ref, g_ref, w_ref, keys_ref, h_ref, st_ref, h_scr):
    j = pl.program_id(2)

    @pl.when(j == 0)
    def _():
        h = _norm_mod(x_ref[...], g_ref[...], mod_ref[4:5, :], mod_ref[3:4, :]).astype(BF16)
        h_scr[...] = h
        h_ref[...] = h

    q = jnp.dot(h_scr[...], w_ref[...], preferred_element_type=F32)
    for blk in range(q.shape[1] // PEER_HALF):
        qh = q[:, blk * PEER_HALF:(blk + 1) * PEER_HALF]
        st_ref[blk] = lax.dot_general(keys_ref[blk], qh, (((1,), (1,)), ((), ())),
                                      preferred_element_type=F32, precision=lax.Precision.HIGHEST)


def _peer_q(x1, mod3, g, w_q, keys):
    nbatch, seq, d = x1.shape
    nq = w_q.shape[1]
    tm, tn = _tile(seq, 512), 4 * PEER_HALF
    nhp = nq // PEER_HALF
    nt = seq // tm
    return pl.pallas_call(
        _peer_q_kernel,
        grid=(nbatch, nt, nq // tn),
        in_specs=[pl.BlockSpec((None, tm, d), lambda b, i, j: (b, i, 0)),
                  pl.BlockSpec((None, ADA_CHUNKS, d), lambda b, i, j: (b, 0, 0)),
                  pl.BlockSpec((1, d), lambda b, i, j: (0, 0)),
                  pl.BlockSpec((d, tn), lambda b, i, j: (0, j)),
                  pl.BlockSpec((tn // PEER_HALF, PEER_NKEYS, PEER_HALF), lambda b, i, j: (j, 0, 0))],
        out_specs=[pl.BlockSpec((None, tm, d), lambda b, i, j: (b, i, 0)),
                   pl.BlockSpec((tn // PEER_HALF, PEER_NKEYS, tm), lambda b, i, j: (j, 0, b * nt + i))],
        out_shape=[jax.ShapeDtypeStruct((nbatch, seq, d), BF16),
                   jax.ShapeDtypeStruct((nhp, PEER_NKEYS, nbatch * seq), F32)],
        scratch_shapes=[pltpu.VMEM((tm, d), BF16)],
        compiler_params=_params(("parallel", "parallel", "arbitrary")),
        name="peer_q",
    )(x1, mod3, g.reshape(1, d), w_q, keys)


def _top_rows(x, count, one_at_a_time):
    rows = []
    ridx = lax.broadcasted_iota(jnp.int32, x.shape, 0)
    for _ in range(count):
        m = jnp.max(x, axis=0, keepdims=True)
        rows.append(m)
        hit = x == m
        if one_at_a_time:
            first = jnp.min(jnp.where(hit, ridx, x.shape[0]), axis=0, keepdims=True)
            hit = ridx == first
        x = jnp.where(hit, -jnp.inf, x)
    removed = jnp.sum(jnp.where(x == -jnp.inf, 1.0, 0.0), axis=0, keepdims=True)
    return rows, removed


def _peer_group_stats(x0, x1, one_at_a_time):
    half = PEER_TOPK // 2
    v0, r0 = _top_rows(x0, PEER_TOPK, one_at_a_time)
    v1, r1 = _top_rows(x1, PEER_TOPK, one_at_a_time)
    v0_all = jnp.concatenate(v0, axis=0)
    v1_all = jnp.concatenate(v1, axis=0)
    cand = [v0[0] + v1_all]
    cand += [v0[a] + v1_all[:half] for a in range(1, half)]
    cand += [v0_all[half:] + v1[0]]
    top, rc = _top_rows(jnp.concatenate(cand, axis=0), PEER_TOPK, one_at_a_time)
    z = jnp.zeros_like(top[0])
    for t in top:
        z = z + jnp.exp(t - top[0])
    stats = jnp.concatenate([top[-1], v0[0], v1[0], 1.0 / z] + [jnp.zeros_like(z)] * 4, axis=0)
    repeated = jnp.max(jnp.maximum(jnp.maximum(r0, r1), rc)) > float(PEER_TOPK)
    return stats, repeated


def _peer_topk_kernel(st_ref, stats_ref):
    ngroups = st_ref.shape[-1] // LANES

    def groups(h, one_at_a_time):
        out = []
        for tg in range(ngroups):
            cols = slice(tg * LANES, (tg + 1) * LANES)
            out.append(_peer_group_stats(st_ref[2 * h, :, cols], st_ref[2 * h + 1, :, cols], one_at_a_time))
        return out

    def per_head(h, carry):
        fast = groups(h, False)
        repeated = fast[0][1]
        for tg in range(ngroups):
            stats_ref[h, :, tg * LANES:(tg + 1) * LANES] = fast[tg][0]
            repeated = repeated | fast[tg][1]

        @pl.when(repeated)
        def _():
            for tg, (stats, _) in enumerate(groups(h, True)):
                stats_ref[h, :, tg * LANES:(tg + 1) * LANES] = stats
        return carry

    lax.fori_loop(0, PEER_HEADS, per_head, 0)


def _peer_topk(st):
    nhp, nkeys, ntok = st.shape
    tmk = _tile(ntok, 256)
    return pl.pallas_call(
        _peer_topk_kernel,
        grid=(ntok // tmk,),
        in_specs=[pl.BlockSpec((nhp, nkeys, tmk), lambda i: (0, 0, i))],
        out_specs=pl.BlockSpec((PEER_HEADS, 8, tmk), lambda i: (0, 0, i)),
        out_shape=jax.ShapeDtypeStruct((PEER_HEADS, 8, ntok), F32),
        compiler_params=_params(("parallel",)),
        name="peer_topk",
    )(st)


EXPERT_SPLIT = 2


def _peer_dense_kernel(x_ref, u_ref, v_ref, st_ref, stats_ref, o_ref, e1_scr):
    e = pl.program_id(1)
    te, tm = u_ref.shape[0], x_ref.shape[0]
    ni = te // PEER_NKEYS
    hw = te // EXPERT_SPLIT

    @pl.when(e == 0)
    def _():
        o_ref[...] = jnp.zeros_like(o_ref)
        for h in range(PEER_HEADS):
            e1_scr[h] = jnp.exp(st_ref[2 * h + 1] - stats_ref[h, 2:3, :]) * stats_ref[h, 3:4, :]

    def scores(p):
        return lax.dot_general(u_ref[p * hw:(p + 1) * hw, :], x_ref[...], (((1,), (1,)), ((), ())),
                               preferred_element_type=F32)

    def activation(p, a_t):
        cols = []
        for il in range(p * hw // PEER_NKEYS, (p + 1) * hw // PEER_NKEYS):
            i = e * ni + il
            rows = []
            for tc in range(tm // LANES):
                ts = slice(tc * LANES, (tc + 1) * LANES)
                g = jnp.zeros((PEER_NKEYS, LANES), F32)
                for h in range(PEER_HEADS):
                    s0 = st_ref[2 * h, pl.ds(i, 1), :][:, ts]
                    e0 = jnp.exp(s0 - stats_ref[h, 1:2, ts])
                    sel = (s0 + st_ref[2 * h + 1, :, ts]) >= stats_ref[h, 0:1, ts]
                    g = g + jnp.where(sel, e0 * e1_scr[h, :, ts], 0.0)
                r0 = il * PEER_NKEYS - p * hw
                a = a_t[r0:r0 + PEER_NKEYS, ts]
                act = 0.5 * a * (1.0 + lax.erf(a * (2.0 ** -0.5))) * g
                rows.append(act.T.astype(BF16))
            cols.append(jnp.concatenate(rows, axis=0))
        return jnp.concatenate(cols, axis=1)

    a_parts = [scores(p) for p in range(EXPERT_SPLIT)]
    for p in range(EXPERT_SPLIT):
        o_ref[...] += jnp.dot(activation(p, a_parts[p]), v_ref[p * hw:(p + 1) * hw, :],
                              preferred_element_type=F32)


def _peer_dense(h2, u_tab, v_tab, st, stats):
    ntok, d = h2.shape
    nexp = u_tab.shape[0]
    tm, te = _tile(ntok, 512), 4 * PEER_NKEYS
    nhp = st.shape[0]
    return pl.pallas_call(
        _peer_dense_kernel,
        grid=(ntok // tm, nexp // te),
        in_specs=[pl.BlockSpec((tm, d), lambda i, e: (i, 0)),
                  pl.BlockSpec((te, d), lambda i, e: (e, 0)),
                  pl.BlockSpec((te, d), lambda i, e: (e, 0)),
                  pl.BlockSpec((nhp, PEER_NKEYS, tm), lambda i, e: (0, 0, i)),
                  pl.BlockSpec((PEER_HEADS, 8, tm), lambda i, e: (0, 0, i))],
        out_specs=pl.BlockSpec((tm, d), lambda i, e: (i, 0)),
        out_shape=jax.ShapeDtypeStruct((ntok, d), F32),
        scratch_shapes=[pltpu.VMEM((PEER_HEADS, PEER_NKEYS, tm), F32)],
        compiler_params=_params(("parallel", "arbitrary")),
        name="peer_dense",
    )(h2, u_tab, v_tab, st, stats)


def _final_kernel(x_ref, p_ref, mod_ref, g_ref, o_ref, *, normalize):
    y = x_ref[...] + mod_ref[5:6, :] * p_ref[...]
    if normalize:
        y = (y * lax.rsqrt(jnp.mean(y * y, axis=-1, keepdims=True) + NORM_EPS)) * g_ref[...]
    o_ref[...] = y


def _final(x1, peer_out, mod3, g, normalize):
    nbatch, seq, d = x1.shape
    tm = _tile(seq, 256)
    spec = pl.BlockSpec((None, tm, d), lambda b, i: (b, i, 0))
    return pl.pallas_call(
        functools.partial(_final_kernel, normalize=normalize),
        grid=(nbatch, seq // tm),
        in_specs=[spec, spec,
                  pl.BlockSpec((None, ADA_CHUNKS, d), lambda b, i: (b, 0, 0)),
                  pl.BlockSpec((1, d), lambda b, i: (0, 0))],
        out_specs=spec,
        out_shape=jax.ShapeDtypeStruct((nbatch, seq, d), F32),
        compiler_params=_params(("parallel", "parallel")),
        name="final",
    )(x1, peer_out, mod3, g.reshape(1, d))


def kernel(x, c, positions, ln1_g, ln2_g, w_ada, b_ada, w_in, w_out, peer_wq, peer_sub_keys, peer_u,
           peer_v, lnf_g):
    nbatch, seq, d = x.shape
    depth = w_ada.shape[0]
    tables = _rope_tables(positions)
    for layer in range(depth):
        mod3 = _ada(c, w_ada[layer], b_ada[layer]).reshape(nbatch, ADA_CHUNKS, d)
        pa, proj, aux = _in_proj(x, mod3, ln1_g[layer], _prep_w_in(w_in[layer]), tables)
        o_a = _attn_a(pa)
        o_b, (w_out_b, w_q_b, u_b, v_b) = _dsa(
            proj, aux, (w_out[layer], peer_wq[layer], peer_u[layer], peer_v[layer]))
        x = _out_proj(o_a, o_b, w_out_b, x, mod3)
        keys = peer_sub_keys[layer].reshape(2 * PEER_HEADS, PEER_NKEYS, PEER_HALF)
        h2, st = _peer_q(x, mod3, ln2_g[layer], w_q_b, keys)
        stats = _peer_topk(st)
        peer_out = _peer_dense(h2.reshape(nbatch * seq, d), u_b, v_b, st, stats)
        x = _final(x, peer_out.reshape(nbatch, seq, d), mod3, lnf_g, normalize=layer + 1 == depth)
    return x
```

```python
import functools
import math

import jax
import jax.numpy as jnp
from jax import lax
from jax.experimental import pallas as pl
from jax.experimental.pallas import tpu as pltpu

F32 = jnp.float32
BF16 = jnp.bfloat16

HEAD_DIM = 128
A_HEADS = 16
A_KV_HEADS = 4
A_GROUP = A_HEADS // A_KV_HEADS
DILATED_BRANCHES = ((128, 1), (512, 4), (2048, 16))
WIN_BLK = 128
B_HEADS = 16
IDX_HEADS = 16
IDX_DIM = 64
DSA_TOPK_MAX = 256
Q_BLK = 128
PEER_HEADS = 8
PEER_NKEYS = 128
PEER_HALF = 128
PEER_TOPK = 16
ROPE_THETA = 10000.0
NORM_EPS = 1e-6
NEG = -1e30
ADA_CHUNKS = 6
INT_MIN = -(2 ** 31)

LANES = 128
VMEM_LIMIT = 56 * 1024 * 1024

PROJ_TILE = 512
IN_COLS = ((A_HEADS + 2 * A_KV_HEADS + B_HEADS + 2) * HEAD_DIM + IDX_HEADS * IDX_DIM + IDX_DIM + IDX_HEADS)
PROJ_TILES = -(-IN_COLS // PROJ_TILE)
A_TILES = (A_HEADS + 2 * A_KV_HEADS) * HEAD_DIM // PROJ_TILE
KA_TILE = A_HEADS * HEAD_DIM // PROJ_TILE
VA_TILE = KA_TILE + 1
PA_HEADS = A_HEADS + 2 * A_KV_HEADS
QB_OFF = 0
KB_OFF = QB_OFF + B_HEADS * HEAD_DIM
VB_OFF = KB_OFF + HEAD_DIM
QI_OFF = VB_OFF + HEAD_DIM
KIKI_OFF = QI_OFF + IDX_HEADS * IDX_DIM
P_COLS = (PROJ_TILES - A_TILES) * PROJ_TILE
WI_LANE = IDX_DIM


def _params(semantics):
    return pltpu.CompilerParams(dimension_semantics=semantics, vmem_limit_bytes=VMEM_LIMIT)


def _tile(n, pref):
    return pref if n % pref == 0 else n


def _norm_mod(x, g, scale, shift):
    xf = x.astype(F32)
    y = xf * lax.rsqrt(jnp.mean(xf * xf, axis=-1, keepdims=True) + NORM_EPS)
    return (y * g) * (1.0 + scale) + shift


def _ada_kernel(ct_ref, w_ref, b_ref, o_ref, *, nbatch):
    @pl.when(pl.program_id(1) == 0)
    def _():
        o_ref[...] = jnp.broadcast_to(b_ref[...], o_ref.shape)

    ct = ct_ref[...]
    s = ct * jax.nn.sigmoid(ct)
    w = w_ref[...]
    rows = [jnp.sum(w * s[:, b:b + 1], axis=0, keepdims=True) for b in range(nbatch)]
    o_ref[...] += jnp.concatenate(rows, axis=0)


def _ada(c, w_ada, b_ada):
    nbatch, d = c.shape
    n = w_ada.shape[1]
    tk, tn = _tile(d, 1024), _tile(n, 2048)
    return pl.pallas_call(
        functools.partial(_ada_kernel, nbatch=nbatch),
        grid=(n // tn, d // tk),
        in_specs=[pl.BlockSpec((tk, nbatch), lambda j, k: (k, 0)),
                  pl.BlockSpec((tk, tn), lambda j, k: (k, j)),
                  pl.BlockSpec((1, tn), lambda j, k: (0, j))],
        out_specs=pl.BlockSpec((nbatch, tn), lambda j, k: (0, j)),
        out_shape=jax.ShapeDtypeStruct((nbatch, n), F32),
        compiler_params=_params(("parallel", "arbitrary")),
        name="ada",
    )(c.T, w_ada, b_ada.reshape(1, n))


def _rope_kernel(pos_ref, ch_ref, sh_ref, ci_ref, sa_ref, sb_ref):
    pos = pos_ref[...].astype(F32)
    lane = lax.broadcasted_iota(jnp.int32, (1, LANES), 1)
    expo_h = -((2 * (lane & 63)).astype(F32)) / HEAD_DIM
    expo_i = -((2 * (lane & 31)).astype(F32)) / IDX_DIM
    inv = jnp.power(ROPE_THETA, jnp.where(lane < 64, expo_h, expo_i))
    ang = pos * inv
    c = jnp.cos(ang)
    s = jnp.sin(ang)
    ch_ref[...] = jnp.where(lane < 64, c, pltpu.roll(c, 64, 1))
    sh_ref[...] = jnp.where(lane < 64, -s, pltpu.roll(s, 64, 1))
    grp = lane >> 5
    c64, c96, c32 = pltpu.roll(c, 64, 1), pltpu.roll(c, 96, 1), pltpu.roll(c, 32, 1)
    s64, s96, s32 = pltpu.roll(s, 64, 1), pltpu.roll(s, 96, 1), pltpu.roll(s, 32, 1)
    ci_ref[...] = jnp.where(grp == 0, c64, jnp.where(grp == 1, c96, jnp.where(grp == 2, c, c32)))
    sa_ref[...] = jnp.where(grp == 0, -s64, jnp.where(grp == 2, -s, 0.0))
    sb_ref[...] = jnp.where(grp == 1, s96, jnp.where(grp == 3, s32, 0.0))


def _rope_tables(positions):
    nbatch, seq = positions.shape
    ts = _tile(seq, 1024)
    spec = pl.BlockSpec((None, ts, LANES), lambda b, i: (b, i, 0))
    shape = jax.ShapeDtypeStruct((nbatch, seq, LANES), F32)
    return pl.pallas_call(
        _rope_kernel,
        grid=(nbatch, seq // ts),
        in_specs=[pl.BlockSpec((None, ts, 1), lambda b, i: (b, i, 0))],
        out_specs=[spec] * 5,
        out_shape=[shape] * 5,
        compiler_params=_params(("parallel", "parallel")),
        name="rope",
    )(positions.reshape(nbatch, seq, 1))


def _rope_head(a, ch, sh):
    return a * ch + pltpu.roll(a, 64, 1) * sh


def _rope_idx(a, ci, sa, sb):
    return a * ci + pltpu.roll(a, 96, 1) * sa + pltpu.roll(a, 32, 1) * sb


def _in_proj_kernel(x_ref, mod_ref, g_ref, w_ref, ch_ref, sh_ref, ci_ref, sa_ref, sb_ref,
                    pa_ref, p_ref, aux_ref, h_scr, acc_scr):
    j = pl.program_id(2)

    @pl.when(j == 0)
    def _():
        h_scr[...] = _norm_mod(x_ref[...], g_ref[...], mod_ref[1:2, :], mod_ref[0:1, :]).astype(BF16)

    acc_scr[...] = jnp.dot(h_scr[...], w_ref[...], preferred_element_type=F32)
    nblk = PROJ_TILE // LANES
    qscale = HEAD_DIM ** -0.5 * math.log2(math.e)
    qb_tile = A_TILES + QB_OFF // PROJ_TILE
    mix_tile = A_TILES + KB_OFF // PROJ_TILE
    last_tile = PROJ_TILES - 1

    def blk(q):
        return acc_scr[:, q * LANES:(q + 1) * LANES]

    def put(q, val):
        p_ref[:, q * LANES:(q + 1) * LANES] = val.astype(BF16)

    def head(q):
        return _rope_head(blk(q), ch_ref[...], sh_ref[...])

    def idx(q):
        return _rope_idx(blk(q), ci_ref[...], sa_ref[...], sb_ref[...])

    @pl.when(j < KA_TILE)
    def _():
        for q in range(nblk):
            pa_ref[q] = head(q) * qscale

    @pl.when(j == KA_TILE)
    def _():
        for q in range(nblk):
            pa_ref[q] = head(q)

    @pl.when(j == VA_TILE)
    def _():
        for q in range(nblk):
            pa_ref[q] = blk(q)

    @pl.when((j >= qb_tile) & (j < mix_tile))
    def _():
        for q in range(nblk):
            put(q, head(q) * qscale)

    @pl.when(j == mix_tile)
    def _():
        put(0, head(0))
        put(1, blk(1))
        put(2, idx(2))
        put(3, idx(3))

    @pl.when((j > mix_tile) & (j < last_tile))
    def _():
        for q in range(nblk):
            put(q, idx(q))

    @pl.when(j == last_tile)
    def _():
        put(0, idx(0))
        put(1, idx(1))
        ki = idx(2)
        lane = lax.broadcasted_iota(jnp.int32, ki.shape, 1)
        put(2, jnp.where(lane < IDX_DIM, ki, pltpu.roll(ki, IDX_DIM, 1)))
        put(3, blk(3))
        aux_ref[...] = blk(2)


def _prep_w_in(w_in):
    assert w_in.shape[1] == IN_COLS
    return jnp.pad(w_in.astype(BF16), ((0, 0), (0, PROJ_TILES * PROJ_TILE - IN_COLS)))


def _in_proj(x, mod3, g, w_perm, tables):
    nbatch, seq, d = x.shape
    tm = _tile(seq, 512)
    nblk = PROJ_TILE // LANES
    tab_spec = pl.BlockSpec((None, tm, LANES), lambda b, i, j: (b, i, 0))
    return pl.pallas_call(
        _in_proj_kernel,
        grid=(nbatch, seq // tm, PROJ_TILES),
        in_specs=[pl.BlockSpec((None, tm, d), lambda b, i, j: (b, i, 0)),
                  pl.BlockSpec((None, ADA_CHUNKS, d), lambda b, i, j: (b, 0, 0)),
                  pl.BlockSpec((1, d), lambda b, i, j: (0, 0)),
                  pl.BlockSpec((d, PROJ_TILE), lambda b, i, j: (0, j))] + [tab_spec] * 5,
        out_specs=[pl.BlockSpec((None, nblk, tm, LANES), lambda b, i, j: (b, jnp.minimum(j, A_TILES - 1), i, 0)),
                   pl.BlockSpec((None, tm, PROJ_TILE), lambda b, i, j: (b, i, jnp.maximum(j - A_TILES, 0))),
                   pl.BlockSpec((None, tm, LANES), lambda b, i, j: (b, i, 0))],
        out_shape=[jax.ShapeDtypeStruct((nbatch, PA_HEADS, seq, LANES), F32),
                   jax.ShapeDtypeStruct((nbatch, seq, P_COLS), BF16),
                   jax.ShapeDtypeStruct((nbatch, seq, LANES), F32)],
        scratch_shapes=[pltpu.VMEM((tm, d), BF16), pltpu.VMEM((tm, PROJ_TILE), F32)],
        compiler_params=_params(("parallel", "parallel", "arbitrary")),
        name="in_proj",
    )(x, mod3, g.reshape(1, d), w_perm, *tables)


def _attn_a_kernel(q_ref, k_ref, v_ref, o_ref, acc_scr, m_scr, l_scr, bias_scr, *, dilations):
    half = pl.program_id(2)
    qrows = q_ref.shape[1]
    nblk = qrows // WIN_BLK

    rows = A_GROUP * WIN_BLK
    row = lax.broadcasted_iota(jnp.int32, (rows, 2 * WIN_BLK), 0) & (WIN_BLK - 1)
    col = lax.broadcasted_iota(jnp.int32, (rows, 2 * WIN_BLK), 1)
    band = (col >= row) & (col <= row + WIN_BLK)
    bias_scr[0] = jnp.where(band & (col >= WIN_BLK), 0.0, NEG)
    bias_scr[1] = jnp.where(band, 0.0, NEG)

    def rows_at(start, dil):
        return pl.ds(start, WIN_BLK) if dil == 1 else pl.ds(start, WIN_BLK, stride=dil)

    for idx, dil in enumerate(dilations):
        per_res = nblk // dil
        first, last = idx == 0, idx == len(dilations) - 1

        def body(blk, carry, dil=dil, per_res=per_res, first=first, last=last):
            r = blk // per_res
            n_loc = blk - r * per_res
            n = half * per_res + n_loc
            q0 = r + dil * WIN_BLK * n_loc
            k0 = r + dil * WIN_BLK * n
            kp = r + dil * WIN_BLK * jnp.maximum(n - 1, 0)
            q4 = jnp.concatenate([q_ref[g, rows_at(q0, dil), :] for g in range(A_GROUP)], axis=0).astype(BF16)
            kc = jnp.concatenate([k_ref[rows_at(kp, dil), :], k_ref[rows_at(k0, dil), :]], axis=0).astype(BF16)
            vc = jnp.concatenate([v_ref[rows_at(kp, dil), :], v_ref[rows_at(k0, dil), :]], axis=0).astype(BF16)
            s = lax.dot_general(q4, kc, (((1,), (1,)), ((), ())), preferred_element_type=F32)
            s = s + bias_scr[jnp.minimum(n, 1)]
            m = jnp.max(s, axis=-1, keepdims=True)
            p = jnp.exp2(s - m)
            l = jnp.sum(p, axis=-1, keepdims=True)
            o = jnp.dot(p.astype(BF16), vc, preferred_element_type=F32)
            for g in range(A_GROUP):
                og = o[g * WIN_BLK:(g + 1) * WIN_BLK]
                mg = jnp.broadcast_to(m[g * WIN_BLK:(g + 1) * WIN_BLK], (WIN_BLK, HEAD_DIM))
                lg = jnp.broadcast_to(l[g * WIN_BLK:(g + 1) * WIN_BLK], (WIN_BLK, HEAD_DIM))
                if not first:
                    mp = m_scr[g, rows_at(q0, dil), :]
                    decay = jnp.exp2(-jnp.abs(mg - mp))
                    w_new = jnp.where(mg >= mp, 1.0, decay)
                    w_old = jnp.where(mg >= mp, decay, 1.0)
                    og = acc_scr[g, rows_at(q0, dil), :] * w_old + og * w_new
                    lg = l_scr[g, rows_at(q0, dil), :] * w_old + lg * w_new
                    mg = jnp.maximum(mg, mp)
                if last:
                    o_ref[pl.ds(pl.multiple_of(q0, WIN_BLK), WIN_BLK), g * HEAD_DIM:(g + 1) * HEAD_DIM] = (
                        (og / lg).astype(o_ref.dtype))
                else:
                    acc_scr[g, rows_at(q0, dil), :] = og
                    m_scr[g, rows_at(q0, dil), :] = mg
                    l_scr[g, rows_at(q0, dil), :] = lg
            return carry

        lax.fori_loop(0, nblk, body, 0, unroll=2)


def _attn_a(pa):
    nbatch, _, seq, _ = pa.shape
    dilations = tuple(sorted((dil for _, dil in DILATED_BRANCHES), reverse=True))
    assert dilations[-1] == 1 and all(w // dil == WIN_BLK for w, dil in DILATED_BRANCHES)
    nhalf = 2 if seq % (2 * WIN_BLK * dilations[0]) == 0 else 1
    qrows = seq // nhalf
    assert qrows % (WIN_BLK * dilations[0]) == 0
    return pl.pallas_call(
        functools.partial(_attn_a_kernel, dilations=dilations),
        grid=(nbatch, A_KV_HEADS, nhalf),
        in_specs=[pl.BlockSpec((None, A_GROUP, qrows, HEAD_DIM), lambda b, h, t: (b, h, t, 0)),
                  pl.BlockSpec((None, None, seq, HEAD_DIM), lambda b, h, t: (b, A_HEADS + h, 0, 0)),
                  pl.BlockSpec((None, None, seq, HEAD_DIM),
                               lambda b, h, t: (b, A_HEADS + A_KV_HEADS + h, 0, 0))],
        out_specs=pl.BlockSpec((None, qrows, A_GROUP * HEAD_DIM), lambda b, h, t: (b, t, h)),
        out_shape=jax.ShapeDtypeStruct((nbatch, seq, A_HEADS * HEAD_DIM), BF16),
        scratch_shapes=[pltpu.VMEM((A_GROUP, qrows, HEAD_DIM), F32),
                        pltpu.VMEM((A_GROUP, qrows, HEAD_DIM), F32),
                        pltpu.VMEM((A_GROUP, qrows, HEAD_DIM), F32),
                        pltpu.VMEM((2, A_GROUP * WIN_BLK, 2 * WIN_BLK), F32)],
        compiler_params=_params(("parallel", "parallel", "arbitrary")),
        name="attn_a",
    )(pa, pa, pa)


KEY_CHUNK = 256


def _ordered_to_float(key):
    return pltpu.bitcast(key ^ ((key >> 31) & jnp.int32(0x7FFFFFFF)), F32)


LOWEST_FINITE_KEY = INT_MIN + 0x00800000


def _dsa_kernel(p_ref, wi_ref, k_ref, kk_ref, v_ref, *rest, topk, ncast):
    cast_in, rest = rest[:ncast], rest[ncast:]
    o_ref, cast_out = rest[0], rest[1:1 + ncast]
    sc_scr, wib_scr, q_scr, p_scr, m_scr, l_scr, acc_scr = rest[1 + ncast:]
    i = pl.program_id(1)
    nchunks = (i + 2) // 2
    halves = KEY_CHUNK // LANES

    for src, dst in zip(cast_in, cast_out):
        dst[...] = src[...].astype(dst.dtype)

    wi = wi_ref[...] * (IDX_HEADS ** -0.5 * IDX_DIM ** -0.5)
    lane = lax.broadcasted_iota(jnp.int32, (Q_BLK, LANES), 1)
    for h in range(IDX_HEADS):
        blk = p_ref[:, QI_OFF + (h // 2) * LANES:QI_OFF + (h // 2 + 1) * LANES]
        keep = (lane < IDX_DIM) if h % 2 == 0 else (lane >= IDX_DIM)
        q_scr[h * Q_BLK:(h + 1) * Q_BLK, :] = jnp.where(keep, blk, jnp.zeros_like(blk))
        wib_scr[h] = jnp.broadcast_to(wi[:, WI_LANE + h:WI_LANE + h + 1], (Q_BLK, LANES))
    qpos = i * Q_BLK + lax.broadcasted_iota(jnp.int32, (Q_BLK, KEY_CHUNK), 0)
    kiota = lax.broadcasted_iota(jnp.int32, (Q_BLK, KEY_CHUNK), 1)

    def score_chunk(c, carry):
        k0 = pl.multiple_of(c * KEY_CHUNK, KEY_CHUNK)
        z = lax.dot_general(q_scr[...], kk_ref[pl.ds(k0, KEY_CHUNK), :], (((1,), (1,)), ((), ())),
                            preferred_element_type=F32)
        sc = jnp.zeros((Q_BLK, KEY_CHUNK), F32)
        for h in range(IDX_HEADS):
            w = wib_scr[h]
            sc = sc + jnp.concatenate([w] * halves, axis=1) * jnp.maximum(z[h * Q_BLK:(h + 1) * Q_BLK], 0.0)
        sc_scr[c] = jnp.where(k0 + kiota <= qpos, sc, -jnp.inf)
        return carry

    lax.fori_loop(0, nchunks, score_chunk, 0)

    def bit_body(t, ans):
        trial = ans | jnp.left_shift(jnp.int32(1), 31 - t)
        thr = _ordered_to_float(trial ^ jnp.int32(INT_MIN))

        def count_chunk(c, acc):
            sc = sc_scr[c]
            for q in range(halves):
                acc = acc + jnp.where(sc[:, q * LANES:(q + 1) * LANES] >= thr, 1.0, 0.0)
            return acc

        acc = lax.fori_loop(0, nchunks, count_chunk, jnp.zeros((Q_BLK, LANES), F32))
        cnt = jnp.sum(acc, axis=1, keepdims=True)
        return jnp.where(cnt >= float(topk), trial, ans)

    ans = lax.fori_loop(0, 32, bit_body, jnp.zeros((Q_BLK, LANES), jnp.int32))
    tau = _ordered_to_float(jnp.maximum(ans ^ jnp.int32(INT_MIN), jnp.int32(LOWEST_FINITE_KEY)))
    tau2 = jnp.concatenate([tau] * halves, axis=1)

    for h in range(B_HEADS):
        q_scr[h * Q_BLK:(h + 1) * Q_BLK, :] = p_ref[:, QB_OFF + h * HEAD_DIM:QB_OFF + (h + 1) * HEAD_DIM]
    m_scr[...] = jnp.full(m_scr.shape, NEG, F32)
    l_scr[...] = jnp.zeros(l_scr.shape, F32)
    acc_scr[...] = jnp.zeros(acc_scr.shape, F32)

    def att_chunk(c, carry):
        k0 = pl.multiple_of(c * KEY_CHUNK, KEY_CHUNK)
        s = lax.dot_general(q_scr[...], k_ref[pl.ds(k0, KEY_CHUNK), :], (((1,), (1,)), ((), ())),
                            preferred_element_type=F32)
        bias = jnp.where(sc_scr[c] >= tau2, 0.0, NEG)
        for h in range(B_HEADS):
            hr = slice(h * Q_BLK, (h + 1) * Q_BLK)
            sh = s[hr] + bias
            m_old = m_scr[hr, :]
            m_new = jnp.maximum(m_old, jnp.max(sh, axis=-1, keepdims=True))
            alpha = jnp.exp2(m_old - m_new)
            p = jnp.exp2(sh - jnp.concatenate([m_new] * halves, axis=1))
            psum = p[:, :LANES]
            for q in range(1, halves):
                psum = psum + p[:, q * LANES:(q + 1) * LANES]
            l_scr[hr, :] = alpha * l_scr[hr, :] + psum
            m_scr[hr, :] = m_new
            p_scr[hr, :] = p.astype(BF16)
            acc_scr[hr, :] = alpha * acc_scr[hr, :]
        acc_scr[...] += jnp.dot(p_scr[...], v_ref[pl.ds(k0, KEY_CHUNK), :], preferred_element_type=F32)
        return carry

    lax.fori_loop(0, nchunks, att_chunk, 0)
    for h in range(B_HEADS):
        hr = slice(h * Q_BLK, (h + 1) * Q_BLK)
        l = jnp.sum(l_scr[hr, :], axis=-1, keepdims=True)
        o_ref[:, h * HEAD_DIM:(h + 1) * HEAD_DIM] = (acc_scr[hr, :] / l).astype(o_ref.dtype)


def _dsa(proj, aux, f32_weights):
    nbatch, seq, _ = proj.shape
    assert seq % KEY_CHUNK == 0
    topk = min(DSA_TOPK_MAX, seq // 4)
    qw = B_HEADS * HEAD_DIM
    nq = seq // Q_BLK
    steps = nbatch * nq
    full = lambda off: pl.BlockSpec((None, seq, LANES), lambda b, i: (b, 0, off // LANES))
    cast_specs = []
    for w in f32_weights:
        assert w.shape[0] % steps == 0
        cast_specs.append(pl.BlockSpec((w.shape[0] // steps, w.shape[1]), lambda b, i: (b * nq + i, 0)))
    outs = pl.pallas_call(
        functools.partial(_dsa_kernel, topk=topk, ncast=len(f32_weights)),
        grid=(nbatch, nq),
        in_specs=[pl.BlockSpec((None, Q_BLK, P_COLS), lambda b, i: (b, i, 0)),
                  pl.BlockSpec((None, Q_BLK, LANES), lambda b, i: (b, i, 0)),
                  full(KB_OFF), full(KIKI_OFF), full(VB_OFF)] + cast_specs,
        out_specs=[pl.BlockSpec((None, Q_BLK, qw), lambda b, i: (b, i, 0))] + cast_specs,
        out_shape=[jax.ShapeDtypeStruct((nbatch, seq, qw), BF16)]
        + [jax.ShapeDtypeStruct(w.shape, BF16) for w in f32_weights],
        scratch_shapes=[pltpu.VMEM((seq // KEY_CHUNK, Q_BLK, KEY_CHUNK), F32),
                        pltpu.VMEM((IDX_HEADS, Q_BLK, LANES), F32),
                        pltpu.VMEM((B_HEADS * Q_BLK, HEAD_DIM), BF16),
                        pltpu.VMEM((B_HEADS * Q_BLK, KEY_CHUNK), BF16),
                        pltpu.VMEM((B_HEADS * Q_BLK, LANES), F32),
                        pltpu.VMEM((B_HEADS * Q_BLK, LANES), F32),
                        pltpu.VMEM((B_HEADS * Q_BLK, HEAD_DIM), F32)],
        compiler_params=_params(("parallel", "arbitrary")),
        name="dsa",
    )(proj, aux, proj, proj, proj, *f32_weights)
    return outs[0], outs[1:]


def _out_proj_kernel(oa_ref, ob_ref, w_ref, x_ref, mod_ref, o_ref):
    ka = oa_ref.shape[-1]
    y = jnp.dot(oa_ref[...], w_ref[:ka, :], preferred_element_type=F32)
    y = y + jnp.dot(ob_ref[...], w_ref[ka:, :], preferred_element_type=F32)
    o_ref[...] = x_ref[...] + mod_ref[2:3, :] * y


def _out_proj(o_a, o_b, w_out, x, mod3):
    nbatch, seq, d = x.shape
    ka, kb = o_a.shape[-1], o_b.shape[-1]
    tm, tn = _tile(seq, 1024), _tile(d, 512)
    return pl.pallas_call(
        _out_proj_kernel,
        grid=(nbatch, seq // tm, d // tn),
        in_specs=[pl.BlockSpec((None, tm, ka), lambda b, i, j: (b, i, 0)),
                  pl.BlockSpec((None, tm, kb), lambda b, i, j: (b, i, 0)),
                  pl.BlockSpec((ka + kb, tn), lambda b, i, j: (0, j)),
                  pl.BlockSpec((None, tm, tn), lambda b, i, j: (b, i, j)),
                  pl.BlockSpec((None, ADA_CHUNKS, tn), lambda b, i, j: (b, 0, j))],
        out_specs=pl.BlockSpec((None, tm, tn), lambda b, i, j: (b, i, j)),
        out_shape=jax.ShapeDtypeStruct((nbatch, seq, d), F32),
        compiler_params=_params(("parallel", "parallel", "parallel")),
        name="out_proj",
    )(o_a, o_b, w_out, x, mod3)


def _peer_q_kernel(x_ref, mod_ref, g_ref, w_ref, keys_ref, h_ref, st_ref, h_scr):
    j = pl.program_id(2)

    @pl.when(j == 0)
    def _():
        h = _norm_mod(x_ref[...], g_ref[...], mod_ref[4:5, :], mod_ref[3:4, :]).astype(BF16)
        h_scr[...] = h
        h_ref[...] = h

    q = jnp.dot(h_scr[...], w_ref[...], preferred_element_type=F32)
    for blk in range(q.shape[1] // PEER_HALF):
        qh = q[:, blk * PEER_HALF:(blk + 1) * PEER_HALF]
        st_ref[blk] = lax.dot_general(keys_ref[blk], qh, (((1,), (1,)), ((), ())),
                                      preferred_element_type=F32, precision=lax.Precision.HIGHEST)


def _peer_q(x1, mod3, g, w_q, keys):
    nbatch, seq, d = x1.shape
    nq = w_q.shape[1]
    tm, tn = _tile(seq, 512), 4 * PEER_HALF
    nhp = nq // PEER_HALF
    nt = seq // tm
    return pl.pallas_call(
        _peer_q_kernel,
        grid=(nbatch, nt, nq // tn),
        in_specs=[pl.BlockSpec((None, tm, d), lambda b, i, j: (b, i, 0)),
                  pl.BlockSpec((None, ADA_CHUNKS, d), lambda b, i, j: (b, 0, 0)),
                  pl.BlockSpec((1, d), lambda b, i, j: (0, 0)),
                  pl.BlockSpec((d, tn), lambda b, i, j: (0, j)),
                  pl.BlockSpec((tn // PEER_HALF, PEER_NKEYS, PEER_HALF), lambda b, i, j: (j, 0, 0))],
        out_specs=[pl.BlockSpec((None, tm, d), lambda b, i, j: (b, i, 0)),
                   pl.BlockSpec((tn // PEER_HALF, PEER_NKEYS, tm), lambda b, i, j: (j, 0, b * nt + i))],
        out_shape=[jax.ShapeDtypeStruct((nbatch, seq, d), BF16),
                   jax.ShapeDtypeStruct((nhp, PEER_NKEYS, nbatch * seq), F32)],
        scratch_shapes=[pltpu.VMEM((tm, d), BF16)],
        compiler_params=_params(("parallel", "parallel", "arbitrary")),
        name="peer_q",
    )(x1, mod3, g.reshape(1, d), w_q, keys)


def _top_rows(x, count, one_at_a_time):
    rows = []
    ridx = lax.broadcasted_iota(jnp.int32, x.shape, 0)
    for _ in range(count):
        m = jnp.max(x, axis=0, keepdims=True)
        rows.append(m)
        hit = x == m
        if one_at_a_time:
            first = jnp.min(jnp.where(hit, ridx, x.shape[0]), axis=0, keepdims=True)
            hit = ridx == first
        x = jnp.where(hit, -jnp.inf, x)
    removed = jnp.sum(jnp.where(x == -jnp.inf, 1.0, 0.0), axis=0, keepdims=True)
    return rows, removed


def _peer_group_stats(x0, x1, one_at_a_time):
    half = PEER_TOPK // 2
    v0, r0 = _top_rows(x0, PEER_TOPK, one_at_a_time)
    v1, r1 = _top_rows(x1, PEER_TOPK, one_at_a_time)
    v0_all = jnp.concatenate(v0, axis=0)
    v1_all = jnp.concatenate(v1, axis=0)
    cand = [v0[0] + v1_all]
    cand += [v0[a] + v1_all[:half] for a in range(1, half)]
    cand += [v0_all[half:] + v1[0]]
    top, rc = _top_rows(jnp.concatenate(cand, axis=0), PEER_TOPK, one_at_a_time)
    z = jnp.zeros_like(top[0])
    for t in top:
        z = z + jnp.exp(t - top[0])
    stats = jnp.concatenate([top[-1], v0[0], v1[0], 1.0 / z] + [jnp.zeros_like(z)] * 4, axis=0)
    repeated = jnp.max(jnp.maximum(jnp.maximum(r0, r1), rc)) > float(PEER_TOPK)
    return stats, repeated


def _peer_topk_kernel(st_ref, stats_ref):
    ngroups = st_ref.shape[-1] // LANES

    def groups(h, one_at_a_time):
        out = []
        for tg in range(ngroups):
            cols = slice(tg * LANES, (tg + 1) * LANES)
            out.append(_peer_group_stats(st_ref[2 * h, :, cols], st_ref[2 * h + 1, :, cols], one_at_a_time))
        return out

    def per_head(h, carry):
        fast = groups(h, False)
        repeated = fast[0][1]
        for tg in range(ngroups):
            stats_ref[h, :, tg * LANES:(tg + 1) * LANES] = fast[tg][0]
            repeated = repeated | fast[tg][1]

        @pl.when(repeated)
        def _():
            for tg, (stats, _) in enumerate(groups(h, True)):
                stats_ref[h, :, tg * LANES:(tg + 1) * LANES] = stats
        return carry

    lax.fori_loop(0, PEER_HEADS, per_head, 0)


def _peer_topk(st):
    nhp, nkeys, ntok = st.shape
    tmk = _tile(ntok, 256)
    return pl.pallas_call(
        _peer_topk_kernel,
        grid=(ntok // tmk,),
        in_specs=[pl.BlockSpec((nhp, nkeys, tmk), lambda i: (0, 0, i))],
        out_specs=pl.BlockSpec((PEER_HEADS, 8, tmk), lambda i: (0, 0, i)),
        out_shape=jax.ShapeDtypeStruct((PEER_HEADS, 8, ntok), F32),
        compiler_params=_params(("parallel",)),
        name="peer_topk",
    )(st)


EXPERT_SPLIT = 2


def _peer_dense_kernel(x_ref, u_ref, v_ref, st_ref, stats_ref, o_ref, e1_scr):
    e = pl.program_id(1)
    te, tm = u_ref.shape[0], x_ref.shape[0]
    ni = te // PEER_NKEYS
    hw = te // EXPERT_SPLIT

    @pl.when(e == 0)
    def _():
        o_ref[...] = jnp.zeros_like(o_ref)
        for h in range(PEER_HEADS):
            e1_scr[h] = jnp.exp(st_ref[2 * h + 1] - stats_ref[h, 2:3, :]) * stats_ref[h, 3:4, :]

    def scores(p):
        return lax.dot_general(u_ref[p * hw:(p + 1) * hw, :], x_ref[...], (((1,), (1,)), ((), ())),
                               preferred_element_type=F32)

    def activation(p, a_t):
        cols = []
        for il in range(p * hw // PEER_NKEYS, (p + 1) * hw // PEER_NKEYS):
            i = e * ni + il
            rows = []
            for tc in range(tm // LANES):
                ts = slice(tc * LANES, (tc + 1) * LANES)
                g = jnp.zeros((PEER_NKEYS, LANES), F32)
                for h in range(PEER_HEADS):
                    s0 = st_ref[2 * h, pl.ds(i, 1), :][:, ts]
                    e0 = jnp.exp(s0 - stats_ref[h, 1:2, ts])
                    sel = (s0 + st_ref[2 * h + 1, :, ts]) >= stats_ref[h, 0:1, ts]
                    g = g + jnp.where(sel, e0 * e1_scr[h, :, ts], 0.0)
                r0 = il * PEER_NKEYS - p * hw
                a = a_t[r0:r0 + PEER_NKEYS, ts]
                act = 0.5 * a * (1.0 + lax.erf(a * (2.0 ** -0.5))) * g
                rows.append(act.T.astype(BF16))
            cols.append(jnp.concatenate(rows, axis=0))
        return jnp.concatenate(cols, axis=1)

    a_parts = [scores(p) for p in range(EXPERT_SPLIT)]
    for p in range(EXPERT_SPLIT):
        o_ref[...] += jnp.dot(activation(p, a_parts[p]), v_ref[p * hw:(p + 1) * hw, :],
                              preferred_element_type=F32)


def _peer_dense(h2, u_tab, v_tab, st, stats):
    ntok, d = h2.shape
    nexp = u_tab.shape[0]
    tm, te = _tile(ntok, 512), 4 * PEER_NKEYS
    nhp = st.shape[0]
    return pl.pallas_call(
        _peer_dense_kernel,
        grid=(ntok // tm, nexp // te),
        in_specs=[pl.BlockSpec((tm, d), lambda i, e: (i, 0)),
                  pl.BlockSpec((te, d), lambda i, e: (e, 0)),
                  pl.BlockSpec((te, d), lambda i, e: (e, 0)),
                  pl.BlockSpec((nhp, PEER_NKEYS, tm), lambda i, e: (0, 0, i)),
                  pl.BlockSpec((PEER_HEADS, 8, tm), lambda i, e: (0, 0, i))],
        out_specs=pl.BlockSpec((tm, d), lambda i, e: (i, 0)),
        out_shape=jax.ShapeDtypeStruct((ntok, d), F32),
        scratch_shapes=[pltpu.VMEM((PEER_HEADS, PEER_NKEYS, tm), F32)],
        compiler_params=_params(("parallel", "arbitrary")),
        name="peer_dense",
    )(h2, u_tab, v_tab, st, stats)


def _final_kernel(x_ref, p_ref, mod_ref, g_ref, o_ref, *, normalize):
    y = x_ref[...] + mod_ref[5:6, :] * p_ref[...]
    if normalize:
        y = (y * lax.rsqrt(jnp.mean(y * y, axis=-1, keepdims=True) + NORM_EPS)) * g_ref[...]
    o_ref[...] = y


def _final(x1, peer_out, mod3, g, normalize):
    nbatch, seq, d = x1.shape
    tm = _tile(seq, 256)
    spec = pl.BlockSpec((None, tm, d), lambda b, i: (b, i, 0))
    return pl.pallas_call(
        functools.partial(_final_kernel, normalize=normalize),
        grid=(nbatch, seq // tm),
        in_specs=[spec, spec,
                  pl.BlockSpec((None, ADA_CHUNKS, d), lambda b, i: (b, 0, 0)),
                  pl.BlockSpec((1, d), lambda b, i: (0, 0))],
        out_specs=spec,
        out_shape=jax.ShapeDtypeStruct((nbatch, seq, d), F32),
        compiler_params=_params(("parallel", "parallel")),
        name="final",
    )(x1, peer_out, mod3, g.reshape(1, d))


def kernel(x, c, positions, ln1_g, ln2_g, w_ada, b_ada, w_in, w_out, peer_wq, peer_sub_keys, peer_u,
           peer_v, lnf_g):
    nbatch, seq, d = x.shape
    depth = w_ada.shape[0]
    tables = _rope_tables(positions)
    for layer in range(depth):
        mod3 = _ada(c, w_ada[layer], b_ada[layer]).reshape(nbatch, ADA_CHUNKS, d)
        pa, proj, aux = _in_proj(x, mod3, ln1_g[layer], _prep_w_in(w_in[layer]), tables)
        o_a = _attn_a(pa)
        o_b, (w_out_b, w_q_b, u_b, v_b) = _dsa(
            proj, aux, (w_out[layer], peer_wq[layer], peer_u[layer], peer_v[layer]))
        x = _out_proj(o_a, o_b, w_out_b, x, mod3)
        keys = peer_sub_keys[layer].reshape(2 * PEER_HEADS, PEER_NKEYS, PEER_HALF)
        h2, st = _peer_q(x, mod3, ln2_g[layer], w_q_b, keys)
        stats = _peer_topk(st)
        peer_out = _peer_dense(h2.reshape(nbatch * seq, d), u_b, v_b, st, stats)
        x = _final(x, peer_out.reshape(nbatch, seq, d), mod3, lnf_g, normalize=layer + 1 == depth)
    return x
```

```python
import functools
import math

import jax
import jax.numpy as jnp
from jax import lax
from jax.experimental import pallas as pl
from jax.experimental.pallas import tpu as pltpu

F32 = jnp.float32
BF16 = jnp.bfloat16

HEAD_DIM = 128
A_HEADS = 16
A_KV_HEADS = 4
A_GROUP = A_HEADS // A_KV_HEADS
DILATED_BRANCHES = ((128, 1), (512, 4), (2048, 16))
WIN_BLK = 128
B_HEADS = 16
IDX_HEADS = 16
IDX_DIM = 64
DSA_TOPK_MAX = 256
Q_BLK = 128
PEER_HEADS = 8
PEER_NKEYS = 128
PEER_HALF = 128
PEER_TOPK = 16
ROPE_THETA = 10000.0
NORM_EPS = 1e-6
NEG = -1e30
ADA_CHUNKS = 6
INT_MIN = -(2 ** 31)

LANES = 128
VMEM_LIMIT = 56 * 1024 * 1024

PROJ_TILE = 512
IN_COLS = ((A_HEADS + 2 * A_KV_HEADS + B_HEADS + 2) * HEAD_DIM + IDX_HEADS * IDX_DIM + IDX_DIM + IDX_HEADS)
PROJ_TILES = -(-IN_COLS // PROJ_TILE)
A_TILES = (A_HEADS + 2 * A_KV_HEADS) * HEAD_DIM // PROJ_TILE
KA_TILE = A_HEADS * HEAD_DIM // PROJ_TILE
VA_TILE = KA_TILE + 1
PA_HEADS = A_HEADS + 2 * A_KV_HEADS
QB_OFF = 0
KB_OFF = QB_OFF + B_HEADS * HEAD_DIM
VB_OFF = KB_OFF + HEAD_DIM
QI_OFF = VB_OFF + HEAD_DIM
KIKI_OFF = QI_OFF + IDX_HEADS * IDX_DIM
P_COLS = (PROJ_TILES - A_TILES) * PROJ_TILE
WI_LANE = IDX_DIM


def _params(semantics):
    return pltpu.CompilerParams(dimension_semantics=semantics, vmem_limit_bytes=VMEM_LIMIT)


def _tile(n, pref):
    return pref if n % pref == 0 else n


def _norm_mod(x, g, scale, shift):
    xf = x.astype(F32)
    y = xf * lax.rsqrt(jnp.mean(xf * xf, axis=-1, keepdims=True) + NORM_EPS)
    return (y * g) * (1.0 + scale) + shift


def _ada_kernel(ct_ref, w_ref, b_ref, o_ref, *, nbatch):
    @pl.when(pl.program_id(1) == 0)
    def _():
        o_ref[...] = jnp.broadcast_to(b_ref[...], o_ref.shape)

    ct = ct_ref[...]
    s = ct * jax.nn.sigmoid(ct)
    w = w_ref[...]
    rows = [jnp.sum(w * s[:, b:b + 1], axis=0, keepdims=True) for b in range(nbatch)]
    o_ref[...] += jnp.concatenate(rows, axis=0)


def _ada(c, w_ada, b_ada):
    nbatch, d = c.shape
    n = w_ada.shape[1]
    tk, tn = _tile(d, 1024), _tile(n, 2048)
    return pl.pallas_call(
        functools.partial(_ada_kernel, nbatch=nbatch),
        grid=(n // tn, d // tk),
        in_specs=[pl.BlockSpec((tk, nbatch), lambda j, k: (k, 0)),
                  pl.BlockSpec((tk, tn), lambda j, k: (k, j)),
                  pl.BlockSpec((1, tn), lambda j, k: (0, j))],
        out_specs=pl.BlockSpec((nbatch, tn), lambda j, k: (0, j)),
        out_shape=jax.ShapeDtypeStruct((nbatch, n), F32),
        compiler_params=_params(("parallel", "arbitrary")),
        name="ada",
    )(c.T, w_ada, b_ada.reshape(1, n))


def _rope_kernel(pos_ref, ch_ref, sh_ref, ci_ref, sa_ref, sb_ref):
    pos = pos_ref[...].astype(F32)
    lane = lax.broadcasted_iota(jnp.int32, (1, LANES), 1)
    expo_h = -((2 * (lane & 63)).astype(F32)) / HEAD_DIM
    expo_i = -((2 * (lane & 31)).astype(F32)) / IDX_DIM
    inv = jnp.power(ROPE_THETA, jnp.where(lane < 64, expo_h, expo_i))
    ang = pos * inv
    c = jnp.cos(ang)
    s = jnp.sin(ang)
    ch_ref[...] = jnp.where(lane < 64, c, pltpu.roll(c, 64, 1))
    sh_ref[...] = jnp.where(lane < 64, -s, pltpu.roll(s, 64, 1))
    grp = lane >> 5
    c64, c96, c32 = pltpu.roll(c, 64, 1), pltpu.roll(c, 96, 1), pltpu.roll(c, 32, 1)
    s64, s96, s32 = pltpu.roll(s, 64, 1), pltpu.roll(s, 96, 1), pltpu.roll(s, 32, 1)
    ci_ref[...] = jnp.where(grp == 0, c64, jnp.where(grp == 1, c96, jnp.where(grp == 2, c, c32)))
    sa_ref[...] = jnp.where(grp == 0, -s64, jnp.where(grp == 2, -s, 0.0))
    sb_ref[...] = jnp.where(grp == 1, s96, jnp.where(grp == 3, s32, 0.0))


def _rope_tables(positions):
    nbatch, seq = positions.shape
    ts = _tile(seq, 1024)
    spec = pl.BlockSpec((None, ts, LANES), lambda b, i: (b, i, 0))
    shape = jax.ShapeDtypeStruct((nbatch, seq, LANES), F32)
    return pl.pallas_call(
        _rope_kernel,
        grid=(nbatch, seq // ts),
        in_specs=[pl.BlockSpec((None, ts, 1), lambda b, i: (b, i, 0))],
        out_specs=[spec] * 5,
        out_shape=[shape] * 5,
        compiler_params=_params(("parallel", "parallel")),
        name="rope",
    )(positions.reshape(nbatch, seq, 1))


def _rope_head(a, ch, sh):
    return a * ch + pltpu.roll(a, 64, 1) * sh


def _rope_idx(a, ci, sa, sb):
    return a * ci + pltpu.roll(a, 96, 1) * sa + pltpu.roll(a, 32, 1) * sb


def _in_proj_kernel(x_ref, mod_ref, g_ref, w_ref, ch_ref, sh_ref, ci_ref, sa_ref, sb_ref,
                    pa_ref, p_ref, aux_ref, h_scr, acc_scr):
    j = pl.program_id(2)

    @pl.when(j == 0)
    def _():
        h_scr[...] = _norm_mod(x_ref[...], g_ref[...], mod_ref[1:2, :], mod_ref[0:1, :]).astype(BF16)

    acc_scr[...] = jnp.dot(h_scr[...], w_ref[...], preferred_element_type=F32)
    nblk = PROJ_TILE // LANES
    qscale = HEAD_DIM ** -0.5 * math.log2(math.e)
    qb_tile = A_TILES + QB_OFF // PROJ_TILE
    mix_tile = A_TILES + KB_OFF // PROJ_TILE
    last_tile = PROJ_TILES - 1

    def blk(q):
        return acc_scr[:, q * LANES:(q + 1) * LANES]

    def put(q, val):
        p_ref[:, q * LANES:(q + 1) * LANES] = val.astype(BF16)

    def head(q):
        return _rope_head(blk(q), ch_ref[...], sh_ref[...])

    def idx(q):
        return _rope_idx(blk(q), ci_ref[...], sa_ref[...], sb_ref[...])

    @pl.when(j < KA_TILE)
    def _():
        for q in range(nblk):
            pa_ref[q] = head(q) * qscale

    @pl.when(j == KA_TILE)
    def _():
        for q in range(nblk):
            pa_ref[q] = head(q)

    @pl.when(j == VA_TILE)
    def _():
        for q in range(nblk):
            pa_ref[q] = blk(q)

    @pl.when((j >= qb_tile) & (j < mix_tile))
    def _():
        for q in range(nblk):
            put(q, head(q) * qscale)

    @pl.when(j == mix_tile)
    def _():
        put(0, head(0))
        put(1, blk(1))
        put(2, idx(2))
        put(3, idx(3))

    @pl.when((j > mix_tile) & (j < last_tile))
    def _():
        for q in range(nblk):
            put(q, idx(q))

    @pl.when(j == last_tile)
    def _():
        put(0, idx(0))
        put(1, idx(1))
        ki = idx(2)
        lane = lax.broadcasted_iota(jnp.int32, ki.shape, 1)
        put(2, jnp.where(lane < IDX_DIM, ki, pltpu.roll(ki, IDX_DIM, 1)))
        put(3, blk(3))
        aux_ref[...] = blk(2)


def _prep_w_in(w_in):
    assert w_in.shape[1] == IN_COLS
    return jnp.pad(w_in.astype(BF16), ((0, 0), (0, PROJ_TILES * PROJ_TILE - IN_COLS)))


def _in_proj(x, mod3, g, w_perm, tables):
    nbatch, seq, d = x.shape
    tm = _tile(seq, 512)
    nblk = PROJ_TILE // LANES
    tab_spec = pl.BlockSpec((None, tm, LANES), lambda b, i, j: (b, i, 0))
    return pl.pallas_call(
        _in_proj_kernel,
        grid=(nbatch, seq // tm, PROJ_TILES),
        in_specs=[pl.BlockSpec((None, tm, d), lambda b, i, j: (b, i, 0)),
                  pl.BlockSpec((None, ADA_CHUNKS, d), lambda b, i, j: (b, 0, 0)),
                  pl.BlockSpec((1, d), lambda b, i, j: (0, 0)),
                  pl.BlockSpec((d, PROJ_TILE), lambda b, i, j: (0, j))] + [tab_spec] * 5,
        out_specs=[pl.BlockSpec((None, nblk, tm, LANES), lambda b, i, j: (b, jnp.minimum(j, A_TILES - 1), i, 0)),
                   pl.BlockSpec((None, tm, PROJ_TILE), lambda b, i, j: (b, i, jnp.maximum(j - A_TILES, 0))),
                   pl.BlockSpec((None, tm, LANES), lambda b, i, j: (b, i, 0))],
        out_shape=[jax.ShapeDtypeStruct((nbatch, PA_HEADS, seq, LANES), F32),
                   jax.ShapeDtypeStruct((nbatch, seq, P_COLS), BF16),
                   jax.ShapeDtypeStruct((nbatch, seq, LANES), F32)],
        scratch_shapes=[pltpu.VMEM((tm, d), BF16), pltpu.VMEM((tm, PROJ_TILE), F32)],
        compiler_params=_params(("parallel", "parallel", "arbitrary")),
        name="in_proj",
    )(x, mod3, g.reshape(1, d), w_perm, *tables)


def _attn_a_kernel(q_ref, k_ref, v_ref, o_ref, acc_scr, m_scr, l_scr, bias_scr, *, dilations):
    half = pl.program_id(2)
    qrows = q_ref.shape[1]
    nblk = qrows // WIN_BLK

    rows = A_GROUP * WIN_BLK
    row = lax.broadcasted_iota(jnp.int32, (rows, 2 * WIN_BLK), 0) & (WIN_BLK - 1)
    col = lax.broadcasted_iota(jnp.int32, (rows, 2 * WIN_BLK), 1)
    band = (col >= row) & (col <= row + WIN_BLK)
    bias_scr[0] = jnp.where(band & (col >= WIN_BLK), 0.0, NEG)
    bias_scr[1] = jnp.where(band, 0.0, NEG)

    def rows_at(start, dil):
        return pl.ds(start, WIN_BLK) if dil == 1 else pl.ds(start, WIN_BLK, stride=dil)

    for idx, dil in enumerate(dilations):
        per_res = nblk // dil
        first, last = idx == 0, idx == len(dilations) - 1

        def body(blk, carry, dil=dil, per_res=per_res, first=first, last=last):
            r = blk // per_res
            n_loc = blk - r * per_res
            n = half * per_res + n_loc
            q0 = r + dil * WIN_BLK * n_loc
            k0 = r + dil * WIN_BLK * n
            kp = r + dil * WIN_BLK * jnp.maximum(n - 1, 0)
            q4 = jnp.concatenate([q_ref[g, rows_at(q0, dil), :] for g in range(A_GROUP)], axis=0).astype(BF16)
            kc = jnp.concatenate([k_ref[rows_at(kp, dil), :], k_ref[rows_at(k0, dil), :]], axis=0).astype(BF16)
            vc = jnp.concatenate([v_ref[rows_at(kp, dil), :], v_ref[rows_at(k0, dil), :]], axis=0).astype(BF16)
            s = lax.dot_general(q4, kc, (((1,), (1,)), ((), ())), preferred_element_type=F32)
            s = s + bias_scr[jnp.minimum(n, 1)]
            m = jnp.max(s, axis=-1, keepdims=True)
            p = jnp.exp2(s - m)
            l = jnp.sum(p, axis=-1, keepdims=True)
            o = jnp.dot(p.astype(BF16), vc, preferred_element_type=F32)
            for g in range(A_GROUP):
                og = o[g * WIN_BLK:(g + 1) * WIN_BLK]
                mg = jnp.broadcast_to(m[g * WIN_BLK:(g + 1) * WIN_BLK], (WIN_BLK, HEAD_DIM))
                lg = jnp.broadcast_to(l[g * WIN_BLK:(g + 1) * WIN_BLK], (WIN_BLK, HEAD_DIM))
                if not first:
                    mp = m_scr[g, rows_at(q0, dil), :]
                    decay = jnp.exp2(-jnp.abs(mg - mp))
                    w_new = jnp.where(mg >= mp, 1.0, decay)
                    w_old = jnp.where(mg >= mp, decay, 1.0)
                    og = acc_scr[g, rows_at(q0, dil), :] * w_old + og * w_new
                    lg = l_scr[g, rows_at(q0, dil), :] * w_old + lg * w_new
                    mg = jnp.maximum(mg, mp)
                if last:
                    o_ref[pl.ds(pl.multiple_of(q0, WIN_BLK), WIN_BLK), g * HEAD_DIM:(g + 1) * HEAD_DIM] = (
                        (og / lg).astype(o_ref.dtype))
                else:
                    acc_scr[g, rows_at(q0, dil), :] = og
                    m_scr[g, rows_at(q0, dil), :] = mg
                    l_scr[g, rows_at(q0, dil), :] = lg
            return carry

        lax.fori_loop(0, nblk, body, 0, unroll=2)


def _attn_a(pa):
    nbatch, _, seq, _ = pa.shape
    dilations = tuple(sorted((dil for _, dil in DILATED_BRANCHES), reverse=True))
    assert dilations[-1] == 1 and all(w // dil == WIN_BLK for w, dil in DILATED_BRANCHES)
    nhalf = 2 if seq % (2 * WIN_BLK * dilations[0]) == 0 else 1
    qrows = seq // nhalf
    assert qrows % (WIN_BLK * dilations[0]) == 0
    return pl.pallas_call(
        functools.partial(_attn_a_kernel, dilations=dilations),
        grid=(nbatch, A_KV_HEADS, nhalf),
        in_specs=[pl.BlockSpec((None, A_GROUP, qrows, HEAD_DIM), lambda b, h, t: (b, h, t, 0)),
                  pl.BlockSpec((None, None, seq, HEAD_DIM), lambda b, h, t: (b, A_HEADS + h, 0, 0)),
                  pl.BlockSpec((None, None, seq, HEAD_DIM),
                               lambda b, h, t: (b, A_HEADS + A_KV_HEADS + h, 0, 0))],
        out_specs=pl.BlockSpec((None, qrows, A_GROUP * HEAD_DIM), lambda b, h, t: (b, t, h)),
        out_shape=jax.ShapeDtypeStruct((nbatch, seq, A_HEADS * HEAD_DIM), BF16),
        scratch_shapes=[pltpu.VMEM((A_GROUP, qrows, HEAD_DIM), F32),
                        pltpu.VMEM((A_GROUP, qrows, HEAD_DIM), F32),
                        pltpu.VMEM((A_GROUP, qrows, HEAD_DIM), F32),
                        pltpu.VMEM((2, A_GROUP * WIN_BLK, 2 * WIN_BLK), F32)],
        compiler_params=_params(("parallel", "parallel", "arbitrary")),
        name="attn_a",
    )(pa, pa, pa)


KEY_CHUNK = 256


def _ordered_to_float(key):
    return pltpu.bitcast(key ^ ((key >> 31) & jnp.int32(0x7FFFFFFF)), F32)


LOWEST_FINITE_KEY = INT_MIN + 0x00800000


def _dsa_kernel(p_ref, wi_ref, k_ref, kk_ref, v_ref, *rest, topk, ncast):
    cast_in, rest = rest[:ncast], rest[ncast:]
    o_ref, cast_out = rest[0], rest[1:1 + ncast]
    sc_scr, wib_scr, q_scr, p_scr, m_scr, l_scr, acc_scr = rest[1 + ncast:]
    i = pl.program_id(1)
    nchunks = (i + 2) // 2
    halves = KEY_CHUNK // LANES

    for src, dst in zip(cast_in, cast_out):
        dst[...] = src[...].astype(dst.dtype)

    wi = wi_ref[...] * (IDX_HEADS ** -0.5 * IDX_DIM ** -0.5)
    lane = lax.broadcasted_iota(jnp.int32, (Q_BLK, LANES), 1)
    for h in range(IDX_HEADS):
        blk = p_ref[:, QI_OFF + (h // 2) * LANES:QI_OFF + (h // 2 + 1) * LANES]
        keep = (lane < IDX_DIM) if h % 2 == 0 else (lane >= IDX_DIM)
        q_scr[h * Q_BLK:(h + 1) * Q_BLK, :] = jnp.where(keep, blk, jnp.zeros_like(blk))
        wib_scr[h] = jnp.broadcast_to(wi[:, WI_LANE + h:WI_LANE + h + 1], (Q_BLK, LANES))
    qpos = i * Q_BLK + lax.broadcasted_iota(jnp.int32, (Q_BLK, KEY_CHUNK), 0)
    kiota = lax.broadcasted_iota(jnp.int32, (Q_BLK, KEY_CHUNK), 1)

    def score_chunk(c, carry):
        k0 = pl.multiple_of(c * KEY_CHUNK, KEY_CHUNK)
        z = lax.dot_general(q_scr[...], kk_ref[pl.ds(k0, KEY_CHUNK), :], (((1,), (1,)), ((), ())),
                            preferred_element_type=F32)
        sc = jnp.zeros((Q_BLK, KEY_CHUNK), F32)
        for h in range(IDX_HEADS):
            w = wib_scr[h]
            sc = sc + jnp.concatenate([w] * halves, axis=1) * jnp.maximum(z[h * Q_BLK:(h + 1) * Q_BLK], 0.0)
        sc_scr[c] = jnp.where(k0 + kiota <= qpos, sc, -jnp.inf)
        return carry

    lax.fori_loop(0, nchunks, score_chunk, 0)

    def bit_body(t, ans):
        trial = ans | jnp.left_shift(jnp.int32(1), 31 - t)
        thr = _ordered_to_float(trial ^ jnp.int32(INT_MIN))

        def count_chunk(c, acc):
            sc = sc_scr[c]
            for q in range(halves):
                acc = acc + jnp.where(sc[:, q * LANES:(q + 1) * LANES] >= thr, 1.0, 0.0)
            return acc

        acc = lax.fori_loop(0, nchunks, count_chunk, jnp.zeros((Q_BLK, LANES), F32))
        cnt = jnp.sum(acc, axis=1, keepdims=True)
        return jnp.where(cnt >= float(topk), trial, ans)

    ans = lax.fori_loop(0, 32, bit_body, jnp.zeros((Q_BLK, LANES), jnp.int32))
    tau = _ordered_to_float(jnp.maximum(ans ^ jnp.int32(INT_MIN), jnp.int32(LOWEST_FINITE_KEY)))
    tau2 = jnp.concatenate([tau] * halves, axis=1)

    for h in range(B_HEADS):
        q_scr[h * Q_BLK:(h + 1) * Q_BLK, :] = p_ref[:, QB_OFF + h * HEAD_DIM:QB_OFF + (h + 1) * HEAD_DIM]
    m_scr[...] = jnp.full(m_scr.shape, NEG, F32)
    l_scr[...] = jnp.zeros(l_scr.shape, F32)
    acc_scr[...] = jnp.zeros(acc_scr.shape, F32)

    def att_chunk(c, carry):
        k0 = pl.multiple_of(c * KEY_CHUNK, KEY_CHUNK)
        s = lax.dot_general(q_scr[...], k_ref[pl.ds(k0, KEY_CHUNK), :], (((1,), (1,)), ((), ())),
                            preferred_element_type=F32)
        bias = jnp.where(sc_scr[c] >= tau2, 0.0, NEG)
        for h in range(B_HEADS):
            hr = slice(h * Q_BLK, (h + 1) * Q_BLK)
            sh = s[hr] + bias
            m_old = m_scr[hr, :]
            m_new = jnp.maximum(m_old, jnp.max(sh, axis=-1, keepdims=True))
            alpha = jnp.exp2(m_old - m_new)
            p = jnp.exp2(sh - jnp.concatenate([m_new] * halves, axis=1))
            psum = p[:, :LANES]
            for q in range(1, halves):
                psum = psum + p[:, q * LANES:(q + 1) * LANES]
            l_scr[hr, :] = alpha * l_scr[hr, :] + psum
            m_scr[hr, :] = m_new
            p_scr[hr, :] = p.astype(BF16)
            acc_scr[hr, :] = alpha * acc_scr[hr, :]
        acc_scr[...] += jnp.dot(p_scr[...], v_ref[pl.ds(k0, KEY_CHUNK), :], preferred_element_type=F32)
        return carry

    lax.fori_loop(0, nchunks, att_chunk, 0)
    for h in range(B_HEADS):
        hr = slice(h * Q_BLK, (h + 1) * Q_BLK)
        l = jnp.sum(l_scr[hr, :], axis=-1, keepdims=True)
        o_ref[:, h * HEAD_DIM:(h + 1) * HEAD_DIM] = (acc_scr[hr, :] / l).astype(o_ref.dtype)


def _dsa(proj, aux, f32_weights):
    nbatch, seq, _ = proj.shape
    assert seq % KEY_CHUNK == 0
    topk = min(DSA_TOPK_MAX, seq // 4)
    qw = B_HEADS * HEAD_DIM
    nq = seq // Q_BLK
    steps = nbatch * nq
    full = lambda off: pl.BlockSpec((None, seq, LANES), lambda b, i: (b, 0, off // LANES))
    cast_specs = []
    for w in f32_weights:
        assert w.shape[0] % steps == 0
        cast_specs.append(pl.BlockSpec((w.shape[0] // steps, w.shape[1]), lambda b, i: (b * nq + i, 0)))
    outs = pl.pallas_call(
        functools.partial(_dsa_kernel, topk=topk, ncast=len(f32_weights)),
        grid=(nbatch, nq),
        in_specs=[pl.BlockSpec((None, Q_BLK, P_COLS), lambda b, i: (b, i, 0)),
                  pl.BlockSpec((None, Q_BLK, LANES), lambda b, i: (b, i, 0)),
                  full(KB_OFF), full(KIKI_OFF), full(VB_OFF)] + cast_specs,
        out_specs=[pl.BlockSpec((None, Q_BLK, qw), lambda b, i: (b, i, 0))] + cast_specs,
        out_shape=[jax.ShapeDtypeStruct((nbatch, seq, qw), BF16)]
        + [jax.ShapeDtypeStruct(w.shape, BF16) for w in f32_weights],
        scratch_shapes=[pltpu.VMEM((seq // KEY_CHUNK, Q_BLK, KEY_CHUNK), F32),
                        pltpu.VMEM((IDX_HEADS, Q_BLK, LANES), F32),
                        pltpu.VMEM((B_HEADS * Q_BLK, HEAD_DIM), BF16),
                        pltpu.VMEM((B_HEADS * Q_BLK, KEY_CHUNK), BF16),
                        pltpu.VMEM((B_HEADS * Q_BLK, LANES), F32),
                        pltpu.VMEM((B_HEADS * Q_BLK, LANES), F32),
                        pltpu.VMEM((B_HEADS * Q_BLK, HEAD_DIM), F32)],
        compiler_params=_params(("parallel", "arbitrary")),
        name="dsa",
    )(proj, aux, proj, proj, proj, *f32_weights)
    return outs[0], outs[1:]


def _out_proj_kernel(oa_ref, ob_ref, w_ref, x_ref, mod_ref, o_ref):
    ka = oa_ref.shape[-1]
    y = jnp.dot(oa_ref[...], w_ref[:ka, :], preferred_element_type=F32)
    y = y + jnp.dot(ob_ref[...], w_ref[ka:, :], preferred_element_type=F32)
    o_ref[...] = x_ref[...] + mod_ref[2:3, :] * y


def _out_proj(o_a, o_b, w_out, x, mod3):
    nbatch, seq, d = x.shape
    ka, kb = o_a.shape[-1], o_b.shape[-1]
    tm, tn = _tile(seq, 1024), _tile(d, 512)
    return pl.pallas_call(
        _out_proj_kernel,
        grid=(nbatch, seq // tm, d // tn),
        in_specs=[pl.BlockSpec((None, tm, ka), lambda b, i, j: (b, i, 0)),
                  pl.BlockSpec((None, tm, kb), lambda b, i, j: (b, i, 0)),
                  pl.BlockSpec((ka + kb, tn), lambda b, i, j: (0, j)),
                  pl.BlockSpec((None, tm, tn), lambda b, i, j: (b, i, j)),
                  pl.BlockSpec((None, ADA_CHUNKS, tn), lambda b, i, j: (b, 0, j))],
        out_specs=pl.BlockSpec((None, tm, tn), lambda b, i, j: (b, i, j)),
        out_shape=jax.ShapeDtypeStruct((nbatch, seq, d), F32),
        compiler_params=_params(("parallel", "parallel", "parallel")),
        name="out_proj",
    )(o_a, o_b, w_out, x, mod3)


def _peer_q_kernel(x_ref, mod_ref, g_ref, w_ref, keys_ref, h_ref, st_ref, h_scr):
    j = pl.program_id(2)

    @pl.when(j == 0)
    def _():
        h = _norm_mod(x_ref[...], g_ref[...], mod_ref[4:5, :], mod_ref[3:4, :]).astype(BF16)
        h_scr[...] = h
        h_ref[...] = h

    q = jnp.dot(h_scr[...], w_ref[...], preferred_element_type=F32)
    for blk in range(q.shape[1] // PEER_HALF):
        qh = q[:, blk * PEER_HALF:(blk + 1) * PEER_HALF]
        st_ref[blk] = lax.dot_general(keys_ref[blk], qh, (((1,), (1,)), ((), ())),
                                      preferred_element_type=F32, precision=lax.Precision.HIGHEST)


def _peer_q(x1, mod3, g, w_q, keys):
    nbatch, seq, d = x1.shape
    nq = w_q.shape[1]
    tm, tn = _tile(seq, 512), 4 * PEER_HALF
    nhp = nq // PEER_HALF
    nt = seq // tm
    return pl.pallas_call(
        _peer_q_kernel,
        grid=(nbatch, nt, nq // tn),
        in_specs=[pl.BlockSpec((None, tm, d), lambda b, i, j: (b, i, 0)),
                  pl.BlockSpec((None, ADA_CHUNKS, d), lambda b, i, j: (b, 0, 0)),
                  pl.BlockSpec((1, d), lambda b, i, j: (0, 0)),
                  pl.BlockSpec((d, tn), lambda b, i, j: (0, j)),
                  pl.BlockSpec((tn // PEER_HALF, PEER_NKEYS, PEER_HALF), lambda b, i, j: (j, 0, 0))],
        out_specs=[pl.BlockSpec((None, tm, d), lambda b, i, j: (b, i, 0)),
                   pl.BlockSpec((tn // PEER_HALF, PEER_NKEYS, tm), lambda b, i, j: (j, 0, b * nt + i))],
        out_shape=[jax.ShapeDtypeStruct((nbatch, seq, d), BF16),
                   jax.ShapeDtypeStruct((nhp, PEER_NKEYS, nbatch * seq), F32)],
        scratch_shapes=[pltpu.VMEM((tm, d), BF16)],
        compiler_params=_params(("parallel", "parallel", "arbitrary")),
        name="peer_q",
    )(x1, mod3, g.reshape(1, d), w_q, keys)


def _top_rows(x, count, one_at_a_time):
    rows = []
    ridx = lax.broadcasted_iota(jnp.int32, x.shape, 0)
    for _ in range(count):
        m = jnp.max(x, axis=0, keepdims=True)
        rows.append(m)
        hit = x == m
        if one_at_a_time:
            first = jnp.min(jnp.where(hit, ridx, x.shape[0]), axis=0, keepdims=True)
            hit = ridx == first
        x = jnp.where(hit, -jnp.inf, x)
    removed = jnp.sum(jnp.where(x == -jnp.inf, 1.0, 0.0), axis=0, keepdims=True)
    return rows, removed


def _peer_group_stats(x0, x1, one_at_a_time):
    half = PEER_TOPK // 2
    v0, r0 = _top_rows(x0, PEER_TOPK, one_at_a_time)
    v1, r1 = _top_rows(x1, PEER_TOPK, one_at_a_time)
    v0_all = jnp.concatenate(v0, axis=0)
    v1_all = jnp.concatenate(v1, axis=0)
    cand = [v0[0] + v1_all]
    cand += [v0[a] + v1_all[:half] for a in range(1, half)]
    cand += [v0_all[half:] + v1[0]]
    top, rc = _top_rows(jnp.concatenate(cand, axis=0), PEER_TOPK, one_at_a_time)
    z = jnp.zeros_like(top[0])
    for t in top:
        z = z + jnp.exp(t - top[0])
    stats = jnp.concatenate([top[-1], v0[0], v1[0], 1.0 / z] + [jnp.zeros_like(z)] * 4, axis=0)
    repeated = jnp.max(jnp.maximum(jnp.maximum(r0, r1), rc)) > float(PEER_TOPK)
    return stats, repeated


HEADS_PER_STEP = 4


def _peer_topk_kernel(st_ref, stats_ref):
    ngroups = st_ref.shape[-1] // LANES

    def group_stats(h, tg, one_at_a_time):
        cols = slice(tg * LANES, (tg + 1) * LANES)
        return _peer_group_stats(st_ref[2 * h, :, cols], st_ref[2 * h + 1, :, cols], one_at_a_time)

    def heads_step(hs, carry):
        work = [(HEADS_PER_STEP * hs + k, tg) for k in range(HEADS_PER_STEP) for tg in range(ngroups)]
        fast = [group_stats(h, tg, False) for h, tg in work]
        for (h, tg), (stats, _) in zip(work, fast):
            stats_ref[h, :, tg * LANES:(tg + 1) * LANES] = stats
        for (h, tg), (_, repeated) in zip(work, fast):
            @pl.when(repeated)
            def _(h=h, tg=tg):
                stats_ref[h, :, tg * LANES:(tg + 1) * LANES] = group_stats(h, tg, True)[0]
        return carry

    lax.fori_loop(0, PEER_HEADS // HEADS_PER_STEP, heads_step, 0)


def _peer_topk(st):
    nhp, nkeys, ntok = st.shape
    tmk = _tile(ntok, 256)
    return pl.pallas_call(
        _peer_topk_kernel,
        grid=(ntok // tmk,),
        in_specs=[pl.BlockSpec((nhp, nkeys, tmk), lambda i: (0, 0, i))],
        out_specs=pl.BlockSpec((PEER_HEADS, 8, tmk), lambda i: (0, 0, i)),
        out_shape=jax.ShapeDtypeStruct((PEER_HEADS, 8, ntok), F32),
        compiler_params=_params(("parallel",)),
        name="peer_topk",
    )(st)


EXPERT_SPLIT = 2


def _peer_dense_kernel(x_ref, u_ref, v_ref, st_ref, stats_ref, o_ref, e1_scr):
    e = pl.program_id(1)
    te, tm = u_ref.shape[0], x_ref.shape[0]
    ni = te // PEER_NKEYS
    hw = te // EXPERT_SPLIT

    @pl.when(e == 0)
    def _():
        o_ref[...] = jnp.zeros_like(o_ref)
        for h in range(PEER_HEADS):
            e1_scr[h] = jnp.exp(st_ref[2 * h + 1] - stats_ref[h, 2:3, :]) * stats_ref[h, 3:4, :]

    def scores(p):
        return lax.dot_general(u_ref[p * hw:(p + 1) * hw, :], x_ref[...], (((1,), (1,)), ((), ())),
                               preferred_element_type=F32)

    def activation(p, a_t):
        cols = []
        for il in range(p * hw // PEER_NKEYS, (p + 1) * hw // PEER_NKEYS):
            i = e * ni + il
            rows = []
            for tc in range(tm // LANES):
                ts = slice(tc * LANES, (tc + 1) * LANES)
                g = jnp.zeros((PEER_NKEYS, LANES), F32)
                for h in range(PEER_HEADS):
                    s0 = st_ref[2 * h, pl.ds(i, 1), :][:, ts]
                    e0 = jnp.exp(s0 - stats_ref[h, 1:2, ts])
                    sel = (s0 + st_ref[2 * h + 1, :, ts]) >= stats_ref[h, 0:1, ts]
                    g = g + jnp.where(sel, e0 * e1_scr[h, :, ts], 0.0)
                r0 = il * PEER_NKEYS - p * hw
                a = a_t[r0:r0 + PEER_NKEYS, ts]
                act = 0.5 * a * (1.0 + lax.erf(a * (2.0 ** -0.5))) * g
                rows.append(act.T.astype(BF16))
            cols.append(jnp.concatenate(rows, axis=0))
        return jnp.concatenate(cols, axis=1)

    a_parts = [scores(p) for p in range(EXPERT_SPLIT)]
    for p in range(EXPERT_SPLIT):
        o_ref[...] += jnp.dot(activation(p, a_parts[p]), v_ref[p * hw:(p + 1) * hw, :],
                              preferred_element_type=F32)


def _peer_dense(h2, u_tab, v_tab, st, stats):
    ntok, d = h2.shape
    nexp = u_tab.shape[0]
    tm, te = _tile(ntok, 512), 4 * PEER_NKEYS
    nhp = st.shape[0]
    return pl.pallas_call(
        _peer_dense_kernel,
        grid=(ntok // tm, nexp // te),
        in_specs=[pl.BlockSpec((tm, d), lambda i, e: (i, 0)),
                  pl.BlockSpec((te, d), lambda i, e: (e, 0)),
                  pl.BlockSpec((te, d), lambda i, e: (e, 0)),
                  pl.BlockSpec((nhp, PEER_NKEYS, tm), lambda i, e: (0, 0, i)),
                  pl.BlockSpec((PEER_HEADS, 8, tm), lambda i, e: (0, 0, i))],
        out_specs=pl.BlockSpec((tm, d), lambda i, e: (i, 0)),
        out_shape=jax.ShapeDtypeStruct((ntok, d), F32),
        scratch_shapes=[pltpu.VMEM((PEER_HEADS, PEER_NKEYS, tm), F32)],
        compiler_params=_params(("parallel", "arbitrary")),
        name="peer_dense",
    )(h2, u_tab, v_tab, st, stats)


def _final_kernel(x_ref, p_ref, mod_ref, g_ref, o_ref, *, normalize):
    y = x_ref[...] + mod_ref[5:6, :] * p_ref[...]
    if normalize:
        y = (y * lax.rsqrt(jnp.mean(y * y, axis=-1, keepdims=True) + NORM_EPS)) * g_ref[...]
    o_ref[...] = y


def _final(x1, peer_out, mod3, g, normalize):
    nbatch, seq, d = x1.shape
    tm = _tile(seq, 256)
    spec = pl.BlockSpec((None, tm, d), lambda b, i: (b, i, 0))
    return pl.pallas_call(
        functools.partial(_final_kernel, normalize=normalize),
        grid=(nbatch, seq // tm),
        in_specs=[spec, spec,
                  pl.BlockSpec((None, ADA_CHUNKS, d), lambda b, i: (b, 0, 0)),
                  pl.BlockSpec((1, d), lambda b, i: (0, 0))],
        out_specs=spec,
        out_shape=jax.ShapeDtypeStruct((nbatch, seq, d), F32),
        compiler_params=_params(("parallel", "parallel")),
        name="final",
    )(x1, peer_out, mod3, g.reshape(1, d))


def kernel(x, c, positions, ln1_g, ln2_g, w_ada, b_ada, w_in, w_out, peer_wq, peer_sub_keys, peer_u,
           peer_v, lnf_g):
    nbatch, seq, d = x.shape
    depth = w_ada.shape[0]
    tables = _rope_tables(positions)
    for layer in range(depth):
        mod3 = _ada(c, w_ada[layer], b_ada[layer]).reshape(nbatch, ADA_CHUNKS, d)
        pa, proj, aux = _in_proj(x, mod3, ln1_g[layer], _prep_w_in(w_in[layer]), tables)
        o_a = _attn_a(pa)
        o_b, (w_out_b, w_q_b, u_b, v_b) = _dsa(
            proj, aux, (w_out[layer], peer_wq[layer], peer_u[layer], peer_v[layer]))
        x = _out_proj(o_a, o_b, w_out_b, x, mod3)
        keys = peer_sub_keys[layer].reshape(2 * PEER_HEADS, PEER_NKEYS, PEER_HALF)
        h2, st = _peer_q(x, mod3, ln2_g[layer], w_q_b, keys)
        stats = _peer_topk(st)
        peer_out = _peer_dense(h2.reshape(nbatch * seq, d), u_b, v_b, st, stats)
        x = _final(x, peer_out.reshape(nbatch, seq, d), mod3, lnf_g, normalize=layer + 1 == depth)
    return x
```

```python
import functools
import math

import jax
import jax.numpy as jnp
from jax import lax
from jax.experimental import pallas as pl
from jax.experimental.pallas import tpu as pltpu

F32 = jnp.float32
BF16 = jnp.bfloat16

HEAD_DIM = 128
A_HEADS = 16
A_KV_HEADS = 4
A_GROUP = A_HEADS // A_KV_HEADS
DILATED_BRANCHES = ((128, 1), (512, 4), (2048, 16))
WIN_BLK = 128
B_HEADS = 16
IDX_HEADS = 16
IDX_DIM = 64
DSA_TOPK_MAX = 256
Q_BLK = 128
PEER_HEADS = 8
PEER_NKEYS = 128
PEER_HALF = 128
PEER_TOPK = 16
ROPE_THETA = 10000.0
NORM_EPS = 1e-6
NEG = -1e30
ADA_CHUNKS = 6
INT_MIN = -(2 ** 31)

LANES = 128
VMEM_LIMIT = 56 * 1024 * 1024

PROJ_TILE = 512
IN_COLS = ((A_HEADS + 2 * A_KV_HEADS + B_HEADS + 2) * HEAD_DIM + IDX_HEADS * IDX_DIM + IDX_DIM + IDX_HEADS)
PROJ_TILES = -(-IN_COLS // PROJ_TILE)
A_TILES = (A_HEADS + 2 * A_KV_HEADS) * HEAD_DIM // PROJ_TILE
KA_TILE = A_HEADS * HEAD_DIM // PROJ_TILE
VA_TILE = KA_TILE + 1
PA_HEADS = A_HEADS + 2 * A_KV_HEADS
QB_OFF = 0
KB_OFF = QB_OFF + B_HEADS * HEAD_DIM
VB_OFF = KB_OFF + HEAD_DIM
QI_OFF = VB_OFF + HEAD_DIM
KIKI_OFF = QI_OFF + IDX_HEADS * IDX_DIM
P_COLS = (PROJ_TILES - A_TILES) * PROJ_TILE
WI_LANE = IDX_DIM


def _params(semantics):
    return pltpu.CompilerParams(dimension_semantics=semantics, vmem_limit_bytes=VMEM_LIMIT)


def _tile(n, pref):
    return pref if n % pref == 0 else n


def _norm_mod(x, g, scale, shift):
    xf = x.astype(F32)
    y = xf * lax.rsqrt(jnp.mean(xf * xf, axis=-1, keepdims=True) + NORM_EPS)
    return (y * g) * (1.0 + scale) + shift


def _ada_kernel(ct_ref, w_ref, b_ref, o_ref, *, nbatch):
    @pl.when(pl.program_id(1) == 0)
    def _():
        o_ref[...] = jnp.broadcast_to(b_ref[...], o_ref.shape)

    ct = ct_ref[...]
    s = ct * jax.nn.sigmoid(ct)
    w = w_ref[...]
    rows = [jnp.sum(w * s[:, b:b + 1], axis=0, keepdims=True) for b in range(nbatch)]
    o_ref[...] += jnp.concatenate(rows, axis=0)


def _ada(c, w_ada, b_ada):
    nbatch, d = c.shape
    n = w_ada.shape[1]
    tk, tn = _tile(d, 1024), _tile(n, 2048)
    return pl.pallas_call(
        functools.partial(_ada_kernel, nbatch=nbatch),
        grid=(n // tn, d // tk),
        in_specs=[pl.BlockSpec((tk, nbatch), lambda j, k: (k, 0)),
                  pl.BlockSpec((tk, tn), lambda j, k: (k, j)),
                  pl.BlockSpec((1, tn), lambda j, k: (0, j))],
        out_specs=pl.BlockSpec((nbatch, tn), lambda j, k: (0, j)),
        out_shape=jax.ShapeDtypeStruct((nbatch, n), F32),
        compiler_params=_params(("parallel", "arbitrary")),
        name="ada",
    )(c.T, w_ada, b_ada.reshape(1, n))


def _rope_kernel(pos_ref, ch_ref, sh_ref, ci_ref, sa_ref, sb_ref):
    pos = pos_ref[...].astype(F32)
    lane = lax.broadcasted_iota(jnp.int32, (1, LANES), 1)
    expo_h = -((2 * (lane & 63)).astype(F32)) / HEAD_DIM
    expo_i = -((2 * (lane & 31)).astype(F32)) / IDX_DIM
    inv = jnp.power(ROPE_THETA, jnp.where(lane < 64, expo_h, expo_i))
    ang = pos * inv
    c = jnp.cos(ang)
    s = jnp.sin(ang)
    ch_ref[...] = jnp.where(lane < 64, c, pltpu.roll(c, 64, 1))
    sh_ref[...] = jnp.where(lane < 64, -s, pltpu.roll(s, 64, 1))
    grp = lane >> 5
    c64, c96, c32 = pltpu.roll(c, 64, 1), pltpu.roll(c, 96, 1), pltpu.roll(c, 32, 1)
    s64, s96, s32 = pltpu.roll(s, 64, 1), pltpu.roll(s, 96, 1), pltpu.roll(s, 32, 1)
    ci_ref[...] = jnp.where(grp == 0, c64, jnp.where(grp == 1, c96, jnp.where(grp == 2, c, c32)))
    sa_ref[...] = jnp.where(grp == 0, -s64, jnp.where(grp == 2, -s, 0.0))
    sb_ref[...] = jnp.where(grp == 1, s96, jnp.where(grp == 3, s32, 0.0))


def _rope_tables(positions):
    nbatch, seq = positions.shape
    ts = _tile(seq, 1024)
    spec = pl.BlockSpec((None, ts, LANES), lambda b, i: (b, i, 0))
    shape = jax.ShapeDtypeStruct((nbatch, seq, LANES), F32)
    return pl.pallas_call(
        _rope_kernel,
        grid=(nbatch, seq // ts),
        in_specs=[pl.BlockSpec((None, ts, 1), lambda b, i: (b, i, 0))],
        out_specs=[spec] * 5,
        out_shape=[shape] * 5,
        compiler_params=_params(("parallel", "parallel")),
        name="rope",
    )(positions.reshape(nbatch, seq, 1))


def _rope_head(a, ch, sh):
    return a * ch + pltpu.roll(a, 64, 1) * sh


def _rope_idx(a, ci, sa, sb):
    return a * ci + pltpu.roll(a, 96, 1) * sa + pltpu.roll(a, 32, 1) * sb


def _in_proj_kernel(x_ref, mod_ref, g_ref, w_ref, ch_ref, sh_ref, ci_ref, sa_ref, sb_ref,
                    pa_ref, p_ref, aux_ref, h_scr, acc_scr):
    j = pl.program_id(2)

    @pl.when(j == 0)
    def _():
        h_scr[...] = _norm_mod(x_ref[...], g_ref[...], mod_ref[1:2, :], mod_ref[0:1, :]).astype(BF16)

    acc_scr[...] = lax.dot_general(h_scr[...], w_ref[...], (((1,), (1,)), ((), ())),
                                   preferred_element_type=F32)
    nblk = PROJ_TILE // LANES
    qscale = HEAD_DIM ** -0.5 * math.log2(math.e)
    qb_tile = A_TILES + QB_OFF // PROJ_TILE
    mix_tile = A_TILES + KB_OFF // PROJ_TILE
    last_tile = PROJ_TILES - 1

    def blk(q):
        return acc_scr[:, q * LANES:(q + 1) * LANES]

    def put(q, val):
        p_ref[:, q * LANES:(q + 1) * LANES] = val.astype(BF16)

    def head(q):
        return _rope_head(blk(q), ch_ref[...], sh_ref[...])

    def idx(q):
        return _rope_idx(blk(q), ci_ref[...], sa_ref[...], sb_ref[...])

    @pl.when(j < KA_TILE)
    def _():
        for q in range(nblk):
            pa_ref[q] = head(q) * qscale

    @pl.when(j == KA_TILE)
    def _():
        for q in range(nblk):
            pa_ref[q] = head(q)

    @pl.when(j == VA_TILE)
    def _():
        for q in range(nblk):
            pa_ref[q] = blk(q)

    @pl.when((j >= qb_tile) & (j < mix_tile))
    def _():
        for q in range(nblk):
            put(q, head(q) * qscale)

    @pl.when(j == mix_tile)
    def _():
        put(0, head(0))
        put(1, blk(1))
        put(2, idx(2))
        put(3, idx(3))

    @pl.when((j > mix_tile) & (j < last_tile))
    def _():
        for q in range(nblk):
            put(q, idx(q))

    @pl.when(j == last_tile)
    def _():
        put(0, idx(0))
        put(1, idx(1))
        ki = idx(2)
        lane = lax.broadcasted_iota(jnp.int32, ki.shape, 1)
        put(2, jnp.where(lane < IDX_DIM, ki, pltpu.roll(ki, IDX_DIM, 1)))
        put(3, blk(3))
        aux_ref[...] = blk(2)


def _prep_w_in(w_in):
    assert w_in.shape[1] == IN_COLS
    return jnp.pad(jnp.swapaxes(w_in, 0, 1).astype(BF16), ((0, PROJ_TILES * PROJ_TILE - IN_COLS), (0, 0)))


def _in_proj(x, mod3, g, w_perm, tables):
    nbatch, seq, d = x.shape
    tm = _tile(seq, 512)
    nblk = PROJ_TILE // LANES
    tab_spec = pl.BlockSpec((None, tm, LANES), lambda b, i, j: (b, i, 0))
    return pl.pallas_call(
        _in_proj_kernel,
        grid=(nbatch, seq // tm, PROJ_TILES),
        in_specs=[pl.BlockSpec((None, tm, d), lambda b, i, j: (b, i, 0)),
                  pl.BlockSpec((None, ADA_CHUNKS, d), lambda b, i, j: (b, 0, 0)),
                  pl.BlockSpec((1, d), lambda b, i, j: (0, 0)),
                  pl.BlockSpec((PROJ_TILE, d), lambda b, i, j: (j, 0))] + [tab_spec] * 5,
        out_specs=[pl.BlockSpec((None, nblk, tm, LANES), lambda b, i, j: (b, jnp.minimum(j, A_TILES - 1), i, 0)),
                   pl.BlockSpec((None, tm, PROJ_TILE), lambda b, i, j: (b, i, jnp.maximum(j - A_TILES, 0))),
                   pl.BlockSpec((None, tm, LANES), lambda b, i, j: (b, i, 0))],
        out_shape=[jax.ShapeDtypeStruct((nbatch, PA_HEADS, seq, LANES), F32),
                   jax.ShapeDtypeStruct((nbatch, seq, P_COLS), BF16),
                   jax.ShapeDtypeStruct((nbatch, seq, LANES), F32)],
        scratch_shapes=[pltpu.VMEM((tm, d), BF16), pltpu.VMEM((tm, PROJ_TILE), F32)],
        compiler_params=_params(("parallel", "parallel", "arbitrary")),
        name="in_proj",
    )(x, mod3, g.reshape(1, d), w_perm, *tables)


def _attn_a_kernel(q_ref, k_ref, v_ref, o_ref, acc_scr, m_scr, l_scr, bias_scr, *, dilations):
    half = pl.program_id(2)
    qrows = q_ref.shape[1]
    nblk = qrows // WIN_BLK

    rows = A_GROUP * WIN_BLK
    row = lax.broadcasted_iota(jnp.int32, (rows, 2 * WIN_BLK), 0) & (WIN_BLK - 1)
    col = lax.broadcasted_iota(jnp.int32, (rows, 2 * WIN_BLK), 1)
    band = (col >= row) & (col <= row + WIN_BLK)
    bias_scr[0] = jnp.where(band & (col >= WIN_BLK), 0.0, NEG)
    bias_scr[1] = jnp.where(band, 0.0, NEG)

    def rows_at(start, dil):
        return pl.ds(start, WIN_BLK) if dil == 1 else pl.ds(start, WIN_BLK, stride=dil)

    for idx, dil in enumerate(dilations):
        per_res = nblk // dil
        first, last = idx == 0, idx == len(dilations) - 1

        def body(blk, carry, dil=dil, per_res=per_res, first=first, last=last):
            r = blk // per_res
            n_loc = blk - r * per_res
            n = half * per_res + n_loc
            q0 = r + dil * WIN_BLK * n_loc
            k0 = r + dil * WIN_BLK * n
            kp = r + dil * WIN_BLK * jnp.maximum(n - 1, 0)
            q4 = jnp.concatenate([q_ref[g, rows_at(q0, dil), :] for g in range(A_GROUP)], axis=0).astype(BF16)
            kc = jnp.concatenate([k_ref[rows_at(kp, dil), :], k_ref[rows_at(k0, dil), :]], axis=0).astype(BF16)
            vc = jnp.concatenate([v_ref[rows_at(kp, dil), :], v_ref[rows_at(k0, dil), :]], axis=0).astype(BF16)
            s = lax.dot_general(q4, kc, (((1,), (1,)), ((), ())), preferred_element_type=F32)
            s = s + bias_scr[jnp.minimum(n, 1)]
            m = jnp.max(s, axis=-1, keepdims=True)
            p = jnp.exp2(s - m)
            l = jnp.sum(p, axis=-1, keepdims=True)
            o = jnp.dot(p.astype(BF16), vc, preferred_element_type=F32)
            for g in range(A_GROUP):
                og = o[g * WIN_BLK:(g + 1) * WIN_BLK]
                mg = jnp.broadcast_to(m[g * WIN_BLK:(g + 1) * WIN_BLK], (WIN_BLK, HEAD_DIM))
                lg = jnp.broadcast_to(l[g * WIN_BLK:(g + 1) * WIN_BLK], (WIN_BLK, HEAD_DIM))
                if not first:
                    mp = m_scr[g, rows_at(q0, dil), :]
                    decay = jnp.exp2(-jnp.abs(mg - mp))
                    w_new = jnp.where(mg >= mp, 1.0, decay)
                    w_old = jnp.where(mg >= mp, decay, 1.0)
                    og = acc_scr[g, rows_at(q0, dil), :] * w_old + og * w_new
                    lg = l_scr[g, rows_at(q0, dil), :] * w_old + lg * w_new
                    mg = jnp.maximum(mg, mp)
                if last:
                    o_ref[pl.ds(pl.multiple_of(q0, WIN_BLK), WIN_BLK), g * HEAD_DIM:(g + 1) * HEAD_DIM] = (
                        (og / lg).astype(o_ref.dtype))
                else:
                    acc_scr[g, rows_at(q0, dil), :] = og
                    m_scr[g, rows_at(q0, dil), :] = mg
                    l_scr[g, rows_at(q0, dil), :] = lg
            return carry

        lax.fori_loop(0, nblk, body, 0, unroll=4)


def _attn_a(pa):
    nbatch, _, seq, _ = pa.shape
    dilations = tuple(sorted((dil for _, dil in DILATED_BRANCHES), reverse=True))
    assert dilations[-1] == 1 and all(w // dil == WIN_BLK for w, dil in DILATED_BRANCHES)
    nhalf = 2 if seq % (2 * WIN_BLK * dilations[0]) == 0 else 1
    qrows = seq // nhalf
    assert qrows % (WIN_BLK * dilations[0]) == 0
    return pl.pallas_call(
        functools.partial(_attn_a_kernel, dilations=dilations),
        grid=(nbatch, A_KV_HEADS, nhalf),
        in_specs=[pl.BlockSpec((None, A_GROUP, qrows, HEAD_DIM), lambda b, h, t: (b, h, t, 0)),
                  pl.BlockSpec((None, None, seq, HEAD_DIM), lambda b, h, t: (b, A_HEADS + h, 0, 0)),
                  pl.BlockSpec((None, None, seq, HEAD_DIM),
                               lambda b, h, t: (b, A_HEADS + A_KV_HEADS + h, 0, 0))],
        out_specs=pl.BlockSpec((None, qrows, A_GROUP * HEAD_DIM), lambda b, h, t: (b, t, h)),
        out_shape=jax.ShapeDtypeStruct((nbatch, seq, A_HEADS * HEAD_DIM), BF16),
        scratch_shapes=[pltpu.VMEM((A_GROUP, qrows, HEAD_DIM), F32),
                        pltpu.VMEM((A_GROUP, qrows, HEAD_DIM), F32),
                        pltpu.VMEM((A_GROUP, qrows, HEAD_DIM), F32),
                        pltpu.VMEM((2, A_GROUP * WIN_BLK, 2 * WIN_BLK), F32)],
        compiler_params=_params(("parallel", "parallel", "arbitrary")),
        name="attn_a",
    )(pa, pa, pa)


KEY_CHUNK = 256


def _ordered_to_float(key):
    return pltpu.bitcast(key ^ ((key >> 31) & jnp.int32(0x7FFFFFFF)), F32)


LOWEST_FINITE_KEY = INT_MIN + 0x00800000


def _dsa_kernel(p_ref, wi_ref, k_ref, kk_ref, v_ref, *rest, topk, ncast):
    cast_in, rest = rest[:ncast], rest[ncast:]
    o_ref, cast_out = rest[0], rest[1:1 + ncast]
    sc_scr, wib_scr, q_scr, p_scr, m_scr, l_scr, acc_scr = rest[1 + ncast:]
    i = pl.program_id(1)
    nchunks = (i + 2) // 2
    halves = KEY_CHUNK // LANES

    for src, dst in zip(cast_in, cast_out):
        dst[...] = src[...].astype(dst.dtype)

    wi = wi_ref[...] * (IDX_HEADS ** -0.5 * IDX_DIM ** -0.5)
    lane = lax.broadcasted_iota(jnp.int32, (Q_BLK, LANES), 1)
    for h in range(IDX_HEADS):
        blk = p_ref[:, QI_OFF + (h // 2) * LANES:QI_OFF + (h // 2 + 1) * LANES]
        keep = (lane < IDX_DIM) if h % 2 == 0 else (lane >= IDX_DIM)
        q_scr[h * Q_BLK:(h + 1) * Q_BLK, :] = jnp.where(keep, blk, jnp.zeros_like(blk))
        wib_scr[h] = jnp.broadcast_to(wi[:, WI_LANE + h:WI_LANE + h + 1], (Q_BLK, LANES))
    qpos = i * Q_BLK + lax.broadcasted_iota(jnp.int32, (Q_BLK, KEY_CHUNK), 0)
    kiota = lax.broadcasted_iota(jnp.int32, (Q_BLK, KEY_CHUNK), 1)

    def score_chunk(c, carry):
        k0 = pl.multiple_of(c * KEY_CHUNK, KEY_CHUNK)
        z = lax.dot_general(q_scr[...], kk_ref[pl.ds(k0, KEY_CHUNK), :], (((1,), (1,)), ((), ())),
                            preferred_element_type=F32)
        sc = jnp.zeros((Q_BLK, KEY_CHUNK), F32)
        for h in range(IDX_HEADS):
            w = wib_scr[h]
            sc = sc + jnp.concatenate([w] * halves, axis=1) * jnp.maximum(z[h * Q_BLK:(h + 1) * Q_BLK], 0.0)
        sc_scr[c] = jnp.where(k0 + kiota <= qpos, sc, -jnp.inf)
        return carry

    lax.fori_loop(0, nchunks, score_chunk, 0)

    def bit_body(t, ans):
        trial = ans | jnp.left_shift(jnp.int32(1), 31 - t)
        thr = _ordered_to_float(trial ^ jnp.int32(INT_MIN))

        def count_chunk(c, acc):
            sc = sc_scr[c]
            for q in range(halves):
                acc = acc + jnp.where(sc[:, q * LANES:(q + 1) * LANES] >= thr, 1.0, 0.0)
            return acc

        acc = lax.fori_loop(0, nchunks, count_chunk, jnp.zeros((Q_BLK, LANES), F32))
        cnt = jnp.sum(acc, axis=1, keepdims=True)
        return jnp.where(cnt >= float(topk), trial, ans)

    ans = lax.fori_loop(0, 32, bit_body, jnp.zeros((Q_BLK, LANES), jnp.int32))
    tau = _ordered_to_float(jnp.maximum(ans ^ jnp.int32(INT_MIN), jnp.int32(LOWEST_FINITE_KEY)))
    tau2 = jnp.concatenate([tau] * halves, axis=1)

    for h in range(B_HEADS):
        q_scr[h * Q_BLK:(h + 1) * Q_BLK, :] = p_ref[:, QB_OFF + h * HEAD_DIM:QB_OFF + (h + 1) * HEAD_DIM]
    m_scr[...] = jnp.full(m_scr.shape, NEG, F32)
    l_scr[...] = jnp.zeros(l_scr.shape, F32)
    acc_scr[...] = jnp.zeros(acc_scr.shape, F32)

    def att_chunk(c, carry):
        k0 = pl.multiple_of(c * KEY_CHUNK, KEY_CHUNK)
        s = lax.dot_general(q_scr[...], k_ref[pl.ds(k0, KEY_CHUNK), :], (((1,), (1,)), ((), ())),
                            preferred_element_type=F32)
        bias = jnp.where(sc_scr[c] >= tau2, 0.0, NEG)
        for h in range(B_HEADS):
            hr = slice(h * Q_BLK, (h + 1) * Q_BLK)
            sh = s[hr] + bias
            m_old = m_scr[hr, :]
            m_new = jnp.maximum(m_old, jnp.max(sh, axis=-1, keepdims=True))
            alpha = jnp.exp2(m_old - m_new)
            p = jnp.exp2(sh - jnp.concatenate([m_new] * halves, axis=1))
            psum = p[:, :LANES]
            for q in range(1, halves):
                psum = psum + p[:, q * LANES:(q + 1) * LANES]
            l_scr[hr, :] = alpha * l_scr[hr, :] + psum
            m_scr[hr, :] = m_new
            p_scr[hr, :] = p.astype(BF16)
            acc_scr[hr, :] = alpha * acc_scr[hr, :]
        acc_scr[...] += jnp.dot(p_scr[...], v_ref[pl.ds(k0, KEY_CHUNK), :], preferred_element_type=F32)
        return carry

    lax.fori_loop(0, nchunks, att_chunk, 0)
    for h in range(B_HEADS):
        hr = slice(h * Q_BLK, (h + 1) * Q_BLK)
        l = jnp.sum(l_scr[hr, :], axis=-1, keepdims=True)
        o_ref[:, h * HEAD_DIM:(h + 1) * HEAD_DIM] = (acc_scr[hr, :] / l).astype(o_ref.dtype)


def _dsa(proj, aux, f32_weights):
    nbatch, seq, _ = proj.shape
    assert seq % KEY_CHUNK == 0
    topk = min(DSA_TOPK_MAX, seq // 4)
    qw = B_HEADS * HEAD_DIM
    nq = seq // Q_BLK
    steps = nbatch * nq
    full = lambda off: pl.BlockSpec((None, seq, LANES), lambda b, i: (b, 0, off // LANES))
    cast_specs = []
    for w in f32_weights:
        assert w.shape[0] % steps == 0
        cast_specs.append(pl.BlockSpec((w.shape[0] // steps, w.shape[1]), lambda b, i: (b * nq + i, 0)))
    outs = pl.pallas_call(
        functools.partial(_dsa_kernel, topk=topk, ncast=len(f32_weights)),
        grid=(nbatch, nq),
        in_specs=[pl.BlockSpec((None, Q_BLK, P_COLS), lambda b, i: (b, i, 0)),
                  pl.BlockSpec((None, Q_BLK, LANES), lambda b, i: (b, i, 0)),
                  full(KB_OFF), full(KIKI_OFF), full(VB_OFF)] + cast_specs,
        out_specs=[pl.BlockSpec((None, Q_BLK, qw), lambda b, i: (b, i, 0))] + cast_specs,
        out_shape=[jax.ShapeDtypeStruct((nbatch, seq, qw), BF16)]
        + [jax.ShapeDtypeStruct(w.shape, BF16) for w in f32_weights],
        scratch_shapes=[pltpu.VMEM((seq // KEY_CHUNK, Q_BLK, KEY_CHUNK), F32),
                        pltpu.VMEM((IDX_HEADS, Q_BLK, LANES), F32),
                        pltpu.VMEM((B_HEADS * Q_BLK, HEAD_DIM), BF16),
                        pltpu.VMEM((B_HEADS * Q_BLK, KEY_CHUNK), BF16),
                        pltpu.VMEM((B_HEADS * Q_BLK, LANES), F32),
                        pltpu.VMEM((B_HEADS * Q_BLK, LANES), F32),
                        pltpu.VMEM((B_HEADS * Q_BLK, HEAD_DIM), F32)],
        compiler_params=_params(("parallel", "arbitrary")),
        name="dsa",
    )(proj, aux, proj, proj, proj, *f32_weights)
    return outs[0], outs[1:]


def _out_proj_kernel(oa_ref, ob_ref, w_ref, x_ref, mod_ref, o_ref):
    ka = oa_ref.shape[-1]
    y = jnp.dot(oa_ref[...], w_ref[:ka, :], preferred_element_type=F32)
    y = y + jnp.dot(ob_ref[...], w_ref[ka:, :], preferred_element_type=F32)
    o_ref[...] = x_ref[...] + mod_ref[2:3, :] * y


def _out_proj(o_a, o_b, w_out, x, mod3):
    nbatch, seq, d = x.shape
    ka, kb = o_a.shape[-1], o_b.shape[-1]
    tm, tn = _tile(seq, 1024), _tile(d, 512)
    return pl.pallas_call(
        _out_proj_kernel,
        grid=(nbatch, seq // tm, d // tn),
        in_specs=[pl.BlockSpec((None, tm, ka), lambda b, i, j: (b, i, 0)),
                  pl.BlockSpec((None, tm, kb), lambda b, i, j: (b, i, 0)),
                  pl.BlockSpec((ka + kb, tn), lambda b, i, j: (0, j)),
                  pl.BlockSpec((None, tm, tn), lambda b, i, j: (b, i, j)),
                  pl.BlockSpec((None, ADA_CHUNKS, tn), lambda b, i, j: (b, 0, j))],
        out_specs=pl.BlockSpec((None, tm, tn), lambda b, i, j: (b, i, j)),
        out_shape=jax.ShapeDtypeStruct((nbatch, seq, d), F32),
        compiler_params=_params(("parallel", "parallel", "parallel")),
        name="out_proj",
    )(o_a, o_b, w_out, x, mod3)


def _peer_q_kernel(x_ref, mod_ref, g_ref, w_ref, keys_ref, h_ref, st_ref, h_scr):
    j = pl.program_id(2)

    @pl.when(j == 0)
    def _():
        h = _norm_mod(x_ref[...], g_ref[...], mod_ref[4:5, :], mod_ref[3:4, :]).astype(BF16)
        h_scr[...] = h
        h_ref[...] = h

    q = jnp.dot(h_scr[...], w_ref[...], preferred_element_type=F32)
    for blk in range(q.shape[1] // PEER_HALF):
        qh = q[:, blk * PEER_HALF:(blk + 1) * PEER_HALF]
        st_ref[blk] = lax.dot_general(keys_ref[blk], qh, (((1,), (1,)), ((), ())),
                                      preferred_element_type=F32, precision=lax.Precision.HIGHEST)


def _peer_q(x1, mod3, g, w_q, keys):
    nbatch, seq, d = x1.shape
    nq = w_q.shape[1]
    tm, tn = _tile(seq, 512), 4 * PEER_HALF
    nhp = nq // PEER_HALF
    nt = seq // tm
    return pl.pallas_call(
        _peer_q_kernel,
        grid=(nbatch, nt, nq // tn),
        in_specs=[pl.BlockSpec((None, tm, d), lambda b, i, j: (b, i, 0)),
                  pl.BlockSpec((None, ADA_CHUNKS, d), lambda b, i, j: (b, 0, 0)),
                  pl.BlockSpec((1, d), lambda b, i, j: (0, 0)),
                  pl.BlockSpec((d, tn), lambda b, i, j: (0, j)),
                  pl.BlockSpec((tn // PEER_HALF, PEER_NKEYS, PEER_HALF), lambda b, i, j: (j, 0, 0))],
        out_specs=[pl.BlockSpec((None, tm, d), lambda b, i, j: (b, i, 0)),
                   pl.BlockSpec((tn // PEER_HALF, PEER_NKEYS, tm), lambda b, i, j: (j, 0, b * nt + i))],
        out_shape=[jax.ShapeDtypeStruct((nbatch, seq, d), BF16),
                   jax.ShapeDtypeStruct((nhp, PEER_NKEYS, nbatch * seq), F32)],
        scratch_shapes=[pltpu.VMEM((tm, d), BF16)],
        compiler_params=_params(("parallel", "parallel", "arbitrary")),
        name="peer_q",
    )(x1, mod3, g.reshape(1, d), w_q, keys)


def _top_rows(x, count, one_at_a_time):
    rows = []
    ridx = lax.broadcasted_iota(jnp.int32, x.shape, 0)
    for _ in range(count):
        m = jnp.max(x, axis=0, keepdims=True)
        rows.append(m)
        hit = x == m
        if one_at_a_time:
            first = jnp.min(jnp.where(hit, ridx, x.shape[0]), axis=0, keepdims=True)
            hit = ridx == first
        x = jnp.where(hit, -jnp.inf, x)
    removed = jnp.sum(jnp.where(x == -jnp.inf, 1.0, 0.0), axis=0, keepdims=True)
    return rows, removed


def _peer_group_stats(x0, x1, one_at_a_time):
    half = PEER_TOPK // 2
    v0, r0 = _top_rows(x0, PEER_TOPK, one_at_a_time)
    v1, r1 = _top_rows(x1, PEER_TOPK, one_at_a_time)
    v0_all = jnp.concatenate(v0, axis=0)
    v1_all = jnp.concatenate(v1, axis=0)
    cand = [v0[0] + v1_all]
    cand += [v0[a] + v1_all[:half] for a in range(1, half)]
    cand += [v0_all[half:] + v1[0]]
    top, rc = _top_rows(jnp.concatenate(cand, axis=0), PEER_TOPK, one_at_a_time)
    z = jnp.zeros_like(top[0])
    for t in top:
        z = z + jnp.exp(t - top[0])
    stats = jnp.concatenate([top[-1], v0[0], v1[0], 1.0 / z] + [jnp.zeros_like(z)] * 4, axis=0)
    repeated = jnp.max(jnp.maximum(jnp.maximum(r0, r1), rc)) > float(PEER_TOPK)
    return stats, repeated


HEADS_PER_STEP = 4


def _peer_topk_kernel(st_ref, stats_ref):
    ngroups = st_ref.shape[-1] // LANES

    def group_stats(h, tg, one_at_a_time):
        cols = slice(tg * LANES, (tg + 1) * LANES)
        return _peer_group_stats(st_ref[2 * h, :, cols], st_ref[2 * h + 1, :, cols], one_at_a_time)

    def heads_step(hs, carry):
        work = [(HEADS_PER_STEP * hs + k, tg) for k in range(HEADS_PER_STEP) for tg in range(ngroups)]
        fast = [group_stats(h, tg, False) for h, tg in work]
        for (h, tg), (stats, _) in zip(work, fast):
            stats_ref[h, :, tg * LANES:(tg + 1) * LANES] = stats
        for (h, tg), (_, repeated) in zip(work, fast):
            @pl.when(repeated)
            def _(h=h, tg=tg):
                stats_ref[h, :, tg * LANES:(tg + 1) * LANES] = group_stats(h, tg, True)[0]
        return carry

    lax.fori_loop(0, PEER_HEADS // HEADS_PER_STEP, heads_step, 0)


def _peer_topk(st):
    nhp, nkeys, ntok = st.shape
    tmk = _tile(ntok, 256)
    return pl.pallas_call(
        _peer_topk_kernel,
        grid=(ntok // tmk,),
        in_specs=[pl.BlockSpec((nhp, nkeys, tmk), lambda i: (0, 0, i))],
        out_specs=pl.BlockSpec((PEER_HEADS, 8, tmk), lambda i: (0, 0, i)),
        out_shape=jax.ShapeDtypeStruct((PEER_HEADS, 8, ntok), F32),
        compiler_params=_params(("parallel",)),
        name="peer_topk",
    )(st)


EXPERT_SPLIT = 2


def _peer_dense_kernel(x_ref, u_ref, v_ref, st_ref, stats_ref, o_ref, e1_scr):
    e = pl.program_id(1)
    te, tm = u_ref.shape[0], x_ref.shape[0]
    ni = te // PEER_NKEYS
    hw = te // EXPERT_SPLIT

    @pl.when(e == 0)
    def _():
        o_ref[...] = jnp.zeros_like(o_ref)
        for h in range(PEER_HEADS):
            e1_scr[h] = jnp.exp(st_ref[2 * h + 1] - stats_ref[h, 2:3, :]) * stats_ref[h, 3:4, :]

    def scores(p):
        return lax.dot_general(u_ref[p * hw:(p + 1) * hw, :], x_ref[...], (((1,), (1,)), ((), ())),
                               preferred_element_type=F32)

    def activation(p, a_t):
        cols = []
        for il in range(p * hw // PEER_NKEYS, (p + 1) * hw // PEER_NKEYS):
            i = e * ni + il
            rows = []
            for tc in range(tm // LANES):
                ts = slice(tc * LANES, (tc + 1) * LANES)
                g = jnp.zeros((PEER_NKEYS, LANES), F32)
                for h in range(PEER_HEADS):
                    s0 = st_ref[2 * h, pl.ds(i, 1), :][:, ts]
                    e0 = jnp.exp(s0 - stats_ref[h, 1:2, ts])
                    sel = (s0 + st_ref[2 * h + 1, :, ts]) >= stats_ref[h, 0:1, ts]
                    g = g + jnp.where(sel, e0 * e1_scr[h, :, ts], 0.0)
                r0 = il * PEER_NKEYS - p * hw
                a = a_t[r0:r0 + PEER_NKEYS, ts]
                act = 0.5 * a * (1.0 + lax.erf(a * (2.0 ** -0.5))) * g
                rows.append(act.T.astype(BF16))
            cols.append(jnp.concatenate(rows, axis=0))
        return jnp.concatenate(cols, axis=1)

    a_parts = [scores(p) for p in range(EXPERT_SPLIT)]
    for p in range(EXPERT_SPLIT):
        o_ref[...] += jnp.dot(activation(p, a_parts[p]), v_ref[p * hw:(p + 1) * hw, :],
                              preferred_element_type=F32)


def _peer_dense(h2, u_tab, v_tab, st, stats):
    ntok, d = h2.shape
    nexp = u_tab.shape[0]
    tm, te = _tile(ntok, 512), 4 * PEER_NKEYS
    nhp = st.shape[0]
    return pl.pallas_call(
        _peer_dense_kernel,
        grid=(ntok // tm, nexp // te),
        in_specs=[pl.BlockSpec((tm, d), lambda i, e: (i, 0)),
                  pl.BlockSpec((te, d), lambda i, e: (e, 0)),
                  pl.BlockSpec((te, d), lambda i, e: (e, 0)),
                  pl.BlockSpec((nhp, PEER_NKEYS, tm), lambda i, e: (0, 0, i)),
                  pl.BlockSpec((PEER_HEADS, 8, tm), lambda i, e: (0, 0, i))],
        out_specs=pl.BlockSpec((tm, d), lambda i, e: (i, 0)),
        out_shape=jax.ShapeDtypeStruct((ntok, d), F32),
        scratch_shapes=[pltpu.VMEM((PEER_HEADS, PEER_NKEYS, tm), F32)],
        compiler_params=_params(("parallel", "arbitrary")),
        name="peer_dense",
    )(h2, u_tab, v_tab, st, stats)


def _final_kernel(x_ref, p_ref, mod_ref, g_ref, o_ref, *, normalize):
    y = x_ref[...] + mod_ref[5:6, :] * p_ref[...]
    if normalize:
        y = (y * lax.rsqrt(jnp.mean(y * y, axis=-1, keepdims=True) + NORM_EPS)) * g_ref[...]
    o_ref[...] = y


def _final(x1, peer_out, mod3, g, normalize):
    nbatch, seq, d = x1.shape
    tm = _tile(seq, 256)
    spec = pl.BlockSpec((None, tm, d), lambda b, i: (b, i, 0))
    return pl.pallas_call(
        functools.partial(_final_kernel, normalize=normalize),
        grid=(nbatch, seq // tm),
        in_specs=[spec, spec,
                  pl.BlockSpec((None, ADA_CHUNKS, d), lambda b, i: (b, 0, 0)),
                  pl.BlockSpec((1, d), lambda b, i: (0, 0))],
        out_specs=spec,
        out_shape=jax.ShapeDtypeStruct((nbatch, seq, d), F32),
        compiler_params=_params(("parallel", "parallel")),
        name="final",
    )(x1, peer_out, mod3, g.reshape(1, d))


def kernel(x, c, positions, ln1_g, ln2_g, w_ada, b_ada, w_in, w_out, peer_wq, peer_sub_keys, peer_u,
           peer_v, lnf_g):
    nbatch, seq, d = x.shape
    depth = w_ada.shape[0]
    tables = _rope_tables(positions)
    for layer in range(depth):
        mod3 = _ada(c, w_ada[layer], b_ada[layer]).reshape(nbatch, ADA_CHUNKS, d)
        pa, proj, aux = _in_proj(x, mod3, ln1_g[layer], _prep_w_in(w_in[layer]), tables)
        o_a = _attn_a(pa)
        o_b, (w_out_b, w_q_b, u_b, v_b) = _dsa(
            proj, aux, (w_out[layer], peer_wq[layer], peer_u[layer], peer_v[layer]))
        x = _out_proj(o_a, o_b, w_out_b, x, mod3)
        keys = peer_sub_keys[layer].reshape(2 * PEER_HEADS, PEER_NKEYS, PEER_HALF)
        h2, st = _peer_q(x, mod3, ln2_g[layer], w_q_b, keys)
        stats = _peer_topk(st)
        peer_out = _peer_dense(h2.reshape(nbatch * seq, d), u_b, v_b, st, stats)
        x = _final(x, peer_out.reshape(nbatch, seq, d), mod3, lnf_g, normalize=layer + 1 == depth)
    return x
```

```python
import functools
import math

import jax
import jax.numpy as jnp
from jax import lax
from jax.experimental import pallas as pl
from jax.experimental.pallas import tpu as pltpu

F32 = jnp.float32
BF16 = jnp.bfloat16

HEAD_DIM = 128
A_HEADS = 16
A_KV_HEADS = 4
A_GROUP = A_HEADS // A_KV_HEADS
DILATED_BRANCHES = ((128, 1), (512, 4), (2048, 16))
WIN_BLK = 128
B_HEADS = 16
IDX_HEADS = 16
IDX_DIM = 64
DSA_TOPK_MAX = 256
Q_BLK = 128
PEER_HEADS = 8
PEER_NKEYS = 128
PEER_HALF = 128
PEER_TOPK = 16
ROPE_THETA = 10000.0
NORM_EPS = 1e-6
NEG = -1e30
ADA_CHUNKS = 6
INT_MIN = -(2 ** 31)

LANES = 128
VMEM_LIMIT = 56 * 1024 * 1024

PROJ_TILE = 512
IN_COLS = ((A_HEADS + 2 * A_KV_HEADS + B_HEADS + 2) * HEAD_DIM + IDX_HEADS * IDX_DIM + IDX_DIM + IDX_HEADS)
PROJ_TILES = -(-IN_COLS // PROJ_TILE)
A_TILES = (A_HEADS + 2 * A_KV_HEADS) * HEAD_DIM // PROJ_TILE
KA_TILE = A_HEADS * HEAD_DIM // PROJ_TILE
VA_TILE = KA_TILE + 1
PA_HEADS = A_HEADS + 2 * A_KV_HEADS
QB_OFF = 0
KB_OFF = QB_OFF + B_HEADS * HEAD_DIM
VB_OFF = KB_OFF + HEAD_DIM
QI_OFF = VB_OFF + HEAD_DIM
KIKI_OFF = QI_OFF + IDX_HEADS * IDX_DIM
P_COLS = (PROJ_TILES - A_TILES) * PROJ_TILE
WI_LANE = IDX_DIM


def _params(semantics):
    return pltpu.CompilerParams(dimension_semantics=semantics, vmem_limit_bytes=VMEM_LIMIT)


def _tile(n, pref):
    return pref if n % pref == 0 else n


def _norm_mod(x, g, scale, shift):
    xf = x.astype(F32)
    y = xf * lax.rsqrt(jnp.mean(xf * xf, axis=-1, keepdims=True) + NORM_EPS)
    return (y * g) * (1.0 + scale) + shift


def _ada_kernel(ct_ref, w_ref, b_ref, o_ref, *, nbatch):
    @pl.when(pl.program_id(1) == 0)
    def _():
        o_ref[...] = jnp.broadcast_to(b_ref[...], o_ref.shape)

    ct = ct_ref[...]
    s = ct * jax.nn.sigmoid(ct)
    w = w_ref[...]
    rows = [jnp.sum(w * s[:, b:b + 1], axis=0, keepdims=True) for b in range(nbatch)]
    o_ref[...] += jnp.concatenate(rows, axis=0)


def _ada(c, w_ada, b_ada):
    nbatch, d = c.shape
    n = w_ada.shape[1]
    tk, tn = _tile(d, 1024), _tile(n, 2048)
    return pl.pallas_call(
        functools.partial(_ada_kernel, nbatch=nbatch),
        grid=(n // tn, d // tk),
        in_specs=[pl.BlockSpec((tk, nbatch), lambda j, k: (k, 0)),
                  pl.BlockSpec((tk, tn), lambda j, k: (k, j)),
                  pl.BlockSpec((1, tn), lambda j, k: (0, j))],
        out_specs=pl.BlockSpec((nbatch, tn), lambda j, k: (0, j)),
        out_shape=jax.ShapeDtypeStruct((nbatch, n), F32),
        compiler_params=_params(("parallel", "arbitrary")),
        name="ada",
    )(c.T, w_ada, b_ada.reshape(1, n))


def _rope_kernel(pos_ref, ch_ref, sh_ref, ci_ref, sa_ref, sb_ref):
    pos = pos_ref[...].astype(F32)
    lane = lax.broadcasted_iota(jnp.int32, (1, LANES), 1)
    expo_h = -((2 * (lane & 63)).astype(F32)) / HEAD_DIM
    expo_i = -((2 * (lane & 31)).astype(F32)) / IDX_DIM
    inv = jnp.power(ROPE_THETA, jnp.where(lane < 64, expo_h, expo_i))
    ang = pos * inv
    c = jnp.cos(ang)
    s = jnp.sin(ang)
    ch_ref[...] = jnp.where(lane < 64, c, pltpu.roll(c, 64, 1))
    sh_ref[...] = jnp.where(lane < 64, -s, pltpu.roll(s, 64, 1))
    grp = lane >> 5
    c64, c96, c32 = pltpu.roll(c, 64, 1), pltpu.roll(c, 96, 1), pltpu.roll(c, 32, 1)
    s64, s96, s32 = pltpu.roll(s, 64, 1), pltpu.roll(s, 96, 1), pltpu.roll(s, 32, 1)
    ci_ref[...] = jnp.where(grp == 0, c64, jnp.where(grp == 1, c96, jnp.where(grp == 2, c, c32)))
    sa_ref[...] = jnp.where(grp == 0, -s64, jnp.where(grp == 2, -s, 0.0))
    sb_ref[...] = jnp.where(grp == 1, s96, jnp.where(grp == 3, s32, 0.0))


def _rope_tables(positions):
    nbatch, seq = positions.shape
    ts = _tile(seq, 1024)
    spec = pl.BlockSpec((None, ts, LANES), lambda b, i: (b, i, 0))
    shape = jax.ShapeDtypeStruct((nbatch, seq, LANES), F32)
    return pl.pallas_call(
        _rope_kernel,
        grid=(nbatch, seq // ts),
        in_specs=[pl.BlockSpec((None, ts, 1), lambda b, i: (b, i, 0))],
        out_specs=[spec] * 5,
        out_shape=[shape] * 5,
        compiler_params=_params(("parallel", "parallel")),
        name="rope",
    )(positions.reshape(nbatch, seq, 1))


def _rope_head(a, ch, sh):
    return a * ch + pltpu.roll(a, 64, 1) * sh


def _rope_idx(a, ci, sa, sb):
    return a * ci + pltpu.roll(a, 96, 1) * sa + pltpu.roll(a, 32, 1) * sb


def _in_proj_kernel(x_ref, mod_ref, g_ref, w_ref, wt_ref, ch_ref, sh_ref, ci_ref, sa_ref, sb_ref,
                    pa_ref, p_ref, aux_ref, h_scr, acc_scr):
    j = pl.program_id(2)

    @pl.when(j == 0)
    def _():
        h_scr[...] = _norm_mod(x_ref[...], g_ref[...], mod_ref[1:2, :], mod_ref[0:1, :]).astype(BF16)

    def project(w_tile_ref):
        acc_scr[...] = lax.dot_general(h_scr[...], w_tile_ref[...], (((1,), (1,)), ((), ())),
                                       preferred_element_type=F32)

    @pl.when(j < PROJ_TILES - 1)
    def _():
        project(w_ref)

    @pl.when(j == PROJ_TILES - 1)
    def _():
        project(wt_ref)

    nblk = PROJ_TILE // LANES
    qscale = HEAD_DIM ** -0.5 * math.log2(math.e)
    qb_tile = A_TILES + QB_OFF // PROJ_TILE
    mix_tile = A_TILES + KB_OFF // PROJ_TILE
    last_tile = PROJ_TILES - 1

    def blk(q):
        return acc_scr[:, q * LANES:(q + 1) * LANES]

    def put(q, val):
        p_ref[:, q * LANES:(q + 1) * LANES] = val.astype(BF16)

    def head(q):
        return _rope_head(blk(q), ch_ref[...], sh_ref[...])

    def idx(q):
        return _rope_idx(blk(q), ci_ref[...], sa_ref[...], sb_ref[...])

    @pl.when(j < KA_TILE)
    def _():
        for q in range(nblk):
            pa_ref[q] = head(q) * qscale

    @pl.when(j == KA_TILE)
    def _():
        for q in range(nblk):
            pa_ref[q] = head(q)

    @pl.when(j == VA_TILE)
    def _():
        for q in range(nblk):
            pa_ref[q] = blk(q)

    @pl.when((j >= qb_tile) & (j < mix_tile))
    def _():
        for q in range(nblk):
            put(q, head(q) * qscale)

    @pl.when(j == mix_tile)
    def _():
        put(0, head(0))
        put(1, blk(1))
        put(2, idx(2))
        put(3, idx(3))

    @pl.when((j > mix_tile) & (j < last_tile))
    def _():
        for q in range(nblk):
            put(q, idx(q))

    @pl.when(j == last_tile)
    def _():
        put(0, idx(0))
        put(1, idx(1))
        ki = idx(2)
        lane = lax.broadcasted_iota(jnp.int32, ki.shape, 1)
        put(2, jnp.where(lane < IDX_DIM, ki, pltpu.roll(ki, IDX_DIM, 1)))
        put(3, blk(3))
        aux_ref[...] = blk(2)


def _prep_w_in(w_in):
    assert w_in.shape[1] == IN_COLS
    w_t = jnp.swapaxes(w_in, 0, 1).astype(BF16)
    full = (PROJ_TILES - 1) * PROJ_TILE
    return w_t, jnp.pad(w_t[full:], ((0, PROJ_TILES * PROJ_TILE - IN_COLS), (0, 0)))


def _in_proj(x, mod3, g, w_parts, tables):
    nbatch, seq, d = x.shape
    tm = _tile(seq, 512)
    nblk = PROJ_TILE // LANES
    w_t, w_tail = w_parts
    tab_spec = pl.BlockSpec((None, tm, LANES), lambda b, i, j: (b, i, 0))
    return pl.pallas_call(
        _in_proj_kernel,
        grid=(nbatch, seq // tm, PROJ_TILES),
        in_specs=[pl.BlockSpec((None, tm, d), lambda b, i, j: (b, i, 0)),
                  pl.BlockSpec((None, ADA_CHUNKS, d), lambda b, i, j: (b, 0, 0)),
                  pl.BlockSpec((1, d), lambda b, i, j: (0, 0)),
                  pl.BlockSpec((PROJ_TILE, d), lambda b, i, j: (jnp.minimum(j, PROJ_TILES - 2), 0)),
                  pl.BlockSpec((PROJ_TILE, d), lambda b, i, j: (0, 0))] + [tab_spec] * 5,
        out_specs=[pl.BlockSpec((None, nblk, tm, LANES), lambda b, i, j: (b, jnp.minimum(j, A_TILES - 1), i, 0)),
                   pl.BlockSpec((None, tm, PROJ_TILE), lambda b, i, j: (b, i, jnp.maximum(j - A_TILES, 0))),
                   pl.BlockSpec((None, tm, LANES), lambda b, i, j: (b, i, 0))],
        out_shape=[jax.ShapeDtypeStruct((nbatch, PA_HEADS, seq, LANES), F32),
                   jax.ShapeDtypeStruct((nbatch, seq, P_COLS), BF16),
                   jax.ShapeDtypeStruct((nbatch, seq, LANES), F32)],
        scratch_shapes=[pltpu.VMEM((tm, d), BF16), pltpu.VMEM((tm, PROJ_TILE), F32)],
        compiler_params=_params(("parallel", "parallel", "arbitrary")),
        name="in_proj",
    )(x, mod3, g.reshape(1, d), w_t, w_tail, *tables)


def _attn_a_kernel(q_ref, k_ref, v_ref, o_ref, acc_scr, m_scr, l_scr, bias_scr, *, dilations):
    half = pl.program_id(2)
    qrows = q_ref.shape[1]
    nblk = qrows // WIN_BLK

    rows = A_GROUP * WIN_BLK
    row = lax.broadcasted_iota(jnp.int32, (rows, 2 * WIN_BLK), 0) & (WIN_BLK - 1)
    col = lax.broadcasted_iota(jnp.int32, (rows, 2 * WIN_BLK), 1)
    band = (col >= row) & (col <= row + WIN_BLK)
    bias_scr[0] = jnp.where(band & (col >= WIN_BLK), 0.0, NEG)
    bias_scr[1] = jnp.where(band, 0.0, NEG)

    def rows_at(start, dil):
        return pl.ds(start, WIN_BLK) if dil == 1 else pl.ds(start, WIN_BLK, stride=dil)

    for idx, dil in enumerate(dilations):
        per_res = nblk // dil
        first, last = idx == 0, idx == len(dilations) - 1

        def body(blk, carry, dil=dil, per_res=per_res, first=first, last=last):
            r = blk // per_res
            n_loc = blk - r * per_res
            n = half * per_res + n_loc
            q0 = r + dil * WIN_BLK * n_loc
            k0 = r + dil * WIN_BLK * n
            kp = r + dil * WIN_BLK * jnp.maximum(n - 1, 0)
            q4 = jnp.concatenate([q_ref[g, rows_at(q0, dil), :] for g in range(A_GROUP)], axis=0).astype(BF16)
            kc = jnp.concatenate([k_ref[rows_at(kp, dil), :], k_ref[rows_at(k0, dil), :]], axis=0).astype(BF16)
            vc = jnp.concatenate([v_ref[rows_at(kp, dil), :], v_ref[rows_at(k0, dil), :]], axis=0).astype(BF16)
            s = lax.dot_general(q4, kc, (((1,), (1,)), ((), ())), preferred_element_type=F32)
            s = s + bias_scr[jnp.minimum(n, 1)]
            m = jnp.max(s, axis=-1, keepdims=True)
            p = jnp.exp2(s - m)
            l = jnp.sum(p, axis=-1, keepdims=True)
            o = jnp.dot(p.astype(BF16), vc, preferred_element_type=F32)
            for g in range(A_GROUP):
                og = o[g * WIN_BLK:(g + 1) * WIN_BLK]
                mg = jnp.broadcast_to(m[g * WIN_BLK:(g + 1) * WIN_BLK], (WIN_BLK, HEAD_DIM))
                lg = jnp.broadcast_to(l[g * WIN_BLK:(g + 1) * WIN_BLK], (WIN_BLK, HEAD_DIM))
                if not first:
                    mp = m_scr[g, rows_at(q0, dil), :]
                    decay = jnp.exp2(-jnp.abs(mg - mp))
                    w_new = jnp.where(mg >= mp, 1.0, decay)
                    w_old = jnp.where(mg >= mp, decay, 1.0)
                    og = acc_scr[g, rows_at(q0, dil), :] * w_old + og * w_new
                    lg = l_scr[g, rows_at(q0, dil), :] * w_old + lg * w_new
                    mg = jnp.maximum(mg, mp)
                if last:
                    o_ref[pl.ds(pl.multiple_of(q0, WIN_BLK), WIN_BLK), g * HEAD_DIM:(g + 1) * HEAD_DIM] = (
                        (og / lg).astype(o_ref.dtype))
                else:
                    acc_scr[g, rows_at(q0, dil), :] = og
                    m_scr[g, rows_at(q0, dil), :] = mg
                    l_scr[g, rows_at(q0, dil), :] = lg
            return carry

        lax.fori_loop(0, nblk, body, 0, unroll=4)


def _attn_a(pa):
    nbatch, _, seq, _ = pa.shape
    dilations = tuple(sorted((dil for _, dil in DILATED_BRANCHES), reverse=True))
    assert dilations[-1] == 1 and all(w // dil == WIN_BLK for w, dil in DILATED_BRANCHES)
    nhalf = 2 if seq % (2 * WIN_BLK * dilations[0]) == 0 else 1
    qrows = seq // nhalf
    assert qrows % (WIN_BLK * dilations[0]) == 0
    return pl.pallas_call(
        functools.partial(_attn_a_kernel, dilations=dilations),
        grid=(nbatch, A_KV_HEADS, nhalf),
        in_specs=[pl.BlockSpec((None, A_GROUP, qrows, HEAD_DIM), lambda b, h, t: (b, h, t, 0)),
                  pl.BlockSpec((None, None, seq, HEAD_DIM), lambda b, h, t: (b, A_HEADS + h, 0, 0)),
                  pl.BlockSpec((None, None, seq, HEAD_DIM),
                               lambda b, h, t: (b, A_HEADS + A_KV_HEADS + h, 0, 0))],
        out_specs=pl.BlockSpec((None, qrows, A_GROUP * HEAD_DIM), lambda b, h, t: (b, t, h)),
        out_shape=jax.ShapeDtypeStruct((nbatch, seq, A_HEADS * HEAD_DIM), BF16),
        scratch_shapes=[pltpu.VMEM((A_GROUP, qrows, HEAD_DIM), F32),
                        pltpu.VMEM((A_GROUP, qrows, HEAD_DIM), F32),
                        pltpu.VMEM((A_GROUP, qrows, HEAD_DIM), F32),
                        pltpu.VMEM((2, A_GROUP * WIN_BLK, 2 * WIN_BLK), F32)],
        compiler_params=_params(("parallel", "parallel", "arbitrary")),
        name="attn_a",
    )(pa, pa, pa)


KEY_CHUNK = 256


def _ordered_to_float(key):
    return pltpu.bitcast(key ^ ((key >> 31) & jnp.int32(0x7FFFFFFF)), F32)


LOWEST_FINITE_KEY = INT_MIN + 0x00800000


def _dsa_kernel(p_ref, wi_ref, k_ref, kk_ref, v_ref, *rest, topk, ncast):
    cast_in, rest = rest[:ncast], rest[ncast:]
    o_ref, cast_out = rest[0], rest[1:1 + ncast]
    sc_scr, wib_scr, q_scr, p_scr, m_scr, l_scr, acc_scr = rest[1 + ncast:]
    i = pl.program_id(1)
    nchunks = (i + 2) // 2
    halves = KEY_CHUNK // LANES

    for src, dst in zip(cast_in, cast_out):
        dst[...] = src[...].astype(dst.dtype)

    wi = wi_ref[...] * (IDX_HEADS ** -0.5 * IDX_DIM ** -0.5)
    lane = lax.broadcasted_iota(jnp.int32, (Q_BLK, LANES), 1)
    for h in range(IDX_HEADS):
        blk = p_ref[:, QI_OFF + (h // 2) * LANES:QI_OFF + (h // 2 + 1) * LANES]
        keep = (lane < IDX_DIM) if h % 2 == 0 else (lane >= IDX_DIM)
        q_scr[h * Q_BLK:(h + 1) * Q_BLK, :] = jnp.where(keep, blk, jnp.zeros_like(blk))
        wib_scr[h] = jnp.broadcast_to(wi[:, WI_LANE + h:WI_LANE + h + 1], (Q_BLK, LANES))
    qpos = i * Q_BLK + lax.broadcasted_iota(jnp.int32, (Q_BLK, KEY_CHUNK), 0)
    kiota = lax.broadcasted_iota(jnp.int32, (Q_BLK, KEY_CHUNK), 1)

    def score_chunk(c, carry):
        k0 = pl.multiple_of(c * KEY_CHUNK, KEY_CHUNK)
        z = lax.dot_general(q_scr[...], kk_ref[pl.ds(k0, KEY_CHUNK), :], (((1,), (1,)), ((), ())),
                            preferred_element_type=F32)
        sc = jnp.zeros((Q_BLK, KEY_CHUNK), F32)
        for h in range(IDX_HEADS):
            w = wib_scr[h]
            sc = sc + jnp.concatenate([w] * halves, axis=1) * jnp.maximum(z[h * Q_BLK:(h + 1) * Q_BLK], 0.0)
        sc_scr[c] = jnp.where(k0 + kiota <= qpos, sc, -jnp.inf)
        return carry

    lax.fori_loop(0, nchunks, score_chunk, 0)

    def bit_body(t, ans):
        trial = ans | jnp.left_shift(jnp.int32(1), 31 - t)
        thr = _ordered_to_float(trial ^ jnp.int32(INT_MIN))

        def count_chunk(c, acc):
            sc = sc_scr[c]
            for q in range(halves):
                acc = acc + jnp.where(sc[:, q * LANES:(q + 1) * LANES] >= thr, 1.0, 0.0)
            return acc

        acc = lax.fori_loop(0, nchunks, count_chunk, jnp.zeros((Q_BLK, LANES), F32))
        cnt = jnp.sum(acc, axis=1, keepdims=True)
        return jnp.where(cnt >= float(topk), trial, ans)

    ans = lax.fori_loop(0, 32, bit_body, jnp.zeros((Q_BLK, LANES), jnp.int32))
    tau = _ordered_to_float(jnp.maximum(ans ^ jnp.int32(INT_MIN), jnp.int32(LOWEST_FINITE_KEY)))
    tau2 = jnp.concatenate([tau] * halves, axis=1)

    for h in range(B_HEADS):
        q_scr[h * Q_BLK:(h + 1) * Q_BLK, :] = p_ref[:, QB_OFF + h * HEAD_DIM:QB_OFF + (h + 1) * HEAD_DIM]
    m_scr[...] = jnp.full(m_scr.shape, NEG, F32)
    l_scr[...] = jnp.zeros(l_scr.shape, F32)
    acc_scr[...] = jnp.zeros(acc_scr.shape, F32)

    def att_chunk(c, carry):
        k0 = pl.multiple_of(c * KEY_CHUNK, KEY_CHUNK)
        s = lax.dot_general(q_scr[...], k_ref[pl.ds(k0, KEY_CHUNK), :], (((1,), (1,)), ((), ())),
                            preferred_element_type=F32)
        bias = jnp.where(sc_scr[c] >= tau2, 0.0, NEG)
        for h in range(B_HEADS):
            hr = slice(h * Q_BLK, (h + 1) * Q_BLK)
            sh = s[hr] + bias
            m_old = m_scr[hr, :]
            m_new = jnp.maximum(m_old, jnp.max(sh, axis=-1, keepdims=True))
            alpha = jnp.exp2(m_old - m_new)
            p = jnp.exp2(sh - jnp.concatenate([m_new] * halves, axis=1))
            psum = p[:, :LANES]
            for q in range(1, halves):
                psum = psum + p[:, q * LANES:(q + 1) * LANES]
            l_scr[hr, :] = alpha * l_scr[hr, :] + psum
            m_scr[hr, :] = m_new
            p_scr[hr, :] = p.astype(BF16)
            acc_scr[hr, :] = alpha * acc_scr[hr, :]
        acc_scr[...] += jnp.dot(p_scr[...], v_ref[pl.ds(k0, KEY_CHUNK), :], preferred_element_type=F32)
        return carry

    lax.fori_loop(0, nchunks, att_chunk, 0)
    for h in range(B_HEADS):
        hr = slice(h * Q_BLK, (h + 1) * Q_BLK)
        l = jnp.sum(l_scr[hr, :], axis=-1, keepdims=True)
        o_ref[:, h * HEAD_DIM:(h + 1) * HEAD_DIM] = (acc_scr[hr, :] / l).astype(o_ref.dtype)


def _dsa(proj, aux, f32_weights):
    nbatch, seq, _ = proj.shape
    assert seq % KEY_CHUNK == 0
    topk = min(DSA_TOPK_MAX, seq // 4)
    qw = B_HEADS * HEAD_DIM
    nq = seq // Q_BLK
    steps = nbatch * nq
    full = lambda off: pl.BlockSpec((None, seq, LANES), lambda b, i: (b, 0, off // LANES))
    cast_specs = []
    for w in f32_weights:
        assert w.shape[0] % steps == 0
        cast_specs.append(pl.BlockSpec((w.shape[0] // steps, w.shape[1]), lambda b, i: (b * nq + i, 0)))
    outs = pl.pallas_call(
        functools.partial(_dsa_kernel, topk=topk, ncast=len(f32_weights)),
        grid=(nbatch, nq),
        in_specs=[pl.BlockSpec((None, Q_BLK, P_COLS), lambda b, i: (b, i, 0)),
                  pl.BlockSpec((None, Q_BLK, LANES), lambda b, i: (b, i, 0)),
                  full(KB_OFF), full(KIKI_OFF), full(VB_OFF)] + cast_specs,
        out_specs=[pl.BlockSpec((None, Q_BLK, qw), lambda b, i: (b, i, 0))] + cast_specs,
        out_shape=[jax.ShapeDtypeStruct((nbatch, seq, qw), BF16)]
        + [jax.ShapeDtypeStruct(w.shape, BF16) for w in f32_weights],
        scratch_shapes=[pltpu.VMEM((seq // KEY_CHUNK, Q_BLK, KEY_CHUNK), F32),
                        pltpu.VMEM((IDX_HEADS, Q_BLK, LANES), F32),
                        pltpu.VMEM((B_HEADS * Q_BLK, HEAD_DIM), BF16),
                        pltpu.VMEM((B_HEADS * Q_BLK, KEY_CHUNK), BF16),
                        pltpu.VMEM((B_HEADS * Q_BLK, LANES), F32),
                        pltpu.VMEM((B_HEADS * Q_BLK, LANES), F32),
                        pltpu.VMEM((B_HEADS * Q_BLK, HEAD_DIM), F32)],
        compiler_params=_params(("parallel", "arbitrary")),
        name="dsa",
    )(proj, aux, proj, proj, proj, *f32_weights)
    return outs[0], outs[1:]


def _out_proj_kernel(oa_ref, ob_ref, w_ref, x_ref, mod_ref, o_ref):
    ka = oa_ref.shape[-1]
    y = jnp.dot(oa_ref[...], w_ref[:ka, :], preferred_element_type=F32)
    y = y + jnp.dot(ob_ref[...], w_ref[ka:, :], preferred_element_type=F32)
    o_ref[...] = x_ref[...] + mod_ref[2:3, :] * y


def _out_proj(o_a, o_b, w_out, x, mod3):
    nbatch, seq, d = x.shape
    ka, kb = o_a.shape[-1], o_b.shape[-1]
    tm, tn = _tile(seq, 1024), _tile(d, 512)
    return pl.pallas_call(
        _out_proj_kernel,
        grid=(nbatch, seq // tm, d // tn),
        in_specs=[pl.BlockSpec((None, tm, ka), lambda b, i, j: (b, i, 0)),
                  pl.BlockSpec((None, tm, kb), lambda b, i, j: (b, i, 0)),
                  pl.BlockSpec((ka + kb, tn), lambda b, i, j: (0, j)),
                  pl.BlockSpec((None, tm, tn), lambda b, i, j: (b, i, j)),
                  pl.BlockSpec((None, ADA_CHUNKS, tn), lambda b, i, j: (b, 0, j))],
        out_specs=pl.BlockSpec((None, tm, tn), lambda b, i, j: (b, i, j)),
        out_shape=jax.ShapeDtypeStruct((nbatch, seq, d), F32),
        compiler_params=_params(("parallel", "parallel", "parallel")),
        name="out_proj",
    )(o_a, o_b, w_out, x, mod3)


def _peer_q_kernel(x_ref, mod_ref, g_ref, w_ref, keys_ref, h_ref, st_ref, h_scr):
    j = pl.program_id(2)

    @pl.when(j == 0)
    def _():
        h = _norm_mod(x_ref[...], g_ref[...], mod_ref[4:5, :], mod_ref[3:4, :]).astype(BF16)
        h_scr[...] = h
        h_ref[...] = h

    q = jnp.dot(h_scr[...], w_ref[...], preferred_element_type=F32)
    for blk in range(q.shape[1] // PEER_HALF):
        qh = q[:, blk * PEER_HALF:(blk + 1) * PEER_HALF]
        st_ref[blk] = lax.dot_general(keys_ref[blk], qh, (((1,), (1,)), ((), ())),
                                      preferred_element_type=F32, precision=lax.Precision.HIGHEST)


def _peer_q(x1, mod3, g, w_q, keys):
    nbatch, seq, d = x1.shape
    nq = w_q.shape[1]
    tm, tn = _tile(seq, 512), 4 * PEER_HALF
    nhp = nq // PEER_HALF
    nt = seq // tm
    return pl.pallas_call(
        _peer_q_kernel,
        grid=(nbatch, nt, nq // tn),
        in_specs=[pl.BlockSpec((None, tm, d), lambda b, i, j: (b, i, 0)),
                  pl.BlockSpec((None, ADA_CHUNKS, d), lambda b, i, j: (b, 0, 0)),
                  pl.BlockSpec((1, d), lambda b, i, j: (0, 0)),
                  pl.BlockSpec((d, tn), lambda b, i, j: (0, j)),
                  pl.BlockSpec((tn // PEER_HALF, PEER_NKEYS, PEER_HALF), lambda b, i, j: (j, 0, 0))],
        out_specs=[pl.BlockSpec((None, tm, d), lambda b, i, j: (b, i, 0)),
                   pl.BlockSpec((tn // PEER_HALF, PEER_NKEYS, tm), lambda b, i, j: (j, 0, b * nt + i))],
        out_shape=[jax.ShapeDtypeStruct((nbatch, seq, d), BF16),
                   jax.ShapeDtypeStruct((nhp, PEER_NKEYS, nbatch * seq), F32)],
        scratch_shapes=[pltpu.VMEM((tm, d), BF16)],
        compiler_params=_params(("parallel", "parallel", "arbitrary")),
        name="peer_q",
    )(x1, mod3, g.reshape(1, d), w_q, keys)


def _top_rows(x, count, one_at_a_time):
    rows = []
    ridx = lax.broadcasted_iota(jnp.int32, x.shape, 0)
    for _ in range(count):
        m = jnp.max(x, axis=0, keepdims=True)
        rows.append(m)
        hit = x == m
        if one_at_a_time:
            first = jnp.min(jnp.where(hit, ridx, x.shape[0]), axis=0, keepdims=True)
            hit = ridx == first
        x = jnp.where(hit, -jnp.inf, x)
    removed = jnp.sum(jnp.where(x == -jnp.inf, 1.0, 0.0), axis=0, keepdims=True)
    return rows, removed


def _peer_group_stats(x0, x1, one_at_a_time):
    half = PEER_TOPK // 2
    v0, r0 = _top_rows(x0, PEER_TOPK, one_at_a_time)
    v1, r1 = _top_rows(x1, PEER_TOPK, one_at_a_time)
    v0_all = jnp.concatenate(v0, axis=0)
    v1_all = jnp.concatenate(v1, axis=0)
    cand = [v0[0] + v1_all]
    cand += [v0[a] + v1_all[:half] for a in range(1, half)]
    cand += [v0_all[half:] + v1[0]]
    top, rc = _top_rows(jnp.concatenate(cand, axis=0), PEER_TOPK, one_at_a_time)
    z = jnp.zeros_like(top[0])
    for t in top:
        z = z + jnp.exp(t - top[0])
    stats = jnp.concatenate([top[-1], v0[0], v1[0], 1.0 / z] + [jnp.zeros_like(z)] * 4, axis=0)
    repeated = jnp.max(jnp.maximum(jnp.maximum(r0, r1), rc)) > float(PEER_TOPK)
    return stats, repeated


HEADS_PER_STEP = 4


def _peer_topk_kernel(st_ref, stats_ref):
    ngroups = st_ref.shape[-1] // LANES

    def group_stats(h, tg, one_at_a_time):
        cols = slice(tg * LANES, (tg + 1) * LANES)
        return _peer_group_stats(st_ref[2 * h, :, cols], st_ref[2 * h + 1, :, cols], one_at_a_time)

    def heads_step(hs, carry):
        work = [(HEADS_PER_STEP * hs + k, tg) for k in range(HEADS_PER_STEP) for tg in range(ngroups)]
        fast = [group_stats(h, tg, False) for h, tg in work]
        for (h, tg), (stats, _) in zip(work, fast):
            stats_ref[h, :, tg * LANES:(tg + 1) * LANES] = stats
        for (h, tg), (_, repeated) in zip(work, fast):
            @pl.when(repeated)
            def _(h=h, tg=tg):
                stats_ref[h, :, tg * LANES:(tg + 1) * LANES] = group_stats(h, tg, True)[0]
        return carry

    lax.fori_loop(0, PEER_HEADS // HEADS_PER_STEP, heads_step, 0)


def _peer_topk(st):
    nhp, nkeys, ntok = st.shape
    tmk = _tile(ntok, 256)
    return pl.pallas_call(
        _peer_topk_kernel,
        grid=(ntok // tmk,),
        in_specs=[pl.BlockSpec((nhp, nkeys, tmk), lambda i: (0, 0, i))],
        out_specs=pl.BlockSpec((PEER_HEADS, 8, tmk), lambda i: (0, 0, i)),
        out_shape=jax.ShapeDtypeStruct((PEER_HEADS, 8, ntok), F32),
        compiler_params=_params(("parallel",)),
        name="peer_topk",
    )(st)


EXPERT_SPLIT = 2


def _peer_dense_kernel(x_ref, u_ref, v_ref, st_ref, stats_ref, o_ref, e1_scr):
    e = pl.program_id(1)
    te, tm = u_ref.shape[0], x_ref.shape[0]
    ni = te // PEER_NKEYS
    hw = te // EXPERT_SPLIT

    @pl.when(e == 0)
    def _():
        o_ref[...] = jnp.zeros_like(o_ref)
        for h in range(PEER_HEADS):
            e1_scr[h] = jnp.exp(st_ref[2 * h + 1] - stats_ref[h, 2:3, :]) * stats_ref[h, 3:4, :]

    def scores(p):
        return lax.dot_general(u_ref[p * hw:(p + 1) * hw, :], x_ref[...], (((1,), (1,)), ((), ())),
                               preferred_element_type=F32)

    def activation(p, a_t):
        cols = []
        for il in range(p * hw // PEER_NKEYS, (p + 1) * hw // PEER_NKEYS):
            i = e * ni + il
            rows = []
            for tc in range(tm // LANES):
                ts = slice(tc * LANES, (tc + 1) * LANES)
                g = jnp.zeros((PEER_NKEYS, LANES), F32)
                for h in range(PEER_HEADS):
                    s0 = st_ref[2 * h, pl.ds(i, 1), :][:, ts]
                    e0 = jnp.exp(s0 - stats_ref[h, 1:2, ts])
                    sel = (s0 + st_ref[2 * h + 1, :, ts]) >= stats_ref[h, 0:1, ts]
                    g = g + jnp.where(sel, e0 * e1_scr[h, :, ts], 0.0)
                r0 = il * PEER_NKEYS - p * hw
                a = a_t[r0:r0 + PEER_NKEYS, ts]
                act = 0.5 * a * (1.0 + lax.erf(a * (2.0 ** -0.5))) * g
                rows.append(act.T.astype(BF16))
            cols.append(jnp.concatenate(rows, axis=0))
        return jnp.concatenate(cols, axis=1)

    a_parts = [scores(p) for p in range(EXPERT_SPLIT)]
    for p in range(EXPERT_SPLIT):
        o_ref[...] += jnp.dot(activation(p, a_parts[p]), v_ref[p * hw:(p + 1) * hw, :],
                              preferred_element_type=F32)


def _peer_dense(h2, u_tab, v_tab, st, stats):
    ntok, d = h2.shape
    nexp = u_tab.shape[0]
    tm, te = _tile(ntok, 512), 4 * PEER_NKEYS
    nhp = st.shape[0]
    return pl.pallas_call(
        _peer_dense_kernel,
        grid=(ntok // tm, nexp // te),
        in_specs=[pl.BlockSpec((tm, d), lambda i, e: (i, 0)),
                  pl.BlockSpec((te, d), lambda i, e: (e, 0)),
                  pl.BlockSpec((te, d), lambda i, e: (e, 0)),
                  pl.BlockSpec((nhp, PEER_NKEYS, tm), lambda i, e: (0, 0, i)),
                  pl.BlockSpec((PEER_HEADS, 8, tm), lambda i, e: (0, 0, i))],
        out_specs=pl.BlockSpec((tm, d), lambda i, e: (i, 0)),
        out_shape=jax.ShapeDtypeStruct((ntok, d), F32),
        scratch_shapes=[pltpu.VMEM((PEER_HEADS, PEER_NKEYS, tm), F32)],
        compiler_params=_params(("parallel", "arbitrary")),
        name="peer_dense",
    )(h2, u_tab, v_tab, st, stats)


def _final_kernel(x_ref, p_ref, mod_ref, g_ref, o_ref, *, normalize):
    y = x_ref[...] + mod_ref[5:6, :] * p_ref[...]
    if normalize:
        y = (y * lax.rsqrt(jnp.mean(y * y, axis=-1, keepdims=True) + NORM_EPS)) * g_ref[...]
    o_ref[...] = y


def _final(x1, peer_out, mod3, g, normalize):
    nbatch, seq, d = x1.shape
    tm = _tile(seq, 256)
    spec = pl.BlockSpec((None, tm, d), lambda b, i: (b, i, 0))
    return pl.pallas_call(
        functools.partial(_final_kernel, normalize=normalize),
        grid=(nbatch, seq // tm),
        in_specs=[spec, spec,
                  pl.BlockSpec((None, ADA_CHUNKS, d), lambda b, i: (b, 0, 0)),
                  pl.BlockSpec((1, d), lambda b, i: (0, 0))],
        out_specs=spec,
        out_shape=jax.ShapeDtypeStruct((nbatch, seq, d), F32),
        compiler_params=_params(("parallel", "parallel")),
        name="final",
    )(x1, peer_out, mod3, g.reshape(1, d))


def kernel(x, c, positions, ln1_g, ln2_g, w_ada, b_ada, w_in, w_out, peer_wq, peer_sub_keys, peer_u,
           peer_v, lnf_g):
    nbatch, seq, d = x.shape
    depth = w_ada.shape[0]
    tables = _rope_tables(positions)
    for layer in range(depth):
        mod3 = _ada(c, w_ada[layer], b_ada[layer]).reshape(nbatch, ADA_CHUNKS, d)
        pa, proj, aux = _in_proj(x, mod3, ln1_g[layer], _prep_w_in(w_in[layer]), tables)
        o_a = _attn_a(pa)
        o_b, (w_out_b, w_q_b, u_b, v_b) = _dsa(
            proj, aux, (w_out[layer], peer_wq[layer], peer_u[layer], peer_v[layer]))
        x = _out_proj(o_a, o_b, w_out_b, x, mod3)
        keys = peer_sub_keys[layer].reshape(2 * PEER_HEADS, PEER_NKEYS, PEER_HALF)
        h2, st = _peer_q(x, mod3, ln2_g[layer], w_q_b, keys)
        stats = _peer_topk(st)
        peer_out = _peer_dense(h2.reshape(nbatch * seq, d), u_b, v_b, st, stats)
        x = _final(x, peer_out.reshape(nbatch, seq, d), mod3, lnf_g, normalize=layer + 1 == depth)
    return x
```

```python
import functools
import math

import jax
import jax.numpy as jnp
from jax import lax
from jax.experimental import pallas as pl
from jax.experimental.pallas import tpu as pltpu

F32 = jnp.float32
BF16 = jnp.bfloat16

HEAD_DIM = 128
A_HEADS = 16
A_KV_HEADS = 4
A_GROUP = A_HEADS // A_KV_HEADS
DILATED_BRANCHES = ((128, 1), (512, 4), (2048, 16))
WIN_BLK = 128
B_HEADS = 16
IDX_HEADS = 16
IDX_DIM = 64
DSA_TOPK_MAX = 256
Q_BLK = 128
PEER_HEADS = 8
PEER_NKEYS = 128
PEER_HALF = 128
PEER_TOPK = 16
ROPE_THETA = 10000.0
NORM_EPS = 1e-6
NEG = -1e30
ADA_CHUNKS = 6
INT_MIN = -(2 ** 31)

LANES = 128
VMEM_LIMIT = 56 * 1024 * 1024

PROJ_TILE = 512
IN_COLS = ((A_HEADS + 2 * A_KV_HEADS + B_HEADS + 2) * HEAD_DIM + IDX_HEADS * IDX_DIM + IDX_DIM + IDX_HEADS)
PROJ_TILES = -(-IN_COLS // PROJ_TILE)
A_TILES = (A_HEADS + 2 * A_KV_HEADS) * HEAD_DIM // PROJ_TILE
KA_TILE = A_HEADS * HEAD_DIM // PROJ_TILE
VA_TILE = KA_TILE + 1
PA_HEADS = A_HEADS + 2 * A_KV_HEADS
QB_OFF = 0
KB_OFF = QB_OFF + B_HEADS * HEAD_DIM
VB_OFF = KB_OFF + HEAD_DIM
QI_OFF = VB_OFF + HEAD_DIM
KIKI_OFF = QI_OFF + IDX_HEADS * IDX_DIM
P_COLS = (PROJ_TILES - A_TILES) * PROJ_TILE
WI_LANE = IDX_DIM


def _params(semantics):
    return pltpu.CompilerParams(dimension_semantics=semantics, vmem_limit_bytes=VMEM_LIMIT)


def _tile(n, pref):
    return pref if n % pref == 0 else n


def _norm_mod(x, g, scale, shift):
    xf = x.astype(F32)
    y = xf * lax.rsqrt(jnp.mean(xf * xf, axis=-1, keepdims=True) + NORM_EPS)
    return (y * g) * (1.0 + scale) + shift


def _ada_kernel(ct_ref, w_ref, b_ref, o_ref, *, nbatch):
    @pl.when(pl.program_id(1) == 0)
    def _():
        o_ref[...] = jnp.broadcast_to(b_ref[...], o_ref.shape)

    ct = ct_ref[...]
    s = ct * jax.nn.sigmoid(ct)
    w = w_ref[...]
    rows = [jnp.sum(w * s[:, b:b + 1], axis=0, keepdims=True) for b in range(nbatch)]
    o_ref[...] += jnp.concatenate(rows, axis=0)


def _ada(c, w_ada, b_ada):
    nbatch, d = c.shape
    n = w_ada.shape[1]
    tk, tn = _tile(d, 1024), _tile(n, 2048)
    return pl.pallas_call(
        functools.partial(_ada_kernel, nbatch=nbatch),
        grid=(n // tn, d // tk),
        in_specs=[pl.BlockSpec((tk, nbatch), lambda j, k: (k, 0)),
                  pl.BlockSpec((tk, tn), lambda j, k: (k, j)),
                  pl.BlockSpec((1, tn), lambda j, k: (0, j))],
        out_specs=pl.BlockSpec((nbatch, tn), lambda j, k: (0, j)),
        out_shape=jax.ShapeDtypeStruct((nbatch, n), F32),
        compiler_params=_params(("parallel", "arbitrary")),
        name="ada",
    )(c.T, w_ada, b_ada.reshape(1, n))


def _rope_kernel(pos_ref, ch_ref, sh_ref, ci_ref, sa_ref, sb_ref):
    pos = pos_ref[...].astype(F32)
    lane = lax.broadcasted_iota(jnp.int32, (1, LANES), 1)
    expo_h = -((2 * (lane & 63)).astype(F32)) / HEAD_DIM
    expo_i = -((2 * (lane & 31)).astype(F32)) / IDX_DIM
    inv = jnp.power(ROPE_THETA, jnp.where(lane < 64, expo_h, expo_i))
    ang = pos * inv
    c = jnp.cos(ang)
    s = jnp.sin(ang)
    ch_ref[...] = jnp.where(lane < 64, c, pltpu.roll(c, 64, 1))
    sh_ref[...] = jnp.where(lane < 64, -s, pltpu.roll(s, 64, 1))
    grp = lane >> 5
    c64, c96, c32 = pltpu.roll(c, 64, 1), pltpu.roll(c, 96, 1), pltpu.roll(c, 32, 1)
    s64, s96, s32 = pltpu.roll(s, 64, 1), pltpu.roll(s, 96, 1), pltpu.roll(s, 32, 1)
    ci_ref[...] = jnp.where(grp == 0, c64, jnp.where(grp == 1, c96, jnp.where(grp == 2, c, c32)))
    sa_ref[...] = jnp.where(grp == 0, -s64, jnp.where(grp == 2, -s, 0.0))
    sb_ref[...] = jnp.where(grp == 1, s96, jnp.where(grp == 3, s32, 0.0))


def _rope_tables(positions):
    nbatch, seq = positions.shape
    ts = _tile(seq, 1024)
    spec = pl.BlockSpec((None, ts, LANES), lambda b, i: (b, i, 0))
    shape = jax.ShapeDtypeStruct((nbatch, seq, LANES), F32)
    return pl.pallas_call(
        _rope_kernel,
        grid=(nbatch, seq // ts),
        in_specs=[pl.BlockSpec((None, ts, 1), lambda b, i: (b, i, 0))],
        out_specs=[spec] * 5,
        out_shape=[shape] * 5,
        compiler_params=_params(("parallel", "parallel")),
        name="rope",
    )(positions.reshape(nbatch, seq, 1))


def _rope_head(a, ch, sh):
    return a * ch + pltpu.roll(a, 64, 1) * sh


def _rope_idx(a, ci, sa, sb):
    return a * ci + pltpu.roll(a, 96, 1) * sa + pltpu.roll(a, 32, 1) * sb


def _in_proj_kernel(x_ref, mod_ref, g_ref, w_ref, wt_ref, ch_ref, sh_ref, ci_ref, sa_ref, sb_ref,
                    pa_ref, p_ref, aux_ref, h_scr):
    j = pl.program_id(2)

    @pl.when(j == 0)
    def _():
        h_scr[...] = _norm_mod(x_ref[...], g_ref[...], mod_ref[1:2, :], mod_ref[0:1, :]).astype(BF16)

    nblk = PROJ_TILE // LANES
    qscale = HEAD_DIM ** -0.5 * math.log2(math.e)
    qb_tile = A_TILES + QB_OFF // PROJ_TILE
    mix_tile = A_TILES + KB_OFF // PROJ_TILE
    last_tile = PROJ_TILES - 1

    def project(w_tile_ref):
        width = 2 * LANES
        halves = [lax.dot_general(h_scr[...], w_tile_ref[p * width:(p + 1) * width, :],
                                  (((1,), (1,)), ((), ())), preferred_element_type=F32)
                  for p in range(PROJ_TILE // width)]

        def blk(q):
            return halves[q // 2][:, (q % 2) * LANES:(q % 2 + 1) * LANES]
        return blk

    def put(q, val):
        p_ref[:, q * LANES:(q + 1) * LANES] = val.astype(BF16)

    def head(a):
        return _rope_head(a, ch_ref[...], sh_ref[...])

    def idx(a):
        return _rope_idx(a, ci_ref[...], sa_ref[...], sb_ref[...])

    @pl.when(j < KA_TILE)
    def _():
        blk = project(w_ref)
        for q in range(nblk):
            pa_ref[q] = head(blk(q)) * qscale

    @pl.when(j == KA_TILE)
    def _():
        blk = project(w_ref)
        for q in range(nblk):
            pa_ref[q] = head(blk(q))

    @pl.when(j == VA_TILE)
    def _():
        blk = project(w_ref)
        for q in range(nblk):
            pa_ref[q] = blk(q)

    @pl.when((j >= qb_tile) & (j < mix_tile))
    def _():
        blk = project(w_ref)
        for q in range(nblk):
            put(q, head(blk(q)) * qscale)

    @pl.when(j == mix_tile)
    def _():
        blk = project(w_ref)
        put(0, head(blk(0)))
        put(1, blk(1))
        put(2, idx(blk(2)))
        put(3, idx(blk(3)))

    @pl.when((j > mix_tile) & (j < last_tile))
    def _():
        blk = project(w_ref)
        for q in range(nblk):
            put(q, idx(blk(q)))

    @pl.when(j == last_tile)
    def _():
        blk = project(wt_ref)
        put(0, idx(blk(0)))
        put(1, idx(blk(1)))
        raw = blk(2)
        ki = idx(raw)
        lane = lax.broadcasted_iota(jnp.int32, ki.shape, 1)
        put(2, jnp.where(lane < IDX_DIM, ki, pltpu.roll(ki, IDX_DIM, 1)))
        put(3, blk(3))
        aux_ref[...] = raw


def _prep_w_in(w_in):
    assert w_in.shape[1] == IN_COLS
    w_t = jnp.swapaxes(w_in, 0, 1).astype(BF16)
    full = (PROJ_TILES - 1) * PROJ_TILE
    return w_t, jnp.pad(w_t[full:], ((0, PROJ_TILES * PROJ_TILE - IN_COLS), (0, 0)))


def _in_proj(x, mod3, g, w_parts, tables):
    nbatch, seq, d = x.shape
    tm = _tile(seq, 512)
    nblk = PROJ_TILE // LANES
    w_t, w_tail = w_parts
    tab_spec = pl.BlockSpec((None, tm, LANES), lambda b, i, j: (b, i, 0))
    return pl.pallas_call(
        _in_proj_kernel,
        grid=(nbatch, seq // tm, PROJ_TILES),
        in_specs=[pl.BlockSpec((None, tm, d), lambda b, i, j: (b, i, 0)),
                  pl.BlockSpec((None, ADA_CHUNKS, d), lambda b, i, j: (b, 0, 0)),
                  pl.BlockSpec((1, d), lambda b, i, j: (0, 0)),
                  pl.BlockSpec((PROJ_TILE, d), lambda b, i, j: (jnp.minimum(j, PROJ_TILES - 2), 0)),
                  pl.BlockSpec((PROJ_TILE, d), lambda b, i, j: (0, 0))] + [tab_spec] * 5,
        out_specs=[pl.BlockSpec((None, nblk, tm, LANES), lambda b, i, j: (b, jnp.minimum(j, A_TILES - 1), i, 0)),
                   pl.BlockSpec((None, tm, PROJ_TILE), lambda b, i, j: (b, i, jnp.maximum(j - A_TILES, 0))),
                   pl.BlockSpec((None, tm, LANES), lambda b, i, j: (b, i, 0))],
        out_shape=[jax.ShapeDtypeStruct((nbatch, PA_HEADS, seq, LANES), F32),
                   jax.ShapeDtypeStruct((nbatch, seq, P_COLS), BF16),
                   jax.ShapeDtypeStruct((nbatch, seq, LANES), F32)],
        scratch_shapes=[pltpu.VMEM((tm, d), BF16)],
        compiler_params=_params(("parallel", "parallel", "arbitrary")),
        name="in_proj",
    )(x, mod3, g.reshape(1, d), w_t, w_tail, *tables)


def _attn_a_kernel(q_ref, k_ref, v_ref, o_ref, acc_scr, m_scr, l_scr, bias_scr, *, dilations):
    half = pl.program_id(2)
    qrows = q_ref.shape[1]
    nblk = qrows // WIN_BLK

    rows = A_GROUP * WIN_BLK
    row = lax.broadcasted_iota(jnp.int32, (rows, 2 * WIN_BLK), 0) & (WIN_BLK - 1)
    col = lax.broadcasted_iota(jnp.int32, (rows, 2 * WIN_BLK), 1)
    band = (col >= row) & (col <= row + WIN_BLK)
    bias_scr[0] = jnp.where(band & (col >= WIN_BLK), 0.0, NEG)
    bias_scr[1] = jnp.where(band, 0.0, NEG)

    def rows_at(start, dil):
        return pl.ds(start, WIN_BLK) if dil == 1 else pl.ds(start, WIN_BLK, stride=dil)

    for idx, dil in enumerate(dilations):
        per_res = nblk // dil
        first, last = idx == 0, idx == len(dilations) - 1

        def body(blk, carry, dil=dil, per_res=per_res, first=first, last=last):
            r = blk // per_res
            n_loc = blk - r * per_res
            n = half * per_res + n_loc
            q0 = r + dil * WIN_BLK * n_loc
            k0 = r + dil * WIN_BLK * n
            kp = r + dil * WIN_BLK * jnp.maximum(n - 1, 0)
            q4 = jnp.concatenate([q_ref[g, rows_at(q0, dil), :] for g in range(A_GROUP)], axis=0).astype(BF16)
            kc = jnp.concatenate([k_ref[rows_at(kp, dil), :], k_ref[rows_at(k0, dil), :]], axis=0).astype(BF16)
            vc = jnp.concatenate([v_ref[rows_at(kp, dil), :], v_ref[rows_at(k0, dil), :]], axis=0).astype(BF16)
            s = lax.dot_general(q4, kc, (((1,), (1,)), ((), ())), preferred_element_type=F32)
            s = s + bias_scr[jnp.minimum(n, 1)]
            m = jnp.max(s, axis=-1, keepdims=True)
            p = jnp.exp2(s - m)
            l = jnp.sum(p, axis=-1, keepdims=True)
            o = jnp.dot(p.astype(BF16), vc, preferred_element_type=F32)
            for g in range(A_GROUP):
                og = o[g * WIN_BLK:(g + 1) * WIN_BLK]
                mg = jnp.broadcast_to(m[g * WIN_BLK:(g + 1) * WIN_BLK], (WIN_BLK, HEAD_DIM))
                lg = jnp.broadcast_to(l[g * WIN_BLK:(g + 1) * WIN_BLK], (WIN_BLK, HEAD_DIM))
                if not first:
                    mp = m_scr[g, rows_at(q0, dil), :]
                    decay = jnp.exp2(-jnp.abs(mg - mp))
                    w_new = jnp.where(mg >= mp, 1.0, decay)
                    w_old = jnp.where(mg >= mp, decay, 1.0)
                    og = acc_scr[g, rows_at(q0, dil), :] * w_old + og * w_new
                    lg = l_scr[g, rows_at(q0, dil), :] * w_old + lg * w_new
                    mg = jnp.maximum(mg, mp)
                if last:
                    o_ref[pl.ds(pl.multiple_of(q0, WIN_BLK), WIN_BLK), g * HEAD_DIM:(g + 1) * HEAD_DIM] = (
                        (og / lg).astype(o_ref.dtype))
                else:
                    acc_scr[g, rows_at(q0, dil), :] = og
                    m_scr[g, rows_at(q0, dil), :] = mg
                    l_scr[g, rows_at(q0, dil), :] = lg
            return carry

        lax.fori_loop(0, nblk, body, 0, unroll=4)


def _attn_a(pa):
    nbatch, _, seq, _ = pa.shape
    dilations = tuple(sorted((dil for _, dil in DILATED_BRANCHES), reverse=True))
    assert dilations[-1] == 1 and all(w // dil == WIN_BLK for w, dil in DILATED_BRANCHES)
    nhalf = 2 if seq % (2 * WIN_BLK * dilations[0]) == 0 else 1
    qrows = seq // nhalf
    assert qrows % (WIN_BLK * dilations[0]) == 0
    return pl.pallas_call(
        functools.partial(_attn_a_kernel, dilations=dilations),
        grid=(nbatch, A_KV_HEADS, nhalf),
        in_specs=[pl.BlockSpec((None, A_GROUP, qrows, HEAD_DIM), lambda b, h, t: (b, h, t, 0)),
                  pl.BlockSpec((None, None, seq, HEAD_DIM), lambda b, h, t: (b, A_HEADS + h, 0, 0)),
                  pl.BlockSpec((None, None, seq, HEAD_DIM),
                               lambda b, h, t: (b, A_HEADS + A_KV_HEADS + h, 0, 0))],
        out_specs=pl.BlockSpec((None, qrows, A_GROUP * HEAD_DIM), lambda b, h, t: (b, t, h)),
        out_shape=jax.ShapeDtypeStruct((nbatch, seq, A_HEADS * HEAD_DIM), BF16),
        scratch_shapes=[pltpu.VMEM((A_GROUP, qrows, HEAD_DIM), F32),
                        pltpu.VMEM((A_GROUP, qrows, HEAD_DIM), F32),
                        pltpu.VMEM((A_GROUP, qrows, HEAD_DIM), F32),
                        pltpu.VMEM((2, A_GROUP * WIN_BLK, 2 * WIN_BLK), F32)],
        compiler_params=_params(("parallel", "parallel", "arbitrary")),
        name="attn_a",
    )(pa, pa, pa)


KEY_CHUNK = 256


def _ordered_to_float(key):
    return pltpu.bitcast(key ^ ((key >> 31) & jnp.int32(0x7FFFFFFF)), F32)


LOWEST_FINITE_KEY = INT_MIN + 0x00800000


def _dsa_kernel(p_ref, wi_ref, k_ref, kk_ref, v_ref, *rest, topk, ncast):
    cast_in, rest = rest[:ncast], rest[ncast:]
    o_ref, cast_out = rest[0], rest[1:1 + ncast]
    sc_scr, wib_scr, q_scr, p_scr, m_scr, l_scr, acc_scr = rest[1 + ncast:]
    i = pl.program_id(1)
    nchunks = (i + 2) // 2
    halves = KEY_CHUNK // LANES

    for src, dst in zip(cast_in, cast_out):
        dst[...] = src[...].astype(dst.dtype)

    wi = wi_ref[...] * (IDX_HEADS ** -0.5 * IDX_DIM ** -0.5)
    lane = lax.broadcasted_iota(jnp.int32, (Q_BLK, LANES), 1)
    for h in range(IDX_HEADS):
        blk = p_ref[:, QI_OFF + (h // 2) * LANES:QI_OFF + (h // 2 + 1) * LANES]
        keep = (lane < IDX_DIM) if h % 2 == 0 else (lane >= IDX_DIM)
        q_scr[h * Q_BLK:(h + 1) * Q_BLK, :] = jnp.where(keep, blk, jnp.zeros_like(blk))
        wib_scr[h] = jnp.broadcast_to(wi[:, WI_LANE + h:WI_LANE + h + 1], (Q_BLK, LANES))
    qpos = i * Q_BLK + lax.broadcasted_iota(jnp.int32, (Q_BLK, KEY_CHUNK), 0)
    kiota = lax.broadcasted_iota(jnp.int32, (Q_BLK, KEY_CHUNK), 1)

    def score_chunk(c, carry):
        k0 = pl.multiple_of(c * KEY_CHUNK, KEY_CHUNK)
        z = lax.dot_general(q_scr[...], kk_ref[pl.ds(k0, KEY_CHUNK), :], (((1,), (1,)), ((), ())),
                            preferred_element_type=F32)
        sc = jnp.zeros((Q_BLK, KEY_CHUNK), F32)
        for h in range(IDX_HEADS):
            w = wib_scr[h]
            sc = sc + jnp.concatenate([w] * halves, axis=1) * jnp.maximum(z[h * Q_BLK:(h + 1) * Q_BLK], 0.0)
        sc_scr[c] = jnp.where(k0 + kiota <= qpos, sc, -jnp.inf)
        return carry

    lax.fori_loop(0, nchunks, score_chunk, 0)

    def bit_body(t, ans):
        trial = ans | jnp.left_shift(jnp.int32(1), 31 - t)
        thr = _ordered_to_float(trial ^ jnp.int32(INT_MIN))

        def count_chunk(c, acc):
            sc = sc_scr[c]
            for q in range(halves):
                acc = acc + jnp.where(sc[:, q * LANES:(q + 1) * LANES] >= thr, 1.0, 0.0)
            return acc

        acc = lax.fori_loop(0, nchunks, count_chunk, jnp.zeros((Q_BLK, LANES), F32))
        cnt = jnp.sum(acc, axis=1, keepdims=True)
        return jnp.where(cnt >= float(topk), trial, ans)

    ans = lax.fori_loop(0, 32, bit_body, jnp.zeros((Q_BLK, LANES), jnp.int32))
    tau = _ordered_to_float(jnp.maximum(ans ^ jnp.int32(INT_MIN), jnp.int32(LOWEST_FINITE_KEY)))
    tau2 = jnp.concatenate([tau] * halves, axis=1)

    for h in range(B_HEADS):
        q_scr[h * Q_BLK:(h + 1) * Q_BLK, :] = p_ref[:, QB_OFF + h * HEAD_DIM:QB_OFF + (h + 1) * HEAD_DIM]
    m_scr[...] = jnp.full(m_scr.shape, NEG, F32)
    l_scr[...] = jnp.zeros(l_scr.shape, F32)
    acc_scr[...] = jnp.zeros(acc_scr.shape, F32)

    def att_chunk(c, carry):
        k0 = pl.multiple_of(c * KEY_CHUNK, KEY_CHUNK)
        s = lax.dot_general(q_scr[...], k_ref[pl.ds(k0, KEY_CHUNK), :], (((1,), (1,)), ((), ())),
                            preferred_element_type=F32)
        bias = jnp.where(sc_scr[c] >= tau2, 0.0, NEG)
        for h in range(B_HEADS):
            hr = slice(h * Q_BLK, (h + 1) * Q_BLK)
            sh = s[hr] + bias
            m_old = m_scr[hr, :]
            m_new = jnp.maximum(m_old, jnp.max(sh, axis=-1, keepdims=True))
            alpha = jnp.exp2(m_old - m_new)
            p = jnp.exp2(sh - jnp.concatenate([m_new] * halves, axis=1))
            psum = p[:, :LANES]
            for q in range(1, halves):
                psum = psum + p[:, q * LANES:(q + 1) * LANES]
            l_scr[hr, :] = alpha * l_scr[hr, :] + psum
            m_scr[hr, :] = m_new
            p_scr[hr, :] = p.astype(BF16)
            acc_scr[hr, :] = alpha * acc_scr[hr, :]
        acc_scr[...] += jnp.dot(p_scr[...], v_ref[pl.ds(k0, KEY_CHUNK), :], preferred_element_type=F32)
        return carry

    lax.fori_loop(0, nchunks, att_chunk, 0)
    for h in range(B_HEADS):
        hr = slice(h * Q_BLK, (h + 1) * Q_BLK)
        l = jnp.sum(l_scr[hr, :], axis=-1, keepdims=True)
        o_ref[:, h * HEAD_DIM:(h + 1) * HEAD_DIM] = (acc_scr[hr, :] / l).astype(o_ref.dtype)


def _dsa(proj, aux, f32_weights):
    nbatch, seq, _ = proj.shape
    assert seq % KEY_CHUNK == 0
    topk = min(DSA_TOPK_MAX, seq // 4)
    qw = B_HEADS * HEAD_DIM
    nq = seq // Q_BLK
    steps = nbatch * nq
    full = lambda off: pl.BlockSpec((None, seq, LANES), lambda b, i: (b, 0, off // LANES))
    cast_specs = []
    for w in f32_weights:
        assert w.shape[0] % steps == 0
        cast_specs.append(pl.BlockSpec((w.shape[0] // steps, w.shape[1]), lambda b, i: (b * nq + i, 0)))
    outs = pl.pallas_call(
        functools.partial(_dsa_kernel, topk=topk, ncast=len(f32_weights)),
        grid=(nbatch, nq),
        in_specs=[pl.BlockSpec((None, Q_BLK, P_COLS), lambda b, i: (b, i, 0)),
                  pl.BlockSpec((None, Q_BLK, LANES), lambda b, i: (b, i, 0)),
                  full(KB_OFF), full(KIKI_OFF), full(VB_OFF)] + cast_specs,
        out_specs=[pl.BlockSpec((None, Q_BLK, qw), lambda b, i: (b, i, 0))] + cast_specs,
        out_shape=[jax.ShapeDtypeStruct((nbatch, seq, qw), BF16)]
        + [jax.ShapeDtypeStruct(w.shape, BF16) for w in f32_weights],
        scratch_shapes=[pltpu.VMEM((seq // KEY_CHUNK, Q_BLK, KEY_CHUNK), F32),
                        pltpu.VMEM((IDX_HEADS, Q_BLK, LANES), F32),
                        pltpu.VMEM((B_HEADS * Q_BLK, HEAD_DIM), BF16),
                        pltpu.VMEM((B_HEADS * Q_BLK, KEY_CHUNK), BF16),
                        pltpu.VMEM((B_HEADS * Q_BLK, LANES), F32),
                        pltpu.VMEM((B_HEADS * Q_BLK, LANES), F32),
                        pltpu.VMEM((B_HEADS * Q_BLK, HEAD_DIM), F32)],
        compiler_params=_params(("parallel", "arbitrary")),
        name="dsa",
    )(proj, aux, proj, proj, proj, *f32_weights)
    return outs[0], outs[1:]


def _out_proj_kernel(oa_ref, ob_ref, w_ref, x_ref, mod_ref, o_ref):
    ka = oa_ref.shape[-1]
    y = jnp.dot(oa_ref[...], w_ref[:ka, :], preferred_element_type=F32)
    y = y + jnp.dot(ob_ref[...], w_ref[ka:, :], preferred_element_type=F32)
    o_ref[...] = x_ref[...] + mod_ref[2:3, :] * y


def _out_proj(o_a, o_b, w_out, x, mod3):
    nbatch, seq, d = x.shape
    ka, kb = o_a.shape[-1], o_b.shape[-1]
    tm, tn = _tile(seq, 1024), _tile(d, 512)
    return pl.pallas_call(
        _out_proj_kernel,
        grid=(nbatch, seq // tm, d // tn),
        in_specs=[pl.BlockSpec((None, tm, ka), lambda b, i, j: (b, i, 0)),
                  pl.BlockSpec((None, tm, kb), lambda b, i, j: (b, i, 0)),
                  pl.BlockSpec((ka + kb, tn), lambda b, i, j: (0, j)),
                  pl.BlockSpec((None, tm, tn), lambda b, i, j: (b, i, j)),
                  pl.BlockSpec((None, ADA_CHUNKS, tn), lambda b, i, j: (b, 0, j))],
        out_specs=pl.BlockSpec((None, tm, tn), lambda b, i, j: (b, i, j)),
        out_shape=jax.ShapeDtypeStruct((nbatch, seq, d), F32),
        compiler_params=_params(("parallel", "parallel", "parallel")),
        name="out_proj",
    )(o_a, o_b, w_out, x, mod3)


def _peer_q_kernel(x_ref, mod_ref, g_ref, w_ref, keys_ref, h_ref, st_ref, h_scr):
    j = pl.program_id(2)

    @pl.when(j == 0)
    def _():
        h = _norm_mod(x_ref[...], g_ref[...], mod_ref[4:5, :], mod_ref[3:4, :]).astype(BF16)
        h_scr[...] = h
        h_ref[...] = h

    q = jnp.dot(h_scr[...], w_ref[...], preferred_element_type=F32)
    for blk in range(q.shape[1] // PEER_HALF):
        qh = q[:, blk * PEER_HALF:(blk + 1) * PEER_HALF]
        st_ref[blk] = lax.dot_general(keys_ref[blk], qh, (((1,), (1,)), ((), ())),
                                      preferred_element_type=F32, precision=lax.Precision.HIGHEST)


def _peer_q(x1, mod3, g, w_q, keys):
    nbatch, seq, d = x1.shape
    nq = w_q.shape[1]
    tm, tn = _tile(seq, 512), 4 * PEER_HALF
    nhp = nq // PEER_HALF
    nt = seq // tm
    return pl.pallas_call(
        _peer_q_kernel,
        grid=(nbatch, nt, nq // tn),
        in_specs=[pl.BlockSpec((None, tm, d), lambda b, i, j: (b, i, 0)),
                  pl.BlockSpec((None, ADA_CHUNKS, d), lambda b, i, j: (b, 0, 0)),
                  pl.BlockSpec((1, d), lambda b, i, j: (0, 0)),
                  pl.BlockSpec((d, tn), lambda b, i, j: (0, j)),
                  pl.BlockSpec((tn // PEER_HALF, PEER_NKEYS, PEER_HALF), lambda b, i, j: (j, 0, 0))],
        out_specs=[pl.BlockSpec((None, tm, d), lambda b, i, j: (b, i, 0)),
                   pl.BlockSpec((tn // PEER_HALF, PEER_NKEYS, tm), lambda b, i, j: (j, 0, b * nt + i))],
        out_shape=[jax.ShapeDtypeStruct((nbatch, seq, d), BF16),
                   jax.ShapeDtypeStruct((nhp, PEER_NKEYS, nbatch * seq), F32)],
        scratch_shapes=[pltpu.VMEM((tm, d), BF16)],
        compiler_params=_params(("parallel", "parallel", "arbitrary")),
        name="peer_q",
    )(x1, mod3, g.reshape(1, d), w_q, keys)


def _top_rows(x, count, one_at_a_time):
    rows = []
    ridx = lax.broadcasted_iota(jnp.int32, x.shape, 0)
    for _ in range(count):
        m = jnp.max(x, axis=0, keepdims=True)
        rows.append(m)
        hit = x == m
        if one_at_a_time:
            first = jnp.min(jnp.where(hit, ridx, x.shape[0]), axis=0, keepdims=True)
            hit = ridx == first
        x = jnp.where(hit, -jnp.inf, x)
    removed = jnp.sum(jnp.where(x == -jnp.inf, 1.0, 0.0), axis=0, keepdims=True)
    return rows, removed


def _peer_group_stats(x0, x1, one_at_a_time):
    half = PEER_TOPK // 2
    v0, r0 = _top_rows(x0, PEER_TOPK, one_at_a_time)
    v1, r1 = _top_rows(x1, PEER_TOPK, one_at_a_time)
    v0_all = jnp.concatenate(v0, axis=0)
    v1_all = jnp.concatenate(v1, axis=0)
    cand = [v0[0] + v1_all]
    cand += [v0[a] + v1_all[:half] for a in range(1, half)]
    cand += [v0_all[half:] + v1[0]]
    top, rc = _top_rows(jnp.concatenate(cand, axis=0), PEER_TOPK, one_at_a_time)
    z = jnp.zeros_like(top[0])
    for t in top:
        z = z + jnp.exp(t - top[0])
    stats = jnp.concatenate([top[-1], v0[0], v1[0], 1.0 / z] + [jnp.zeros_like(z)] * 4, axis=0)
    repeated = jnp.max(jnp.maximum(jnp.maximum(r0, r1), rc)) > float(PEER_TOPK)
    return stats, repeated


HEADS_PER_STEP = 4


def _peer_topk_kernel(st_ref, stats_ref):
    ngroups = st_ref.shape[-1] // LANES

    def group_stats(h, tg, one_at_a_time):
        cols = slice(tg * LANES, (tg + 1) * LANES)
        return _peer_group_stats(st_ref[2 * h, :, cols], st_ref[2 * h + 1, :, cols], one_at_a_time)

    def heads_step(hs, carry):
        work = [(HEADS_PER_STEP * hs + k, tg) for k in range(HEADS_PER_STEP) for tg in range(ngroups)]
        fast = [group_stats(h, tg, False) for h, tg in work]
        for (h, tg), (stats, _) in zip(work, fast):
            stats_ref[h, :, tg * LANES:(tg + 1) * LANES] = stats
        for (h, tg), (_, repeated) in zip(work, fast):
            @pl.when(repeated)
            def _(h=h, tg=tg):
                stats_ref[h, :, tg * LANES:(tg + 1) * LANES] = group_stats(h, tg, True)[0]
        return carry

    lax.fori_loop(0, PEER_HEADS // HEADS_PER_STEP, heads_step, 0)


def _peer_topk(st):
    nhp, nkeys, ntok = st.shape
    tmk = _tile(ntok, 256)
    return pl.pallas_call(
        _peer_topk_kernel,
        grid=(ntok // tmk,),
        in_specs=[pl.BlockSpec((nhp, nkeys, tmk), lambda i: (0, 0, i))],
        out_specs=pl.BlockSpec((PEER_HEADS, 8, tmk), lambda i: (0, 0, i)),
        out_shape=jax.ShapeDtypeStruct((PEER_HEADS, 8, ntok), F32),
        compiler_params=_params(("parallel",)),
        name="peer_topk",
    )(st)


EXPERT_SPLIT = 2


def _peer_dense_kernel(x_ref, u_ref, v_ref, st_ref, stats_ref, o_ref, e1_scr):
    e = pl.program_id(1)
    te, tm = u_ref.shape[0], x_ref.shape[0]
    ni = te // PEER_NKEYS
    hw = te // EXPERT_SPLIT

    @pl.when(e == 0)
    def _():
        o_ref[...] = jnp.zeros_like(o_ref)
        for h in range(PEER_HEADS):
            e1_scr[h] = jnp.exp(st_ref[2 * h + 1] - stats_ref[h, 2:3, :]) * stats_ref[h, 3:4, :]

    def scores(p):
        return lax.dot_general(u_ref[p * hw:(p + 1) * hw, :], x_ref[...], (((1,), (1,)), ((), ())),
                               preferred_element_type=F32)

    def activation(p, a_t):
        cols = []
        for il in range(p * hw // PEER_NKEYS, (p + 1) * hw // PEER_NKEYS):
            i = e * ni + il
            rows = []
            for tc in range(tm // LANES):
                ts = slice(tc * LANES, (tc + 1) * LANES)
                g = jnp.zeros((PEER_NKEYS, LANES), F32)
                for h in range(PEER_HEADS):
                    s0 = st_ref[2 * h, pl.ds(i, 1), :][:, ts]
                    e0 = jnp.exp(s0 - stats_ref[h, 1:2, ts])
                    sel = (s0 + st_ref[2 * h + 1, :, ts]) >= stats_ref[h, 0:1, ts]
                    g = g + jnp.where(sel, e0 * e1_scr[h, :, ts], 0.0)
                r0 = il * PEER_NKEYS - p * hw
                a = a_t[r0:r0 + PEER_NKEYS, ts]
                act = 0.5 * a * (1.0 + lax.erf(a * (2.0 ** -0.5))) * g
                rows.append(act.T.astype(BF16))
            cols.append(jnp.concatenate(rows, axis=0))
        return jnp.concatenate(cols, axis=1)

    a_parts = [scores(p) for p in range(EXPERT_SPLIT)]
    for p in range(EXPERT_SPLIT):
        o_ref[...] += jnp.dot(activation(p, a_parts[p]), v_ref[p * hw:(p + 1) * hw, :],
                              preferred_element_type=F32)


def _peer_dense(h2, u_tab, v_tab, st, stats):
    ntok, d = h2.shape
    nexp = u_tab.shape[0]
    tm, te = _tile(ntok, 512), 4 * PEER_NKEYS
    nhp = st.shape[0]
    return pl.pallas_call(
        _peer_dense_kernel,
        grid=(ntok // tm, nexp // te),
        in_specs=[pl.BlockSpec((tm, d), lambda i, e: (i, 0)),
                  pl.BlockSpec((te, d), lambda i, e: (e, 0)),
                  pl.BlockSpec((te, d), lambda i, e: (e, 0)),
                  pl.BlockSpec((nhp, PEER_NKEYS, tm), lambda i, e: (0, 0, i)),
                  pl.BlockSpec((PEER_HEADS, 8, tm), lambda i, e: (0, 0, i))],
        out_specs=pl.BlockSpec((tm, d), lambda i, e: (i, 0)),
        out_shape=jax.ShapeDtypeStruct((ntok, d), F32),
        scratch_shapes=[pltpu.VMEM((PEER_HEADS, PEER_NKEYS, tm), F32)],
        compiler_params=_params(("parallel", "arbitrary")),
        name="peer_dense",
    )(h2, u_tab, v_tab, st, stats)


def _final_kernel(x_ref, p_ref, mod_ref, g_ref, o_ref, *, normalize):
    y = x_ref[...] + mod_ref[5:6, :] * p_ref[...]
    if normalize:
        y = (y * lax.rsqrt(jnp.mean(y * y, axis=-1, keepdims=True) + NORM_EPS)) * g_ref[...]
    o_ref[...] = y


def _final(x1, peer_out, mod3, g, normalize):
    nbatch, seq, d = x1.shape
    tm = _tile(seq, 256)
    spec = pl.BlockSpec((None, tm, d), lambda b, i: (b, i, 0))
    return pl.pallas_call(
        functools.partial(_final_kernel, normalize=normalize),
        grid=(nbatch, seq // tm),
        in_specs=[spec, spec,
                  pl.BlockSpec((None, ADA_CHUNKS, d), lambda b, i: (b, 0, 0)),
                  pl.BlockSpec((1, d), lambda b, i: (0, 0))],
        out_specs=spec,
        out_shape=jax.ShapeDtypeStruct((nbatch, seq, d), F32),
        compiler_params=_params(("parallel", "parallel")),
        name="final",
    )(x1, peer_out, mod3, g.reshape(1, d))


def kernel(x, c, positions, ln1_g, ln2_g, w_ada, b_ada, w_in, w_out, peer_wq, peer_sub_keys, peer_u,
           peer_v, lnf_g):
    nbatch, seq, d = x.shape
    depth = w_ada.shape[0]
    tables = _rope_tables(positions)
    for layer in range(depth):
        mod3 = _ada(c, w_ada[layer], b_ada[layer]).reshape(nbatch, ADA_CHUNKS, d)
        pa, proj, aux = _in_proj(x, mod3, ln1_g[layer], _prep_w_in(w_in[layer]), tables)
        o_a = _attn_a(pa)
        o_b, (w_out_b, w_q_b, u_b, v_b) = _dsa(
            proj, aux, (w_out[layer], peer_wq[layer], peer_u[layer], peer_v[layer]))
        x = _out_proj(o_a, o_b, w_out_b, x, mod3)
        keys = peer_sub_keys[layer].reshape(2 * PEER_HEADS, PEER_NKEYS, PEER_HALF)
        h2, st = _peer_q(x, mod3, ln2_g[layer], w_q_b, keys)
        stats = _peer_topk(st)
        peer_out = _peer_dense(h2.reshape(nbatch * seq, d), u_b, v_b, st, stats)
        x = _final(x, peer_out.reshape(nbatch, seq, d), mod3, lnf_g, normalize=layer + 1 == depth)
    return x
```

```python
import functools
import math

import jax
import jax.numpy as jnp
from jax import lax
from jax.experimental import pallas as pl
from jax.experimental.pallas import tpu as pltpu

F32 = jnp.float32
BF16 = jnp.bfloat16

HEAD_DIM = 128
A_HEADS = 16
A_KV_HEADS = 4
A_GROUP = A_HEADS // A_KV_HEADS
DILATED_BRANCHES = ((128, 1), (512, 4), (2048, 16))
WIN_BLK = 128
B_HEADS = 16
IDX_HEADS = 16
IDX_DIM = 64
DSA_TOPK_MAX = 256
Q_BLK = 128
PEER_HEADS = 8
PEER_NKEYS = 128
PEER_HALF = 128
PEER_TOPK = 16
ROPE_THETA = 10000.0
NORM_EPS = 1e-6
NEG = -1e30
ADA_CHUNKS = 6
INT_MIN = -(2 ** 31)

LANES = 128
VMEM_LIMIT = 56 * 1024 * 1024

PROJ_TILE = 512
IN_COLS = ((A_HEADS + 2 * A_KV_HEADS + B_HEADS + 2) * HEAD_DIM + IDX_HEADS * IDX_DIM + IDX_DIM + IDX_HEADS)
PROJ_TILES = -(-IN_COLS // PROJ_TILE)
A_TILES = (A_HEADS + 2 * A_KV_HEADS) * HEAD_DIM // PROJ_TILE
KA_TILE = A_HEADS * HEAD_DIM // PROJ_TILE
VA_TILE = KA_TILE + 1
PA_HEADS = A_HEADS + 2 * A_KV_HEADS
QB_OFF = 0
KB_OFF = QB_OFF + B_HEADS * HEAD_DIM
VB_OFF = KB_OFF + HEAD_DIM
QI_OFF = VB_OFF + HEAD_DIM
KIKI_OFF = QI_OFF + IDX_HEADS * IDX_DIM
P_COLS = (PROJ_TILES - A_TILES) * PROJ_TILE
WI_LANE = IDX_DIM


def _params(semantics):
    return pltpu.CompilerParams(dimension_semantics=semantics, vmem_limit_bytes=VMEM_LIMIT)


def _tile(n, pref):
    return pref if n % pref == 0 else n


def _norm_mod(x, g, scale, shift):
    xf = x.astype(F32)
    y = xf * lax.rsqrt(jnp.mean(xf * xf, axis=-1, keepdims=True) + NORM_EPS)
    return (y * g) * (1.0 + scale) + shift


def _ada_kernel(ct_ref, w_ref, b_ref, o_ref, *, nbatch):
    @pl.when(pl.program_id(1) == 0)
    def _():
        o_ref[...] = jnp.broadcast_to(b_ref[...], o_ref.shape)

    ct = ct_ref[...]
    s = ct * jax.nn.sigmoid(ct)
    w = w_ref[...]
    rows = [jnp.sum(w * s[:, b:b + 1], axis=0, keepdims=True) for b in range(nbatch)]
    o_ref[...] += jnp.concatenate(rows, axis=0)


def _ada(c, w_ada, b_ada):
    nbatch, d = c.shape
    n = w_ada.shape[1]
    tk, tn = _tile(d, 1024), _tile(n, 2048)
    return pl.pallas_call(
        functools.partial(_ada_kernel, nbatch=nbatch),
        grid=(n // tn, d // tk),
        in_specs=[pl.BlockSpec((tk, nbatch), lambda j, k: (k, 0)),
                  pl.BlockSpec((tk, tn), lambda j, k: (k, j)),
                  pl.BlockSpec((1, tn), lambda j, k: (0, j))],
        out_specs=pl.BlockSpec((nbatch, tn), lambda j, k: (0, j)),
        out_shape=jax.ShapeDtypeStruct((nbatch, n), F32),
        compiler_params=_params(("parallel", "arbitrary")),
        name="ada",
    )(c.T, w_ada, b_ada.reshape(1, n))


def _rope_kernel(pos_ref, ch_ref, sh_ref, ci_ref, sa_ref, sb_ref):
    pos = pos_ref[...].astype(F32)
    lane = lax.broadcasted_iota(jnp.int32, (1, LANES), 1)
    expo_h = -((2 * (lane & 63)).astype(F32)) / HEAD_DIM
    expo_i = -((2 * (lane & 31)).astype(F32)) / IDX_DIM
    inv = jnp.power(ROPE_THETA, jnp.where(lane < 64, expo_h, expo_i))
    ang = pos * inv
    c = jnp.cos(ang)
    s = jnp.sin(ang)
    ch_ref[...] = jnp.where(lane < 64, c, pltpu.roll(c, 64, 1))
    sh_ref[...] = jnp.where(lane < 64, -s, pltpu.roll(s, 64, 1))
    grp = lane >> 5
    c64, c96, c32 = pltpu.roll(c, 64, 1), pltpu.roll(c, 96, 1), pltpu.roll(c, 32, 1)
    s64, s96, s32 = pltpu.roll(s, 64, 1), pltpu.roll(s, 96, 1), pltpu.roll(s, 32, 1)
    ci_ref[...] = jnp.where(grp == 0, c64, jnp.where(grp == 1, c96, jnp.where(grp == 2, c, c32)))
    sa_ref[...] = jnp.where(grp == 0, -s64, jnp.where(grp == 2, -s, 0.0))
    sb_ref[...] = jnp.where(grp == 1, s96, jnp.where(grp == 3, s32, 0.0))


def _rope_tables(positions):
    nbatch, seq = positions.shape
    ts = _tile(seq, 1024)
    spec = pl.BlockSpec((None, ts, LANES), lambda b, i: (b, i, 0))
    shape = jax.ShapeDtypeStruct((nbatch, seq, LANES), F32)
    return pl.pallas_call(
        _rope_kernel,
        grid=(nbatch, seq // ts),
        in_specs=[pl.BlockSpec((None, ts, 1), lambda b, i: (b, i, 0))],
        out_specs=[spec] * 5,
        out_shape=[shape] * 5,
        compiler_params=_params(("parallel", "parallel")),
        name="rope",
    )(positions.reshape(nbatch, seq, 1))


def _rope_head(a, ch, sh):
    return a * ch + pltpu.roll(a, 64, 1) * sh


def _rope_idx(a, ci, sa, sb):
    return a * ci + pltpu.roll(a, 96, 1) * sa + pltpu.roll(a, 32, 1) * sb


def _in_proj_kernel(x_ref, mod_ref, g_ref, w_ref, wt_ref, ch_ref, sh_ref, ci_ref, sa_ref, sb_ref,
                    pa_ref, p_ref, aux_ref, h_scr):
    j = pl.program_id(2)

    @pl.when(j == 0)
    def _():
        h_scr[...] = _norm_mod(x_ref[...], g_ref[...], mod_ref[1:2, :], mod_ref[0:1, :]).astype(BF16)

    nblk = PROJ_TILE // LANES
    qscale = HEAD_DIM ** -0.5 * math.log2(math.e)
    qb_tile = A_TILES + QB_OFF // PROJ_TILE
    mix_tile = A_TILES + KB_OFF // PROJ_TILE
    last_tile = PROJ_TILES - 1

    def project(w_tile_ref):
        width = 2 * LANES
        halves = [lax.dot_general(h_scr[...], w_tile_ref[p * width:(p + 1) * width, :],
                                  (((1,), (1,)), ((), ())), preferred_element_type=F32)
                  for p in range(PROJ_TILE // width)]

        def blk(q):
            return halves[q // 2][:, (q % 2) * LANES:(q % 2 + 1) * LANES]
        return blk

    def put(q, val):
        p_ref[:, q * LANES:(q + 1) * LANES] = val.astype(BF16)

    def head(a):
        return _rope_head(a, ch_ref[...], sh_ref[...])

    def idx(a):
        return _rope_idx(a, ci_ref[...], sa_ref[...], sb_ref[...])

    @pl.when(j < KA_TILE)
    def _():
        blk = project(w_ref)
        for q in range(nblk):
            pa_ref[q] = head(blk(q)) * qscale

    @pl.when(j == KA_TILE)
    def _():
        blk = project(w_ref)
        for q in range(nblk):
            pa_ref[q] = head(blk(q))

    @pl.when(j == VA_TILE)
    def _():
        blk = project(w_ref)
        for q in range(nblk):
            pa_ref[q] = blk(q)

    @pl.when((j >= qb_tile) & (j < mix_tile))
    def _():
        blk = project(w_ref)
        for q in range(nblk):
            put(q, head(blk(q)) * qscale)

    @pl.when(j == mix_tile)
    def _():
        blk = project(w_ref)
        put(0, head(blk(0)))
        put(1, blk(1))
        put(2, idx(blk(2)))
        put(3, idx(blk(3)))

    @pl.when((j > mix_tile) & (j < last_tile))
    def _():
        blk = project(w_ref)
        for q in range(nblk):
            put(q, idx(blk(q)))

    @pl.when(j == last_tile)
    def _():
        blk = project(wt_ref)
        put(0, idx(blk(0)))
        put(1, idx(blk(1)))
        raw = blk(2)
        ki = idx(raw)
        lane = lax.broadcasted_iota(jnp.int32, ki.shape, 1)
        put(2, jnp.where(lane < IDX_DIM, ki, pltpu.roll(ki, IDX_DIM, 1)))
        put(3, blk(3))
        aux_ref[...] = raw


def _prep_w_in(w_in):
    assert w_in.shape[1] == IN_COLS
    w_t = jnp.swapaxes(w_in, 0, 1).astype(BF16)
    full = (PROJ_TILES - 1) * PROJ_TILE
    return w_t, jnp.pad(w_t[full:], ((0, PROJ_TILES * PROJ_TILE - IN_COLS), (0, 0)))


def _in_proj(x, mod3, g, w_parts, tables):
    nbatch, seq, d = x.shape
    tm = _tile(seq, 512)
    nblk = PROJ_TILE // LANES
    w_t, w_tail = w_parts
    tab_spec = pl.BlockSpec((None, tm, LANES), lambda b, i, j: (b, i, 0))
    return pl.pallas_call(
        _in_proj_kernel,
        grid=(nbatch, seq // tm, PROJ_TILES),
        in_specs=[pl.BlockSpec((None, tm, d), lambda b, i, j: (b, i, 0)),
                  pl.BlockSpec((None, ADA_CHUNKS, d), lambda b, i, j: (b, 0, 0)),
                  pl.BlockSpec((1, d), lambda b, i, j: (0, 0)),
                  pl.BlockSpec((PROJ_TILE, d), lambda b, i, j: (jnp.minimum(j, PROJ_TILES - 2), 0)),
                  pl.BlockSpec((PROJ_TILE, d), lambda b, i, j: (0, 0))] + [tab_spec] * 5,
        out_specs=[pl.BlockSpec((None, nblk, tm, LANES), lambda b, i, j: (b, jnp.minimum(j, A_TILES - 1), i, 0)),
                   pl.BlockSpec((None, tm, PROJ_TILE), lambda b, i, j: (b, i, jnp.maximum(j - A_TILES, 0))),
                   pl.BlockSpec((None, tm, LANES), lambda b, i, j: (b, i, 0))],
        out_shape=[jax.ShapeDtypeStruct((nbatch, PA_HEADS, seq, LANES), F32),
                   jax.ShapeDtypeStruct((nbatch, seq, P_COLS), BF16),
                   jax.ShapeDtypeStruct((nbatch, seq, LANES), F32)],
        scratch_shapes=[pltpu.VMEM((tm, d), BF16)],
        compiler_params=_params(("parallel", "parallel", "arbitrary")),
        name="in_proj",
    )(x, mod3, g.reshape(1, d), w_t, w_tail, *tables)


def _attn_a_kernel(q_ref, k_ref, v_ref, o_ref, acc_scr, m_scr, l_scr, bias_scr, *, dilations):
    half = pl.program_id(2)
    qrows = q_ref.shape[1]
    nblk = qrows // WIN_BLK

    rows = A_GROUP * WIN_BLK
    row = lax.broadcasted_iota(jnp.int32, (rows, 2 * WIN_BLK), 0) & (WIN_BLK - 1)
    col = lax.broadcasted_iota(jnp.int32, (rows, 2 * WIN_BLK), 1)
    band = (col >= row) & (col <= row + WIN_BLK)
    bias_scr[0] = jnp.where(band & (col >= WIN_BLK), 0.0, NEG)
    bias_scr[1] = jnp.where(band, 0.0, NEG)

    def rows_at(start, dil):
        return pl.ds(start, WIN_BLK) if dil == 1 else pl.ds(start, WIN_BLK, stride=dil)

    for idx, dil in enumerate(dilations):
        per_res = nblk // dil
        first, last = idx == 0, idx == len(dilations) - 1

        def body(blk, carry, dil=dil, per_res=per_res, first=first, last=last):
            r = blk // per_res
            n_loc = blk - r * per_res
            n = half * per_res + n_loc
            q0 = r + dil * WIN_BLK * n_loc
            k0 = r + dil * WIN_BLK * n
            kp = r + dil * WIN_BLK * jnp.maximum(n - 1, 0)
            q4 = jnp.concatenate([q_ref[g, rows_at(q0, dil), :] for g in range(A_GROUP)], axis=0).astype(BF16)
            kc = jnp.concatenate([k_ref[rows_at(kp, dil), :], k_ref[rows_at(k0, dil), :]], axis=0).astype(BF16)
            vc = jnp.concatenate([v_ref[rows_at(kp, dil), :], v_ref[rows_at(k0, dil), :]], axis=0).astype(BF16)
            s = lax.dot_general(q4, kc, (((1,), (1,)), ((), ())), preferred_element_type=F32)
            s = s + bias_scr[jnp.minimum(n, 1)]
            m = jnp.max(s, axis=-1, keepdims=True)
            p = jnp.exp2(s - m)
            l = jnp.sum(p, axis=-1, keepdims=True)
            o = jnp.dot(p.astype(BF16), vc, preferred_element_type=F32)
            for g in range(A_GROUP):
                og = o[g * WIN_BLK:(g + 1) * WIN_BLK]
                mg = jnp.broadcast_to(m[g * WIN_BLK:(g + 1) * WIN_BLK], (WIN_BLK, HEAD_DIM))
                lg = jnp.broadcast_to(l[g * WIN_BLK:(g + 1) * WIN_BLK], (WIN_BLK, HEAD_DIM))
                if not first:
                    mp = m_scr[g, rows_at(q0, dil), :]
                    decay = jnp.exp2(-jnp.abs(mg - mp))
                    w_new = jnp.where(mg >= mp, 1.0, decay)
                    w_old = jnp.where(mg >= mp, decay, 1.0)
                    og = acc_scr[g, rows_at(q0, dil), :] * w_old + og * w_new
                    lg = l_scr[g, rows_at(q0, dil), :] * w_old + lg * w_new
                    mg = jnp.maximum(mg, mp)
                if last:
                    o_ref[pl.ds(pl.multiple_of(q0, WIN_BLK), WIN_BLK), g * HEAD_DIM:(g + 1) * HEAD_DIM] = (
                        (og / lg).astype(o_ref.dtype))
                else:
                    acc_scr[g, rows_at(q0, dil), :] = og
                    m_scr[g, rows_at(q0, dil), :] = mg
                    l_scr[g, rows_at(q0, dil), :] = lg
            return carry

        lax.fori_loop(0, nblk, body, 0, unroll=4)


def _attn_a(pa):
    nbatch, _, seq, _ = pa.shape
    dilations = tuple(sorted((dil for _, dil in DILATED_BRANCHES), reverse=True))
    assert dilations[-1] == 1 and all(w // dil == WIN_BLK for w, dil in DILATED_BRANCHES)
    nhalf = 2 if seq % (2 * WIN_BLK * dilations[0]) == 0 else 1
    qrows = seq // nhalf
    assert qrows % (WIN_BLK * dilations[0]) == 0
    return pl.pallas_call(
        functools.partial(_attn_a_kernel, dilations=dilations),
        grid=(nbatch, A_KV_HEADS, nhalf),
        in_specs=[pl.BlockSpec((None, A_GROUP, qrows, HEAD_DIM), lambda b, h, t: (b, h, t, 0)),
                  pl.BlockSpec((None, None, seq, HEAD_DIM), lambda b, h, t: (b, A_HEADS + h, 0, 0)),
                  pl.BlockSpec((None, None, seq, HEAD_DIM),
                               lambda b, h, t: (b, A_HEADS + A_KV_HEADS + h, 0, 0))],
        out_specs=pl.BlockSpec((None, qrows, A_GROUP * HEAD_DIM), lambda b, h, t: (b, t, h)),
        out_shape=jax.ShapeDtypeStruct((nbatch, seq, A_HEADS * HEAD_DIM), BF16),
        scratch_shapes=[pltpu.VMEM((A_GROUP, qrows, HEAD_DIM), F32),
                        pltpu.VMEM((A_GROUP, qrows, HEAD_DIM), F32),
                        pltpu.VMEM((A_GROUP, qrows, HEAD_DIM), F32),
                        pltpu.VMEM((2, A_GROUP * WIN_BLK, 2 * WIN_BLK), F32)],
        compiler_params=_params(("parallel", "parallel", "arbitrary")),
        name="attn_a",
    )(pa, pa, pa)


KEY_CHUNK = 256


def _ordered_to_float(key):
    return pltpu.bitcast(key ^ ((key >> 31) & jnp.int32(0x7FFFFFFF)), F32)


LOWEST_FINITE_KEY = INT_MIN + 0x00800000


def _dsa_kernel(p_ref, wi_ref, k_ref, kk_ref, v_ref, *rest, topk, ncast):
    cast_in, rest = rest[:ncast], rest[ncast:]
    o_ref, cast_out = rest[0], rest[1:1 + ncast]
    sc_scr, wib_scr, q_scr, p_scr, m_scr, l_scr, acc_scr = rest[1 + ncast:]
    i = pl.program_id(1)
    nchunks = (i + 2) // 2
    halves = KEY_CHUNK // LANES

    for src, dst in zip(cast_in, cast_out):
        dst[...] = src[...].astype(dst.dtype)

    wi = wi_ref[...] * (IDX_HEADS ** -0.5 * IDX_DIM ** -0.5)
    lane = lax.broadcasted_iota(jnp.int32, (Q_BLK, LANES), 1)
    for h in range(IDX_HEADS):
        blk = p_ref[:, QI_OFF + (h // 2) * LANES:QI_OFF + (h // 2 + 1) * LANES]
        keep = (lane < IDX_DIM) if h % 2 == 0 else (lane >= IDX_DIM)
        q_scr[h * Q_BLK:(h + 1) * Q_BLK, :] = jnp.where(keep, blk, jnp.zeros_like(blk))
        wib_scr[h] = jnp.broadcast_to(wi[:, WI_LANE + h:WI_LANE + h + 1], (Q_BLK, LANES))
    qpos = i * Q_BLK + lax.broadcasted_iota(jnp.int32, (Q_BLK, KEY_CHUNK), 0)
    kiota = lax.broadcasted_iota(jnp.int32, (Q_BLK, KEY_CHUNK), 1)

    def score_chunk(c, carry):
        k0 = pl.multiple_of(c * KEY_CHUNK, KEY_CHUNK)
        z = lax.dot_general(q_scr[...], kk_ref[pl.ds(k0, KEY_CHUNK), :], (((1,), (1,)), ((), ())),
                            preferred_element_type=F32)
        sc = jnp.zeros((Q_BLK, KEY_CHUNK), F32)
        for h in range(IDX_HEADS):
            w = wib_scr[h]
            sc = sc + jnp.concatenate([w] * halves, axis=1) * jnp.maximum(z[h * Q_BLK:(h + 1) * Q_BLK], 0.0)
        sc_scr[c] = jnp.where(k0 + kiota <= qpos, sc, -jnp.inf)
        return carry

    lax.fori_loop(0, nchunks, score_chunk, 0)

    def bit_body(t, ans):
        trial = ans | jnp.left_shift(jnp.int32(1), 31 - t)
        thr = _ordered_to_float(trial ^ jnp.int32(INT_MIN))

        def count_chunk(c, acc):
            sc = sc_scr[c]
            for q in range(halves):
                acc = acc + jnp.where(sc[:, q * LANES:(q + 1) * LANES] >= thr, 1.0, 0.0)
            return acc

        acc = lax.fori_loop(0, nchunks, count_chunk, jnp.zeros((Q_BLK, LANES), F32))
        cnt = jnp.sum(acc, axis=1, keepdims=True)
        return jnp.where(cnt >= float(topk), trial, ans)

    ans = lax.fori_loop(0, 32, bit_body, jnp.zeros((Q_BLK, LANES), jnp.int32))
    tau = _ordered_to_float(jnp.maximum(ans ^ jnp.int32(INT_MIN), jnp.int32(LOWEST_FINITE_KEY)))
    tau2 = jnp.concatenate([tau] * halves, axis=1)

    for h in range(B_HEADS):
        q_scr[h * Q_BLK:(h + 1) * Q_BLK, :] = p_ref[:, QB_OFF + h * HEAD_DIM:QB_OFF + (h + 1) * HEAD_DIM]
    m_scr[...] = jnp.full(m_scr.shape, NEG, F32)
    l_scr[...] = jnp.zeros(l_scr.shape, F32)
    acc_scr[...] = jnp.zeros(acc_scr.shape, F32)

    def att_chunk(c, carry):
        k0 = pl.multiple_of(c * KEY_CHUNK, KEY_CHUNK)
        s = lax.dot_general(q_scr[...], k_ref[pl.ds(k0, KEY_CHUNK), :], (((1,), (1,)), ((), ())),
                            preferred_element_type=F32)
        bias = jnp.where(sc_scr[c] >= tau2, 0.0, NEG)
        for h in range(B_HEADS):
            hr = slice(h * Q_BLK, (h + 1) * Q_BLK)
            sh = s[hr] + bias
            m_old = m_scr[hr, :]
            m_new = jnp.maximum(m_old, jnp.max(sh, axis=-1, keepdims=True))
            alpha = jnp.exp2(m_old - m_new)
            p = jnp.exp2(sh - jnp.concatenate([m_new] * halves, axis=1))
            psum = p[:, :LANES]
            for q in range(1, halves):
                psum = psum + p[:, q * LANES:(q + 1) * LANES]
            l_scr[hr, :] = alpha * l_scr[hr, :] + psum
            m_scr[hr, :] = m_new
            p_scr[hr, :] = p.astype(BF16)
            acc_scr[hr, :] = alpha * acc_scr[hr, :]
        acc_scr[...] += jnp.dot(p_scr[...], v_ref[pl.ds(k0, KEY_CHUNK), :], preferred_element_type=F32)
        return carry

    lax.fori_loop(0, nchunks, att_chunk, 0)
    for h in range(B_HEADS):
        hr = slice(h * Q_BLK, (h + 1) * Q_BLK)
        l = jnp.sum(l_scr[hr, :], axis=-1, keepdims=True)
        o_ref[:, h * HEAD_DIM:(h + 1) * HEAD_DIM] = (acc_scr[hr, :] / l).astype(o_ref.dtype)


def _dsa(proj, aux, f32_weights):
    nbatch, seq, _ = proj.shape
    assert seq % KEY_CHUNK == 0
    topk = min(DSA_TOPK_MAX, seq // 4)
    qw = B_HEADS * HEAD_DIM
    nq = seq // Q_BLK
    steps = nbatch * nq
    full = lambda off: pl.BlockSpec((None, seq, LANES), lambda b, i: (b, 0, off // LANES))
    cast_specs = []
    for w in f32_weights:
        assert w.shape[0] % steps == 0
        cast_specs.append(pl.BlockSpec((w.shape[0] // steps, w.shape[1]), lambda b, i: (b * nq + i, 0)))
    outs = pl.pallas_call(
        functools.partial(_dsa_kernel, topk=topk, ncast=len(f32_weights)),
        grid=(nbatch, nq),
        in_specs=[pl.BlockSpec((None, Q_BLK, P_COLS), lambda b, i: (b, i, 0)),
                  pl.BlockSpec((None, Q_BLK, LANES), lambda b, i: (b, i, 0)),
                  full(KB_OFF), full(KIKI_OFF), full(VB_OFF)] + cast_specs,
        out_specs=[pl.BlockSpec((None, Q_BLK, qw), lambda b, i: (b, i, 0))] + cast_specs,
        out_shape=[jax.ShapeDtypeStruct((nbatch, seq, qw), BF16)]
        + [jax.ShapeDtypeStruct(w.shape, BF16) for w in f32_weights],
        scratch_shapes=[pltpu.VMEM((seq // KEY_CHUNK, Q_BLK, KEY_CHUNK), F32),
                        pltpu.VMEM((IDX_HEADS, Q_BLK, LANES), F32),
                        pltpu.VMEM((B_HEADS * Q_BLK, HEAD_DIM), BF16),
                        pltpu.VMEM((B_HEADS * Q_BLK, KEY_CHUNK), BF16),
                        pltpu.VMEM((B_HEADS * Q_BLK, LANES), F32),
                        pltpu.VMEM((B_HEADS * Q_BLK, LANES), F32),
                        pltpu.VMEM((B_HEADS * Q_BLK, HEAD_DIM), F32)],
        compiler_params=_params(("parallel", "arbitrary")),
        name="dsa",
    )(proj, aux, proj, proj, proj, *f32_weights)
    return outs[0], outs[1:]


def _out_proj_kernel(oa_ref, ob_ref, w_ref, x_ref, mod_ref, o_ref):
    ka = oa_ref.shape[-1]
    y = jnp.dot(oa_ref[...], w_ref[:ka, :], preferred_element_type=F32)
    y = y + jnp.dot(ob_ref[...], w_ref[ka:, :], preferred_element_type=F32)
    o_ref[...] = x_ref[...] + mod_ref[2:3, :] * y


def _out_proj(o_a, o_b, w_out, x, mod3):
    nbatch, seq, d = x.shape
    ka, kb = o_a.shape[-1], o_b.shape[-1]
    tm, tn = _tile(seq, 1024), _tile(d, 1024)
    return pl.pallas_call(
        _out_proj_kernel,
        grid=(nbatch, seq // tm, d // tn),
        in_specs=[pl.BlockSpec((None, tm, ka), lambda b, i, j: (b, i, 0)),
                  pl.BlockSpec((None, tm, kb), lambda b, i, j: (b, i, 0)),
                  pl.BlockSpec((ka + kb, tn), lambda b, i, j: (0, j)),
                  pl.BlockSpec((None, tm, tn), lambda b, i, j: (b, i, j)),
                  pl.BlockSpec((None, ADA_CHUNKS, tn), lambda b, i, j: (b, 0, j))],
        out_specs=pl.BlockSpec((None, tm, tn), lambda b, i, j: (b, i, j)),
        out_shape=jax.ShapeDtypeStruct((nbatch, seq, d), F32),
        compiler_params=_params(("parallel", "parallel", "parallel")),
        name="out_proj",
    )(o_a, o_b, w_out, x, mod3)


def _peer_q_kernel(x_ref, mod_ref, g_ref, w_ref, keys_ref, h_ref, st_ref, h_scr):
    j = pl.program_id(2)

    @pl.when(j == 0)
    def _():
        h = _norm_mod(x_ref[...], g_ref[...], mod_ref[4:5, :], mod_ref[3:4, :]).astype(BF16)
        h_scr[...] = h
        h_ref[...] = h

    q = jnp.dot(h_scr[...], w_ref[...], preferred_element_type=F32)
    for blk in range(q.shape[1] // PEER_HALF):
        qh = q[:, blk * PEER_HALF:(blk + 1) * PEER_HALF]
        st_ref[blk] = lax.dot_general(keys_ref[blk], qh, (((1,), (1,)), ((), ())),
                                      preferred_element_type=F32, precision=lax.Precision.HIGHEST)


def _peer_q(x1, mod3, g, w_q, keys):
    nbatch, seq, d = x1.shape
    nq = w_q.shape[1]
    tm, tn = _tile(seq, 512), 4 * PEER_HALF
    nhp = nq // PEER_HALF
    nt = seq // tm
    return pl.pallas_call(
        _peer_q_kernel,
        grid=(nbatch, nt, nq // tn),
        in_specs=[pl.BlockSpec((None, tm, d), lambda b, i, j: (b, i, 0)),
                  pl.BlockSpec((None, ADA_CHUNKS, d), lambda b, i, j: (b, 0, 0)),
                  pl.BlockSpec((1, d), lambda b, i, j: (0, 0)),
                  pl.BlockSpec((d, tn), lambda b, i, j: (0, j)),
                  pl.BlockSpec((tn // PEER_HALF, PEER_NKEYS, PEER_HALF), lambda b, i, j: (j, 0, 0))],
        out_specs=[pl.BlockSpec((None, tm, d), lambda b, i, j: (b, i, 0)),
                   pl.BlockSpec((tn // PEER_HALF, PEER_NKEYS, tm), lambda b, i, j: (j, 0, b * nt + i))],
        out_shape=[jax.ShapeDtypeStruct((nbatch, seq, d), BF16),
                   jax.ShapeDtypeStruct((nhp, PEER_NKEYS, nbatch * seq), F32)],
        scratch_shapes=[pltpu.VMEM((tm, d), BF16)],
        compiler_params=_params(("parallel", "parallel", "arbitrary")),
        name="peer_q",
    )(x1, mod3, g.reshape(1, d), w_q, keys)


def _top_rows(x, count, one_at_a_time):
    rows = []
    ridx = lax.broadcasted_iota(jnp.int32, x.shape, 0)
    for _ in range(count):
        m = jnp.max(x, axis=0, keepdims=True)
        rows.append(m)
        hit = x == m
        if one_at_a_time:
            first = jnp.min(jnp.where(hit, ridx, x.shape[0]), axis=0, keepdims=True)
            hit = ridx == first
        x = jnp.where(hit, -jnp.inf, x)
    removed = jnp.sum(jnp.where(x == -jnp.inf, 1.0, 0.0), axis=0, keepdims=True)
    return rows, removed


def _peer_group_stats(x0, x1, one_at_a_time):
    half = PEER_TOPK // 2
    v0, r0 = _top_rows(x0, PEER_TOPK, one_at_a_time)
    v1, r1 = _top_rows(x1, PEER_TOPK, one_at_a_time)
    v0_all = jnp.concatenate(v0, axis=0)
    v1_all = jnp.concatenate(v1, axis=0)
    cand = [v0[0] + v1_all]
    cand += [v0[a] + v1_all[:half] for a in range(1, half)]
    cand += [v0_all[half:] + v1[0]]
    top, rc = _top_rows(jnp.concatenate(cand, axis=0), PEER_TOPK, one_at_a_time)
    z = jnp.zeros_like(top[0])
    for t in top:
        z = z + jnp.exp(t - top[0])
    stats = jnp.concatenate([top[-1], v0[0], v1[0], 1.0 / z] + [jnp.zeros_like(z)] * 4, axis=0)
    repeated = jnp.max(jnp.maximum(jnp.maximum(r0, r1), rc)) > float(PEER_TOPK)
    return stats, repeated


HEADS_PER_STEP = 4


def _peer_topk_kernel(st_ref, stats_ref):
    ngroups = st_ref.shape[-1] // LANES

    def group_stats(h, tg, one_at_a_time):
        cols = slice(tg * LANES, (tg + 1) * LANES)
        return _peer_group_stats(st_ref[2 * h, :, cols], st_ref[2 * h + 1, :, cols], one_at_a_time)

    def heads_step(hs, carry):
        work = [(HEADS_PER_STEP * hs + k, tg) for k in range(HEADS_PER_STEP) for tg in range(ngroups)]
        fast = [group_stats(h, tg, False) for h, tg in work]
        for (h, tg), (stats, _) in zip(work, fast):
            stats_ref[h, :, tg * LANES:(tg + 1) * LANES] = stats
        for (h, tg), (_, repeated) in zip(work, fast):
            @pl.when(repeated)
            def _(h=h, tg=tg):
                stats_ref[h, :, tg * LANES:(tg + 1) * LANES] = group_stats(h, tg, True)[0]
        return carry

    lax.fori_loop(0, PEER_HEADS // HEADS_PER_STEP, heads_step, 0)


def _peer_topk(st):
    nhp, nkeys, ntok = st.shape
    tmk = _tile(ntok, 256)
    return pl.pallas_call(
        _peer_topk_kernel,
        grid=(ntok // tmk,),
        in_specs=[pl.BlockSpec((nhp, nkeys, tmk), lambda i: (0, 0, i))],
        out_specs=pl.BlockSpec((PEER_HEADS, 8, tmk), lambda i: (0, 0, i)),
        out_shape=jax.ShapeDtypeStruct((PEER_HEADS, 8, ntok), F32),
        compiler_params=_params(("parallel",)),
        name="peer_topk",
    )(st)


EXPERT_SPLIT = 2


def _peer_dense_kernel(x_ref, u_ref, v_ref, st_ref, stats_ref, o_ref, e1_scr):
    e = pl.program_id(1)
    te, tm = u_ref.shape[0], x_ref.shape[0]
    ni = te // PEER_NKEYS
    hw = te // EXPERT_SPLIT

    @pl.when(e == 0)
    def _():
        o_ref[...] = jnp.zeros_like(o_ref)
        for h in range(PEER_HEADS):
            e1_scr[h] = jnp.exp(st_ref[2 * h + 1] - stats_ref[h, 2:3, :]) * stats_ref[h, 3:4, :]

    def scores(p):
        return lax.dot_general(u_ref[p * hw:(p + 1) * hw, :], x_ref[...], (((1,), (1,)), ((), ())),
                               preferred_element_type=F32)

    def activation(p, a_t):
        cols = []
        for il in range(p * hw // PEER_NKEYS, (p + 1) * hw // PEER_NKEYS):
            i = e * ni + il
            rows = []
            for tc in range(tm // LANES):
                ts = slice(tc * LANES, (tc + 1) * LANES)
                g = jnp.zeros((PEER_NKEYS, LANES), F32)
                for h in range(PEER_HEADS):
                    s0 = st_ref[2 * h, pl.ds(i, 1), :][:, ts]
                    e0 = jnp.exp(s0 - stats_ref[h, 1:2, ts])
                    sel = (s0 + st_ref[2 * h + 1, :, ts]) >= stats_ref[h, 0:1, ts]
                    g = g + jnp.where(sel, e0 * e1_scr[h, :, ts], 0.0)
                r0 = il * PEER_NKEYS - p * hw
                a = a_t[r0:r0 + PEER_NKEYS, ts]
                act = 0.5 * a * (1.0 + lax.erf(a * (2.0 ** -0.5))) * g
                rows.append(act.T.astype(BF16))
            cols.append(jnp.concatenate(rows, axis=0))
        return jnp.concatenate(cols, axis=1)

    a_parts = [scores(p) for p in range(EXPERT_SPLIT)]
    for p in range(EXPERT_SPLIT):
        o_ref[...] += jnp.dot(activation(p, a_parts[p]), v_ref[p * hw:(p + 1) * hw, :],
                              preferred_element_type=F32)


def _peer_dense(h2, u_tab, v_tab, st, stats):
    ntok, d = h2.shape
    nexp = u_tab.shape[0]
    tm, te = _tile(ntok, 512), 4 * PEER_NKEYS
    nhp = st.shape[0]
    return pl.pallas_call(
        _peer_dense_kernel,
        grid=(ntok // tm, nexp // te),
        in_specs=[pl.BlockSpec((tm, d), lambda i, e: (i, 0)),
                  pl.BlockSpec((te, d), lambda i, e: (e, 0)),
                  pl.BlockSpec((te, d), lambda i, e: (e, 0)),
                  pl.BlockSpec((nhp, PEER_NKEYS, tm), lambda i, e: (0, 0, i)),
                  pl.BlockSpec((PEER_HEADS, 8, tm), lambda i, e: (0, 0, i))],
        out_specs=pl.BlockSpec((tm, d), lambda i, e: (i, 0)),
        out_shape=jax.ShapeDtypeStruct((ntok, d), F32),
        scratch_shapes=[pltpu.VMEM((PEER_HEADS, PEER_NKEYS, tm), F32)],
        compiler_params=_params(("parallel", "arbitrary")),
        name="peer_dense",
    )(h2, u_tab, v_tab, st, stats)


def _final_kernel(x_ref, p_ref, mod_ref, g_ref, o_ref, *, normalize):
    y = x_ref[...] + mod_ref[5:6, :] * p_ref[...]
    if normalize:
        y = (y * lax.rsqrt(jnp.mean(y * y, axis=-1, keepdims=True) + NORM_EPS)) * g_ref[...]
    o_ref[...] = y


def _final(x1, peer_out, mod3, g, normalize):
    nbatch, seq, d = x1.shape
    tm = _tile(seq, 256)
    spec = pl.BlockSpec((None, tm, d), lambda b, i: (b, i, 0))
    return pl.pallas_call(
        functools.partial(_final_kernel, normalize=normalize),
        grid=(nbatch, seq // tm),
        in_specs=[spec, spec,
                  pl.BlockSpec((None, ADA_CHUNKS, d), lambda b, i: (b, 0, 0)),
                  pl.BlockSpec((1, d), lambda b, i: (0, 0))],
        out_specs=spec,
        out_shape=jax.ShapeDtypeStruct((nbatch, seq, d), F32),
        compiler_params=_params(("parallel", "parallel")),
        name="final",
    )(x1, peer_out, mod3, g.reshape(1, d))


def kernel(x, c, positions, ln1_g, ln2_g, w_ada, b_ada, w_in, w_out, peer_wq, peer_sub_keys, peer_u,
           peer_v, lnf_g):
    nbatch, seq, d = x.shape
    depth = w_ada.shape[0]
    tables = _rope_tables(positions)
    for layer in range(depth):
        mod3 = _ada(c, w_ada[layer], b_ada[layer]).reshape(nbatch, ADA_CHUNKS, d)
        pa, proj, aux = _in_proj(x, mod3, ln1_g[layer], _prep_w_in(w_in[layer]), tables)
        o_a = _attn_a(pa)
        o_b, (w_out_b, w_q_b, u_b, v_b) = _dsa(
            proj, aux, (w_out[layer], peer_wq[layer], peer_u[layer], peer_v[layer]))
        x = _out_proj(o_a, o_b, w_out_b, x, mod3)
        keys = peer_sub_keys[layer].reshape(2 * PEER_HEADS, PEER_NKEYS, PEER_HALF)
        h2, st = _peer_q(x, mod3, ln2_g[layer], w_q_b, keys)
        stats = _peer_topk(st)
        peer_out = _peer_dense(h2.reshape(nbatch * seq, d), u_b, v_b, st, stats)
        x = _final(x, peer_out.reshape(nbatch, seq, d), mod3, lnf_g, normalize=layer + 1 == depth)
    return x
```
